```python
import math
import jax, jax.numpy as jnp
from jax import lax
import numpy as np

D_MODEL = 1024
BATCH = 4
SEQ = 4096
DEPTH = 1

A_HEADS = 8
A_HEAD_DIM = 64
B_PATTERNS = ((128, 1), (512, 4), (2048, 16))
B_HEADS = 8
B_HEAD_DIM = 64
Q_BLOCK = 128
D_FF = 2816
CONV_WIDTH = 3
EPS = 1e-5
ALPHA = (2.0 * DEPTH) ** 0.25
BETA = (8.0 * DEPTH) ** -0.25

A_Q = 2 * A_HEADS * A_HEAD_DIM
A_K = 2 * A_HEADS * A_HEAD_DIM
A_V = A_HEADS * 2 * A_HEAD_DIM
B_QKV = 3 * len(B_PATTERNS) * B_HEADS * B_HEAD_DIM
GATE_COLS = 2 * D_MODEL
IN_COLS = A_Q + A_K + A_V + B_QKV + GATE_COLS
SPLITS = [A_Q, A_Q + A_K, A_Q + A_K + A_V, A_Q + A_K + A_V + B_QKV]

kernel_name = "hybrid_diffattn_dilated_deepnorm_block"


def lambda_init_fn(layer_idx):
    return 0.8 - 0.6 * math.exp(-0.3 * layer_idx)


def alibi_slopes(n):
    return jnp.asarray(np.power(np.float32(2.0), -8.0 * (np.arange(n, dtype=np.float32) + 1) / n), jnp.float32)


def layer_norm(x, g, b):
    xf = x.astype(jnp.float32)
    mu = jnp.mean(xf, axis=-1, keepdims=True)
    var = jnp.mean(jnp.square(xf - mu), axis=-1, keepdims=True)
    y = (xf - mu) * lax.rsqrt(var + EPS) * g.astype(jnp.float32) + b.astype(jnp.float32)
    return y.astype(x.dtype)


def diff_attention(q, k, v, lam, subln_w, lambda_init):
    bsz, seq = q.shape[0], q.shape[1]
    nblk = seq // Q_BLOCK
    slopes = jnp.repeat(alibi_slopes(A_HEADS), 2)
    qb = (q * A_HEAD_DIM ** -0.5).reshape(bsz, nblk, Q_BLOCK, 2 * A_HEADS, A_HEAD_DIM)
    qb = qb.transpose(1, 0, 3, 2, 4)
    kpos = jnp.arange(seq)

    def one_block(args):
        blk, qblk = args
        s = jnp.einsum('bmqd,bkmd->bmqk', qblk, k).astype(jnp.float32)
        dist = (blk * Q_BLOCK + jnp.arange(Q_BLOCK))[:, None] - kpos[None, :]
        s = jnp.where(dist >= 0, s - slopes[:, None, None] * dist, -jnp.inf)
        p = jax.nn.softmax(s, axis=-1).reshape(bsz, A_HEADS, 2, Q_BLOCK, seq)
        a = p[:, :, 0] - lam * p[:, :, 1]
        return jnp.einsum('bhqk,bkhe->bqhe', a.astype(v.dtype), v)

    o = lax.map(one_block, (jnp.arange(nblk), qb))
    o = o.transpose(1, 0, 2, 3, 4).reshape(bsz, seq, A_HEADS, 2 * A_HEAD_DIM).astype(jnp.float32)
    o = o * lax.rsqrt(jnp.mean(jnp.square(o), axis=-1, keepdims=True) + EPS)
    o = o * subln_w.astype(jnp.float32) * (1.0 - lambda_init)
    return o.reshape(bsz, seq, A_HEADS * 2 * A_HEAD_DIM).astype(q.dtype)


def dilated_group(q, k, v, slopes, window, dil):
    bsz, seq, nh, dh = q.shape
    steps = window // dil
    span = dil * steps
    s_pad = -(-seq // span) * span
    padw = ((0, 0), (0, s_pad - seq), (0, 0), (0, 0))
    q, k, v = jnp.pad(q, padw), jnp.pad(k, padw), jnp.pad(v, padw)
    nb = s_pad // span
    split = lambda t: t.reshape(bsz, nb, steps, dil, nh, dh)
    qs, ks, vs = split(q * dh ** -0.5), split(k), split(v)
    prev = lambda t: jnp.pad(t, ((0, 0), (1, 0), (0, 0), (0, 0), (0, 0), (0, 0)))[:, :-1]
    kk = jnp.concatenate([prev(ks), ks], axis=2)
    vv = jnp.concatenate([prev(vs), vs], axis=2)
    s = jnp.einsum('bnqrhd,bnkrhd->bnrhqk', qs, kk).astype(jnp.float32)
    qi = jnp.arange(steps)
    kj = jnp.arange(2 * steps)
    step = qi[:, None] + steps - kj[None, :]
    valid = ((step >= 0) & (step <= steps))[None] & \
        ((jnp.arange(nb)[:, None, None] * steps + kj[None, None, :] - steps) >= 0)
    bias = -slopes[:, None, None] * (step * dil)
    s = jnp.where(valid[None, :, None, None], s + bias, -jnp.inf)
    m = jnp.max(s, axis=-1, keepdims=True)
    e = jnp.exp(s - m)
    den = jnp.sum(e, axis=-1, keepdims=True)
    o = jnp.einsum('bnrhqk,bnkrhd->bnrhqd', e.astype(v.dtype), vv).astype(jnp.float32) / den
    lse = (m + jnp.log(den))[..., 0]
    o = o.transpose(0, 1, 4, 2, 3, 5).reshape(bsz, s_pad, nh, dh)[:, :seq]
    lse = lse.transpose(0, 1, 4, 2, 3).reshape(bsz, s_pad, nh)[:, :seq]
    return o, lse


def setup_inputs(seed: int = 0) -> dict:
    key = jax.random.key(seed)
    ks = jax.random.split(key, 20)
    f32 = jnp.float32
    nrm = lambda k, shape, scale: jax.random.normal(k, shape, f32) * scale
    col_scale = np.ones((IN_COLS,), np.float32)
    col_scale[A_Q + A_K:A_Q + A_K + A_V] = BETA
    b_v0 = A_Q + A_K + A_V + 2 * (B_QKV // 3)
    col_scale[b_v0:b_v0 + B_QKV // 3] = BETA
    w_in = nrm(ks[1], (DEPTH, D_MODEL, IN_COLS), D_MODEL ** -0.5) * jnp.asarray(col_scale)
    return {
        "x": jax.random.normal(ks[0], (BATCH, SEQ, D_MODEL), f32),
        "w_in": w_in,
        "b_gate": nrm(ks[2], (DEPTH, GATE_COLS), 0.01),
        "lambda_q1": nrm(ks[3], (DEPTH, A_HEAD_DIM), 0.1),
        "lambda_k1": nrm(ks[4], (DEPTH, A_HEAD_DIM), 0.1),
        "lambda_q2": nrm(ks[5], (DEPTH, A_HEAD_DIM), 0.1),
        "lambda_k2": nrm(ks[6], (DEPTH, A_HEAD_DIM), 0.1),
        "subln_w": 1.0 + nrm(ks[7], (DEPTH, 2 * A_HEAD_DIM), 0.01),
        "w_pa": nrm(ks[8], (DEPTH, A_V, D_MODEL), A_V ** -0.5),
        "w_pb": nrm(ks[9], (DEPTH, B_HEADS * B_HEAD_DIM, D_MODEL), (B_HEADS * B_HEAD_DIM) ** -0.5),
        "w_o": nrm(ks[10], (DEPTH, D_MODEL, D_MODEL), BETA * D_MODEL ** -0.5),
        "ln1_g": 1.0 + nrm(ks[11], (DEPTH, D_MODEL), 0.01),
        "ln1_b": nrm(ks[12], (DEPTH, D_MODEL), 0.01),
        "w_up": nrm(ks[13], (DEPTH, D_MODEL, 2 * D_FF), D_MODEL ** -0.5),
        "w_conv": nrm(ks[14], (DEPTH, CONV_WIDTH, 2 * D_FF), CONV_WIDTH ** -0.5),
        "b_conv": nrm(ks[15], (DEPTH, 2 * D_FF), 0.01),
        "w_down": nrm(ks[16], (DEPTH, D_FF, D_MODEL), BETA * D_FF ** -0.5),
        "ln2_g": 1.0 + nrm(ks[17], (DEPTH, D_MODEL), 0.01),
        "ln2_b": nrm(ks[18], (DEPTH, D_MODEL), 0.01),
    }


def reference(x, w_in, b_gate, lambda_q1, lambda_k1, lambda_q2, lambda_k2, subln_w,
              w_pa, w_pb, w_o, ln1_g, ln1_b, w_up, w_conv, b_conv, w_down, ln2_g, ln2_b):
    bsz, seq, _ = x.shape
    slopes_b = alibi_slopes(B_HEADS)
    for l in range(DEPTH):
        lambda_init = lambda_init_fn(l)
        proj = x @ w_in[l]
        qa, ka, va, pb, gl = jnp.split(proj, SPLITS, axis=-1)
        qa = qa.reshape(bsz, seq, 2 * A_HEADS, A_HEAD_DIM)
        ka = ka.reshape(bsz, seq, 2 * A_HEADS, A_HEAD_DIM)
        va = va.reshape(bsz, seq, A_HEADS, 2 * A_HEAD_DIM)
        lam = (jnp.exp(jnp.sum(lambda_q1[l].astype(jnp.float32) * lambda_k1[l].astype(jnp.float32)))
               - jnp.exp(jnp.sum(lambda_q2[l].astype(jnp.float32) * lambda_k2[l].astype(jnp.float32)))
               + lambda_init)
        oa = diff_attention(qa, ka, va, lam, subln_w[l], lambda_init)

        pb = pb.reshape(bsz, seq, 3, len(B_PATTERNS), B_HEADS, B_HEAD_DIM)
        outs, lses = [], []
        for g, (window, dil) in enumerate(B_PATTERNS):
            o, lse = dilated_group(pb[:, :, 0, g], pb[:, :, 1, g], pb[:, :, 2, g], slopes_b, window, dil)
            outs.append(o)
            lses.append(lse)
        wts = jax.nn.softmax(jnp.stack(lses), axis=0)
        ob = jnp.sum(wts[..., None] * jnp.stack(outs), axis=0)
        ob = ob.reshape(bsz, seq, B_HEADS * B_HEAD_DIM).astype(x.dtype)

        gates = jax.nn.sigmoid((gl + b_gate[l]).astype(jnp.float32)).astype(x.dtype)
        gates = gates.reshape(bsz, seq, 2, D_MODEL)
        y = gates[:, :, 0] * (oa @ w_pa[l]) + gates[:, :, 1] * (ob @ w_pb[l])
        x = layer_norm(ALPHA * x + y @ w_o[l], ln1_g[l], ln1_b[l])

        h = x @ w_up[l]
        hp = jnp.pad(h, ((0, 0), (CONV_WIDTH - 1, 0), (0, 0)))
        h = b_conv[l] + sum(hp[:, j:j + seq] * w_conv[l, j] for j in range(CONV_WIDTH))
        a, gv = jnp.split(h, 2, axis=-1)
        f = jax.nn.gelu(a, approximate=False) * gv
        x = layer_norm(ALPHA * x + f @ w_down[l], ln2_g[l], ln2_b[l])
    return x
```

```python
import functools
import math

import numpy as np
import jax
import jax.numpy as jnp
from jax import lax
from jax.experimental import pallas as pl
from jax.experimental.pallas import tpu as pltpu

BF16 = jnp.bfloat16
F32 = jnp.float32

D_MODEL = 1024
A_HEADS = 8
A_HEAD_DIM = 64
B_PATTERNS = ((128, 1), (512, 4), (2048, 16))
B_HEADS = 8
B_HEAD_DIM = 64
D_FF = 2816
CONV_WIDTH = 3
EPS = 1e-5
DEPTH = 1
ALPHA = (2.0 * DEPTH) ** 0.25

A_COLS = A_HEADS * 2 * A_HEAD_DIM
B_COLS = B_HEADS * B_HEAD_DIM
QKV_COLS = 3 * A_COLS + 3 * len(B_PATTERNS) * B_COLS
GATE_COLS = 2 * D_MODEL
PAIR = 2 * A_HEAD_DIM
STEPS = 128

VMEM_LIMIT = 56 * 1024 * 1024


def _alibi_slopes(n):
    return np.power(np.float32(2.0), -8.0 * (np.arange(n, dtype=np.float32) + 1) / n).astype(np.float32)


def _nt_dot(a, b):
    return lax.dot_general(a, b, (((1,), (1,)), ((), ())), preferred_element_type=F32)


def _layer_norm(z, g, b):
    mu = jnp.mean(z, axis=-1, keepdims=True)
    zc = z - mu
    var = jnp.mean(zc * zc, axis=-1, keepdims=True)
    return zc * lax.rsqrt(var + EPS) * g + b


def _proj_kernel(x_ref, w_ref, o_ref):
    o_ref[...] = jnp.dot(x_ref[...], w_ref[...], preferred_element_type=F32).astype(o_ref.dtype)


def _gate_kernel(x_ref, w_ref, b_ref, o_ref):
    z = jnp.dot(x_ref[...], w_ref[...], preferred_element_type=F32) + b_ref[...]
    o_ref[...] = (1.0 / (1.0 + jnp.exp(-z))).astype(o_ref.dtype)


def _projection(xb, wb, bias, tm, tn):
    m, k = xb.shape
    n = wb.shape[1]
    in_specs = [pl.BlockSpec((tm, k), lambda i, j: (i, 0)),
                pl.BlockSpec((k, tn), lambda i, j: (0, j))]
    args = [xb, wb]
    body = _proj_kernel
    if bias is not None:
        in_specs.append(pl.BlockSpec((1, tn), lambda i, j: (0, j)))
        args.append(bias)
        body = _gate_kernel
    return pl.pallas_call(
        body,
        grid=(m // tm, n // tn),
        in_specs=in_specs,
        out_specs=pl.BlockSpec((tm, tn), lambda i, j: (i, j)),
        out_shape=jax.ShapeDtypeStruct((m, n), BF16),
        compiler_params=pltpu.CompilerParams(
            dimension_semantics=("parallel", "arbitrary"), vmem_limit_bytes=VMEM_LIMIT),
        name="proj_gate" if bias is not None else "proj_qkv",
    )(*args)


def _diff_attn_kernel(slopes_ref, q_ref, k_ref, v_ref, lq1_ref, lk1_ref, lq2_ref, lk2_ref, subln_ref,
                      o_ref, m_ref, l_ref, acc_ref, *, tq, lambda_init):
    h = pl.program_id(1)
    qi = pl.program_id(2)
    slope = slopes_ref[h]
    q = q_ref[0].astype(F32) * (A_HEAD_DIM ** -0.5)
    lane = lax.broadcasted_iota(jnp.int32, (tq, PAIR), 1)
    q_maps = (jnp.where(lane < A_HEAD_DIM, q, 0.0).astype(BF16),
              jnp.where(lane < A_HEAD_DIM, 0.0, q).astype(BF16))

    m_ref[...] = jnp.full(m_ref.shape, -jnp.inf, F32)
    l_ref[...] = jnp.zeros(l_ref.shape, F32)
    acc_ref[...] = jnp.zeros(acc_ref.shape, F32)

    col = lax.broadcasted_iota(jnp.int32, (1, tq), 1)

    def chunk(j, masked):
        k0 = pl.multiple_of(j * tq, tq)
        kc = k_ref[0, pl.ds(k0, tq), :]
        vc = v_ref[0, pl.ds(k0, tq), :]
        bias = slope * (col + (j - qi) * tq).astype(F32)
        if masked:
            row = lax.broadcasted_iota(jnp.int32, (tq, tq), 0)
            keep = lax.broadcasted_iota(jnp.int32, (tq, tq), 1) <= row
        for mp in range(2):
            s = _nt_dot(q_maps[mp], kc) + bias
            if masked:
                s = jnp.where(keep, s, -jnp.inf)
            m_old = m_ref[mp]
            m_new = jnp.maximum(m_old, jnp.max(s, axis=-1, keepdims=True))
            alpha = jnp.exp(m_old - m_new)
            p = jnp.exp(s - m_new)
            l_ref[mp] = alpha * l_ref[mp] + jnp.sum(p, axis=-1, keepdims=True)
            acc_ref[mp] = alpha * acc_ref[mp] + jnp.dot(p.astype(BF16), vc, preferred_element_type=F32)
            m_ref[mp] = m_new

    def body(j, carry):
        chunk(j, False)
        return carry

    lax.fori_loop(0, qi, body, 0)
    chunk(qi, True)

    lam = (jnp.exp(jnp.sum(lq1_ref[...] * lk1_ref[...], axis=-1, keepdims=True))
           - jnp.exp(jnp.sum(lq2_ref[...] * lk2_ref[...], axis=-1, keepdims=True))
           + lambda_init)
    o = acc_ref[0] / l_ref[0] - lam * (acc_ref[1] / l_ref[1])
    o = o * lax.rsqrt(jnp.mean(o * o, axis=-1, keepdims=True) + EPS)
    o_ref[0] = (o * subln_ref[...] * (1.0 - lambda_init)).astype(o_ref.dtype)


def _diff_attention(p3, slopes, lq1, lk1, lq2, lk2, subln, lambda_init, tq):
    bsz, seq, _ = p3.shape
    vec = lambda n: pl.BlockSpec((1, n), lambda b, h, i: (0, 0))
    kernel = functools.partial(_diff_attn_kernel, tq=tq, lambda_init=lambda_init)
    return pl.pallas_call(
        kernel,
        grid=(bsz, A_HEADS, seq // tq),
        in_specs=[
            pl.BlockSpec(memory_space=pltpu.SMEM),
            pl.BlockSpec((1, tq, PAIR), lambda b, h, i: (b, i, h)),
            pl.BlockSpec((1, seq, PAIR), lambda b, h, i: (b, 0, A_HEADS + h)),
            pl.BlockSpec((1, seq, PAIR), lambda b, h, i: (b, 0, 2 * A_HEADS + h)),
            vec(A_HEAD_DIM), vec(A_HEAD_DIM), vec(A_HEAD_DIM), vec(A_HEAD_DIM), vec(PAIR),
        ],
        out_specs=pl.BlockSpec((1, tq, PAIR), lambda b, h, i: (b, i, h)),
        out_shape=jax.ShapeDtypeStruct((bsz, seq, A_COLS), BF16),
        scratch_shapes=[pltpu.VMEM((2, tq, 1), F32), pltpu.VMEM((2, tq, 1), F32),
                        pltpu.VMEM((2, tq, PAIR), F32)],
        compiler_params=pltpu.CompilerParams(
            dimension_semantics=("parallel", "parallel", "arbitrary"), vmem_limit_bytes=VMEM_LIMIT),
        name="diff_attn",
    )(slopes, p3, p3, p3, lq1, lk1, lq2, lk2, subln)


def _dilated_kernel(q_ref, kp_ref, kc_ref, vp_ref, vc_ref, o_ref, lse_ref, *, dil, slopes):
    n = pl.program_id(2)
    q = q_ref[0].astype(F32) * (B_HEAD_DIM ** -0.5)
    kk = jnp.concatenate([kp_ref[0], kc_ref[0]], axis=0)
    vv = jnp.concatenate([vp_ref[0], vc_ref[0]], axis=0)
    qi = lax.broadcasted_iota(jnp.int32, (STEPS, 2 * STEPS), 0)
    kj = lax.broadcasted_iota(jnp.int32, (STEPS, 2 * STEPS), 1)
    step = qi + STEPS - kj
    valid = (step >= 0) & (step <= STEPS) & (n * STEPS + kj - STEPS >= 0)
    dist = (step * dil).astype(F32)
    lane = lax.broadcasted_iota(jnp.int32, (STEPS, PAIR), 1)
    low = lane < B_HEAD_DIM
    for pr in range(B_HEADS // 2):
        cols = slice(pr * PAIR, (pr + 1) * PAIR)
        qp, kp, vp = q[:, cols], kk[:, cols], vv[:, cols]
        o_halves, lse_halves = [], []
        for par in range(2):
            qm = jnp.where(low, qp, 0.0) if par == 0 else jnp.where(low, 0.0, qp)
            s = _nt_dot(qm.astype(BF16), kp)
            s = jnp.where(valid, s - float(slopes[2 * pr + par]) * dist, -jnp.inf)
            m = jnp.max(s, axis=-1, keepdims=True)
            e = jnp.exp(s - m)
            den = jnp.sum(e, axis=-1, keepdims=True)
            o_halves.append(jnp.dot(e.astype(BF16), vp, preferred_element_type=F32) / den)
            lse_halves.append(m + jnp.log(den))
        o_ref[0, :, cols] = jnp.where(low, o_halves[0], o_halves[1]).astype(o_ref.dtype)
        lse_ref[0, :, cols] = jnp.where(low, lse_halves[0], lse_halves[1])


def _dilated_attention(p3, group, dil):
    bsz, seq, cols = p3.shape
    rows = seq // dil
    pv = p3.reshape(bsz, rows, dil * cols)
    per_row = cols // B_COLS
    first = 3 * A_COLS // B_COLS
    ngroups = len(B_PATTERNS)
    qcol = lambda r: r * per_row + first + group
    kcol = lambda r: r * per_row + first + ngroups + group
    vcol = lambda r: r * per_row + first + 2 * ngroups + group
    blk = (1, STEPS, B_COLS)
    prev = lambda n: jnp.maximum(n - 1, 0)
    kernel = functools.partial(_dilated_kernel, dil=dil, slopes=_alibi_slopes(B_HEADS))
    o, lse = pl.pallas_call(
        kernel,
        grid=(bsz, dil, rows // STEPS),
        in_specs=[
            pl.BlockSpec(blk, lambda b, r, n: (b, n, qcol(r))),
            pl.BlockSpec(blk, lambda b, r, n: (b, prev(n), kcol(r))),
            pl.BlockSpec(blk, lambda b, r, n: (b, n, kcol(r))),
            pl.BlockSpec(blk, lambda b, r, n: (b, prev(n), vcol(r))),
            pl.BlockSpec(blk, lambda b, r, n: (b, n, vcol(r))),
        ],
        out_specs=[pl.BlockSpec(blk, lambda b, r, n: (b, n, r)),
                   pl.BlockSpec(blk, lambda b, r, n: (b, n, r))],
        out_shape=[jax.ShapeDtypeStruct((bsz, rows, dil * B_COLS), BF16),
                   jax.ShapeDtypeStruct((bsz, rows, dil * B_COLS), F32)],
        compiler_params=pltpu.CompilerParams(
            dimension_semantics=("parallel", "parallel", "arbitrary"), vmem_limit_bytes=VMEM_LIMIT),
        name=f"dilated_attn_d{dil}",
    )(pv, pv, pv, pv, pv)
    return o.reshape(bsz * seq, B_COLS), lse.reshape(bsz * seq, B_COLS)


def _merge_kernel(oa_ref, o0_ref, o1_ref, o2_ref, l0_ref, l1_ref, l2_ref, ga_ref, gb_ref, x_ref,
                  wpa_ref, wpb_ref, wo_ref, g_ref, b_ref, out_ref):
    l0, l1, l2 = l0_ref[...], l1_ref[...], l2_ref[...]
    mx = jnp.maximum(jnp.maximum(l0, l1), l2)
    e0, e1, e2 = jnp.exp(l0 - mx), jnp.exp(l1 - mx), jnp.exp(l2 - mx)
    ob = (e0 * o0_ref[...].astype(F32) + e1 * o1_ref[...].astype(F32) + e2 * o2_ref[...].astype(F32)) \
        / (e0 + e1 + e2)
    ya = jnp.dot(oa_ref[...], wpa_ref[...], preferred_element_type=F32)
    yb = jnp.dot(ob.astype(BF16), wpb_ref[...], preferred_element_type=F32)
    y = ga_ref[...].astype(F32) * ya + gb_ref[...].astype(F32) * yb
    z = ALPHA * x_ref[...] + jnp.dot(y.astype(BF16), wo_ref[...], preferred_element_type=F32)
    out_ref[...] = _layer_norm(z, g_ref[...], b_ref[...])


def _merge(oa, obs, lses, gates, x2, wpa, wpb, wo, g, b, tm):
    m = x2.shape[0]
    row = lambda w: pl.BlockSpec((tm, w), lambda i: (i, 0))
    full = lambda a: pl.BlockSpec(a.shape, lambda i: (0, 0))
    return pl.pallas_call(
        _merge_kernel,
        grid=(m // tm,),
        in_specs=[row(A_COLS), row(B_COLS), row(B_COLS), row(B_COLS), row(B_COLS), row(B_COLS), row(B_COLS),
                  pl.BlockSpec((tm, D_MODEL), lambda i: (i, 0)), pl.BlockSpec((tm, D_MODEL), lambda i: (i, 1)),
                  row(D_MODEL), full(wpa), full(wpb), full(wo), full(g), full(b)],
        out_specs=row(D_MODEL),
        out_shape=jax.ShapeDtypeStruct((m, D_MODEL), F32),
        compiler_params=pltpu.CompilerParams(dimension_semantics=("parallel",), vmem_limit_bytes=VMEM_LIMIT),
        name="merge_ln1",
    )(oa, *obs, *lses, gates, gates, x2, wpa, wpb, wo, g, b)


HALO = 16


def _ffn_kernel(x_ref, halo_ref, wa_ref, wg_ref, wca_ref, wcg_ref, bca_ref, bcg_ref, wd_ref, g_ref, b_ref,
                out_ref, xcat_ref, acc_ref, *, tm, tiles_per_seq):
    i = pl.program_id(0)
    c = pl.program_id(1)

    @pl.when(c == 0)
    def _():
        halo = jnp.where(i % tiles_per_seq == 0, 0.0, halo_ref[...])
        xcat_ref[0:HALO, :] = halo.astype(BF16)
        xcat_ref[HALO:, :] = x_ref[...].astype(BF16)
        acc_ref[...] = jnp.zeros(acc_ref.shape, F32)

    xcat = xcat_ref[...]

    def conv(w_ref, wc_ref, bc_ref):
        hfull = jnp.dot(xcat, w_ref[...], preferred_element_type=F32)
        out = bc_ref[...] + wc_ref[CONV_WIDTH - 1:CONV_WIDTH, :] * hfull[HALO:, :]
        for back in range(1, CONV_WIDTH):
            shifted = pltpu.roll(hfull, back, axis=0)[HALO:, :]
            out = out + wc_ref[CONV_WIDTH - 1 - back:CONV_WIDTH - back, :] * shifted
        return out

    a = conv(wa_ref, wca_ref, bca_ref)
    gv = conv(wg_ref, wcg_ref, bcg_ref)
    f = 0.5 * a * (1.0 + lax.erf(a * (2.0 ** -0.5))) * gv
    acc_ref[...] += jnp.dot(f.astype(BF16), wd_ref[...], preferred_element_type=F32)

    @pl.when(c == pl.num_programs(1) - 1)
    def _():
        z = ALPHA * x_ref[...] + acc_ref[...]
        out_ref[...] = _layer_norm(z, g_ref[...], b_ref[...])


def _ffn(x1, w_up, w_conv, b_conv, w_down, g, b, seq, tm, tc):
    m = x1.shape[0]
    nc = D_FF // tc
    halo_blocks = tm // HALO
    kernel = functools.partial(_ffn_kernel, tm=tm, tiles_per_seq=seq // tm)
    vec = pl.BlockSpec((1, D_MODEL), lambda i, c: (0, 0))
    return pl.pallas_call(
        kernel,
        grid=(m // tm, nc),
        in_specs=[
            pl.BlockSpec((tm, D_MODEL), lambda i, c: (i, 0)),
            pl.BlockSpec((HALO, D_MODEL), lambda i, c: (jnp.maximum(i * halo_blocks - 1, 0), 0)),
            pl.BlockSpec((D_MODEL, tc), lambda i, c: (0, c)),
            pl.BlockSpec((D_MODEL, tc), lambda i, c: (0, nc + c)),
            pl.BlockSpec((CONV_WIDTH, tc), lambda i, c: (0, c)),
            pl.BlockSpec((CONV_WIDTH, tc), lambda i, c: (0, nc + c)),
            pl.BlockSpec((1, tc), lambda i, c: (0, c)),
            pl.BlockSpec((1, tc), lambda i, c: (0, nc + c)),
            pl.BlockSpec((tc, D_MODEL), lambda i, c: (c, 0)),
            vec, vec,
        ],
        out_specs=pl.BlockSpec((tm, D_MODEL), lambda i, c: (i, 0)),
        out_shape=jax.ShapeDtypeStruct((m, D_MODEL), F32),
        scratch_shapes=[pltpu.VMEM((HALO + tm, D_MODEL), BF16), pltpu.VMEM((tm, D_MODEL), F32)],
        compiler_params=pltpu.CompilerParams(
            dimension_semantics=("parallel", "arbitrary"), vmem_limit_bytes=VMEM_LIMIT),
        name="ffn_ln2",
    )(x1, x1, w_up, w_up, w_conv, w_conv, b_conv, b_conv, w_down, g, b)


def kernel(x, w_in, b_gate, lambda_q1, lambda_k1, lambda_q2, lambda_k2, subln_w, w_pa, w_pb, w_o, ln1_g, ln1_b,
           w_up, w_conv, b_conv, w_down, ln2_g, ln2_b):
    bsz, seq, d = x.shape
    assert (seq, d) == (4096, D_MODEL) and w_in.shape[0] == DEPTH
    slopes_a = jnp.asarray(_alibi_slopes(A_HEADS))
    for l in range(DEPTH):
        lambda_init = 0.8 - 0.6 * math.exp(-0.3 * l)
        x2 = x.reshape(bsz * seq, d)
        xb = x2.astype(BF16)
        wb = w_in[l].astype(BF16)
        proj = _projection(xb, wb[:, :QKV_COLS], None, tm=2048, tn=512)
        gates = _projection(xb, wb[:, QKV_COLS:], b_gate[l][None, :], tm=2048, tn=512)
        p3 = proj.reshape(bsz, seq, QKV_COLS)

        oa = _diff_attention(p3, slopes_a, lambda_q1[l][None], lambda_k1[l][None], lambda_q2[l][None],
                             lambda_k2[l][None], subln_w[l][None], lambda_init, tq=256)
        obs, lses = [], []
        for g, (window, dil) in enumerate(B_PATTERNS):
            assert window // dil == STEPS
            o, lse = _dilated_attention(p3, g, dil)
            obs.append(o)
            lses.append(lse)

        x1 = _merge(oa.reshape(bsz * seq, A_COLS), obs, lses, gates, x2,
                    w_pa[l].astype(BF16), w_pb[l].astype(BF16), w_o[l].astype(BF16),
                    ln1_g[l][None], ln1_b[l][None], tm=512)
        x2 = _ffn(x1, w_up[l].astype(BF16), w_conv[l], b_conv[l][None], w_down[l].astype(BF16),
                  ln2_g[l][None], ln2_b[l][None], seq, tm=512, tc=1408)
        x = x2.reshape(bsz, seq, d)
    return x
```

```python
import functools
import math

import numpy as np
import jax
import jax.numpy as jnp
from jax import lax
from jax.experimental import pallas as pl
from jax.experimental.pallas import tpu as pltpu

BF16 = jnp.bfloat16
F32 = jnp.float32

D_MODEL = 1024
A_HEADS = 8
A_HEAD_DIM = 64
B_PATTERNS = ((128, 1), (512, 4), (2048, 16))
B_HEADS = 8
B_HEAD_DIM = 64
D_FF = 2816
CONV_WIDTH = 3
EPS = 1e-5
DEPTH = 1
ALPHA = (2.0 * DEPTH) ** 0.25

A_COLS = A_HEADS * 2 * A_HEAD_DIM
B_COLS = B_HEADS * B_HEAD_DIM
QKV_COLS = 3 * A_COLS + 3 * len(B_PATTERNS) * B_COLS
GATE_COLS = 2 * D_MODEL
PAIR = 2 * A_HEAD_DIM
STEPS = 128

VMEM_LIMIT = 56 * 1024 * 1024


def _alibi_slopes(n):
    return np.power(np.float32(2.0), -8.0 * (np.arange(n, dtype=np.float32) + 1) / n).astype(np.float32)


def _nt_dot(a, b):
    return lax.dot_general(a, b, (((1,), (1,)), ((), ())), preferred_element_type=F32)


def _layer_norm(z, g, b):
    mu = jnp.mean(z, axis=-1, keepdims=True)
    zc = z - mu
    var = jnp.mean(zc * zc, axis=-1, keepdims=True)
    return zc * lax.rsqrt(var + EPS) * g + b


def _proj_kernel(x_ref, w_ref, o_ref):
    o_ref[...] = jnp.dot(x_ref[...], w_ref[...], preferred_element_type=F32).astype(o_ref.dtype)


def _gate_kernel(x_ref, w_ref, b_ref, o_ref):
    z = jnp.dot(x_ref[...], w_ref[...], preferred_element_type=F32) + b_ref[...]
    o_ref[...] = (1.0 / (1.0 + jnp.exp(-z))).astype(o_ref.dtype)


def _projection(xb, wb, bias, tm, tn):
    m, k = xb.shape
    n = wb.shape[1]
    in_specs = [pl.BlockSpec((tm, k), lambda i, j: (i, 0)),
                pl.BlockSpec((k, tn), lambda i, j: (0, j))]
    args = [xb, wb]
    body = _proj_kernel
    if bias is not None:
        in_specs.append(pl.BlockSpec((1, tn), lambda i, j: (0, j)))
        args.append(bias)
        body = _gate_kernel
    return pl.pallas_call(
        body,
        grid=(m // tm, n // tn),
        in_specs=in_specs,
        out_specs=pl.BlockSpec((tm, tn), lambda i, j: (i, j)),
        out_shape=jax.ShapeDtypeStruct((m, n), BF16),
        compiler_params=pltpu.CompilerParams(
            dimension_semantics=("parallel", "arbitrary"), vmem_limit_bytes=VMEM_LIMIT),
        name="proj_gate" if bias is not None else "proj_qkv",
    )(*args)


def _diff_attn_kernel(slopes_ref, q_ref, k_ref, v_ref, lq1_ref, lk1_ref, lq2_ref, lk2_ref, subln_ref,
                      o_ref, vt_ref, bias_ref, acc_ref, *, t, lambda_init):
    h = pl.program_id(1)
    qi = pl.program_id(2)
    slope = slopes_ref[h]
    nchunks = vt_ref.shape[0]

    @pl.when(qi == 0)
    def _():
        def transpose_v(c, carry):
            r0 = pl.multiple_of(c * t, t)
            vt_ref[c] = v_ref[0, pl.ds(r0, t), :].astype(F32).T.astype(BF16)
            return carry
        lax.fori_loop(0, nchunks, transpose_v, 0)

    krow = lax.broadcasted_iota(jnp.int32, (t, t), 0)
    qcol = lax.broadcasted_iota(jnp.int32, (t, t), 1)
    bias = slope * krow.astype(F32)
    bias_ref[0] = bias
    bias_ref[1] = jnp.where(krow <= qcol, bias, -jnp.inf)

    qt = q_ref[0].astype(F32).T * (A_HEAD_DIM ** -0.5)
    dim = lax.broadcasted_iota(jnp.int32, (PAIR, t), 0)
    qt_maps = (jnp.where(dim < A_HEAD_DIM, qt, 0.0).astype(BF16),
               jnp.where(dim < A_HEAD_DIM, 0.0, qt).astype(BF16))
    acc_ref[...] = jnp.zeros(acc_ref.shape, F32)

    def chunk(j, stats, diag):
        kc = k_ref[0, pl.ds(pl.multiple_of(j * t, t), t), :]
        vt = vt_ref[j]
        shift = slope * ((j - qi) * t).astype(F32)
        out = []
        for mp in range(2):
            m_old, l_old = stats[2 * mp], stats[2 * mp + 1]
            s = jnp.dot(kc, qt_maps[mp], preferred_element_type=F32) + bias_ref[diag]
            m_new = jnp.maximum(m_old, jnp.max(s, axis=0, keepdims=True) + shift)
            p = jnp.exp(s - (m_new - shift))
            alpha = jnp.exp(m_old - m_new)
            acc_ref[mp] = alpha * acc_ref[mp] + jnp.dot(vt, p.astype(BF16), preferred_element_type=F32)
            out += [m_new, alpha * l_old + jnp.sum(p, axis=0, keepdims=True)]
        return tuple(out)

    neg = jnp.full((1, t), -jnp.inf, F32)
    zero = jnp.zeros((1, t), F32)
    stats = lax.fori_loop(0, qi, lambda j, st: chunk(j, st, 0), (neg, zero, neg, zero))
    _, l1, _, l2 = chunk(qi, stats, 1)

    lam = (jnp.exp(jnp.sum(lq1_ref[...] * lk1_ref[...], axis=-1, keepdims=True))
           - jnp.exp(jnp.sum(lq2_ref[...] * lk2_ref[...], axis=-1, keepdims=True))
           + lambda_init)
    ot = acc_ref[0] / l1 - lam * (acc_ref[1] / l2)
    ot = ot * lax.rsqrt(jnp.mean(ot * ot, axis=0, keepdims=True) + EPS)
    ot = ot * subln_ref[...] * (1.0 - lambda_init)
    o_ref[0] = ot.T.astype(o_ref.dtype)


def _diff_attention(p3, slopes, lq1, lk1, lq2, lk2, subln_col, lambda_init, t):
    bsz, seq, _ = p3.shape
    vec = lambda n: pl.BlockSpec((1, n), lambda b, h, i: (0, 0))
    kernel = functools.partial(_diff_attn_kernel, t=t, lambda_init=lambda_init)
    return pl.pallas_call(
        kernel,
        grid=(bsz, A_HEADS, seq // t),
        in_specs=[
            pl.BlockSpec(memory_space=pltpu.SMEM),
            pl.BlockSpec((1, t, PAIR), lambda b, h, i: (b, i, h)),
            pl.BlockSpec((1, seq, PAIR), lambda b, h, i: (b, 0, A_HEADS + h)),
            pl.BlockSpec((1, seq, PAIR), lambda b, h, i: (b, 0, 2 * A_HEADS + h)),
            vec(A_HEAD_DIM), vec(A_HEAD_DIM), vec(A_HEAD_DIM), vec(A_HEAD_DIM),
            pl.BlockSpec((PAIR, 1), lambda b, h, i: (0, 0)),
        ],
        out_specs=pl.BlockSpec((1, t, PAIR), lambda b, h, i: (b, i, h)),
        out_shape=jax.ShapeDtypeStruct((bsz, seq, A_COLS), BF16),
        scratch_shapes=[pltpu.VMEM((seq // t, PAIR, t), BF16), pltpu.VMEM((2, t, t), F32),
                        pltpu.VMEM((2, PAIR, t), F32)],
        compiler_params=pltpu.CompilerParams(
            dimension_semantics=("parallel", "parallel", "arbitrary"), vmem_limit_bytes=VMEM_LIMIT),
        name="diff_attn",
    )(slopes, p3, p3, p3, lq1, lk1, lq2, lk2, subln_col)


def _dilated_kernel(q_ref, kp_ref, kc_ref, vp_ref, vc_ref, o_ref, lse_ref, *, dil, slopes):
    n = pl.program_id(2)
    q = q_ref[0].astype(F32) * (B_HEAD_DIM ** -0.5)
    kk = jnp.concatenate([kp_ref[0], kc_ref[0]], axis=0)
    vv = jnp.concatenate([vp_ref[0], vc_ref[0]], axis=0)
    qi = lax.broadcasted_iota(jnp.int32, (STEPS, 2 * STEPS), 0)
    kj = lax.broadcasted_iota(jnp.int32, (STEPS, 2 * STEPS), 1)
    step = qi + STEPS - kj
    valid = (step >= 0) & (step <= STEPS) & (n * STEPS + kj - STEPS >= 0)
    dist = (step * dil).astype(F32)
    lane = lax.broadcasted_iota(jnp.int32, (STEPS, PAIR), 1)
    low = lane < B_HEAD_DIM
    for pr in range(B_HEADS // 2):
        cols = slice(pr * PAIR, (pr + 1) * PAIR)
        qp, kp, vp = q[:, cols], kk[:, cols], vv[:, cols]
        o_halves, lse_halves = [], []
        for par in range(2):
            qm = jnp.where(low, qp, 0.0) if par == 0 else jnp.where(low, 0.0, qp)
            s = _nt_dot(qm.astype(BF16), kp)
            s = jnp.where(valid, s - float(slopes[2 * pr + par]) * dist, -jnp.inf)
            m = jnp.max(s, axis=-1, keepdims=True)
            e = jnp.exp(s - m)
            den = jnp.sum(e, axis=-1, keepdims=True)
            o_halves.append(jnp.dot(e.astype(BF16), vp, preferred_element_type=F32) / den)
            lse_halves.append(m + jnp.log(den))
        o_ref[0, :, cols] = jnp.where(low, o_halves[0], o_halves[1]).astype(o_ref.dtype)
        lse_ref[0, :, cols] = jnp.where(low, lse_halves[0], lse_halves[1])


def _dilated_attention(p3, group, dil):
    bsz, seq, cols = p3.shape
    rows = seq // dil
    pv = p3.reshape(bsz, rows, dil * cols)
    per_row = cols // B_COLS
    first = 3 * A_COLS // B_COLS
    ngroups = len(B_PATTERNS)
    qcol = lambda r: r * per_row + first + group
    kcol = lambda r: r * per_row + first + ngroups + group
    vcol = lambda r: r * per_row + first + 2 * ngroups + group
    blk = (1, STEPS, B_COLS)
    prev = lambda n: jnp.maximum(n - 1, 0)
    kernel = functools.partial(_dilated_kernel, dil=dil, slopes=_alibi_slopes(B_HEADS))
    o, lse = pl.pallas_call(
        kernel,
        grid=(bsz, dil, rows // STEPS),
        in_specs=[
            pl.BlockSpec(blk, lambda b, r, n: (b, n, qcol(r))),
            pl.BlockSpec(blk, lambda b, r, n: (b, prev(n), kcol(r))),
            pl.BlockSpec(blk, lambda b, r, n: (b, n, kcol(r))),
            pl.BlockSpec(blk, lambda b, r, n: (b, prev(n), vcol(r))),
            pl.BlockSpec(blk, lambda b, r, n: (b, n, vcol(r))),
        ],
        out_specs=[pl.BlockSpec(blk, lambda b, r, n: (b, n, r)),
                   pl.BlockSpec(blk, lambda b, r, n: (b, n, r))],
        out_shape=[jax.ShapeDtypeStruct((bsz, rows, dil * B_COLS), BF16),
                   jax.ShapeDtypeStruct((bsz, rows, dil * B_COLS), F32)],
        compiler_params=pltpu.CompilerParams(
            dimension_semantics=("parallel", "parallel", "arbitrary"), vmem_limit_bytes=VMEM_LIMIT),
        name=f"dilated_attn_d{dil}",
    )(pv, pv, pv, pv, pv)
    return o.reshape(bsz * seq, B_COLS), lse.reshape(bsz * seq, B_COLS)


def _merge_kernel(oa_ref, o0_ref, o1_ref, o2_ref, l0_ref, l1_ref, l2_ref, ga_ref, gb_ref, x_ref,
                  wpa_ref, wpb_ref, wo_ref, g_ref, b_ref, out_ref):
    l0, l1, l2 = l0_ref[...], l1_ref[...], l2_ref[...]
    mx = jnp.maximum(jnp.maximum(l0, l1), l2)
    e0, e1, e2 = jnp.exp(l0 - mx), jnp.exp(l1 - mx), jnp.exp(l2 - mx)
    ob = (e0 * o0_ref[...].astype(F32) + e1 * o1_ref[...].astype(F32) + e2 * o2_ref[...].astype(F32)) \
        / (e0 + e1 + e2)
    ya = jnp.dot(oa_ref[...], wpa_ref[...], preferred_element_type=F32)
    yb = jnp.dot(ob.astype(BF16), wpb_ref[...], preferred_element_type=F32)
    y = ga_ref[...].astype(F32) * ya + gb_ref[...].astype(F32) * yb
    z = ALPHA * x_ref[...] + jnp.dot(y.astype(BF16), wo_ref[...], preferred_element_type=F32)
    out_ref[...] = _layer_norm(z, g_ref[...], b_ref[...])


def _merge(oa, obs, lses, gates, x2, wpa, wpb, wo, g, b, tm):
    m = x2.shape[0]
    row = lambda w: pl.BlockSpec((tm, w), lambda i: (i, 0))
    full = lambda a: pl.BlockSpec(a.shape, lambda i: (0, 0))
    return pl.pallas_call(
        _merge_kernel,
        grid=(m // tm,),
        in_specs=[row(A_COLS), row(B_COLS), row(B_COLS), row(B_COLS), row(B_COLS), row(B_COLS), row(B_COLS),
                  pl.BlockSpec((tm, D_MODEL), lambda i: (i, 0)), pl.BlockSpec((tm, D_MODEL), lambda i: (i, 1)),
                  row(D_MODEL), full(wpa), full(wpb), full(wo), full(g), full(b)],
        out_specs=row(D_MODEL),
        out_shape=jax.ShapeDtypeStruct((m, D_MODEL), F32),
        compiler_params=pltpu.CompilerParams(dimension_semantics=("parallel",), vmem_limit_bytes=VMEM_LIMIT),
        name="merge_ln1",
    )(oa, *obs, *lses, gates, gates, x2, wpa, wpb, wo, g, b)


HALO = 16


def _ffn_kernel(x_ref, halo_ref, wa_ref, wg_ref, wca_ref, wcg_ref, bca_ref, bcg_ref, wd_ref, g_ref, b_ref,
                out_ref, xcat_ref, acc_ref, *, tm, tiles_per_seq):
    i = pl.program_id(0)
    c = pl.program_id(1)

    @pl.when(c == 0)
    def _():
        halo = jnp.where(i % tiles_per_seq == 0, 0.0, halo_ref[...])
        xcat_ref[0:HALO, :] = halo.astype(BF16)
        xcat_ref[HALO:, :] = x_ref[...].astype(BF16)
        acc_ref[...] = jnp.zeros(acc_ref.shape, F32)

    xcat = xcat_ref[...]

    def conv(w_ref, wc_ref, bc_ref):
        hfull = jnp.dot(xcat, w_ref[...], preferred_element_type=F32)
        out = bc_ref[...] + wc_ref[CONV_WIDTH - 1:CONV_WIDTH, :] * hfull[HALO:, :]
        for back in range(1, CONV_WIDTH):
            shifted = pltpu.roll(hfull, back, axis=0)[HALO:, :]
            out = out + wc_ref[CONV_WIDTH - 1 - back:CONV_WIDTH - back, :] * shifted
        return out

    a = conv(wa_ref, wca_ref, bca_ref)
    gv = conv(wg_ref, wcg_ref, bcg_ref)
    f = 0.5 * a * (1.0 + lax.erf(a * (2.0 ** -0.5))) * gv
    acc_ref[...] += jnp.dot(f.astype(BF16), wd_ref[...], preferred_element_type=F32)

    @pl.when(c == pl.num_programs(1) - 1)
    def _():
        z = ALPHA * x_ref[...] + acc_ref[...]
        out_ref[...] = _layer_norm(z, g_ref[...], b_ref[...])


def _ffn(x1, w_up, w_conv, b_conv, w_down, g, b, seq, tm, tc):
    m = x1.shape[0]
    nc = D_FF // tc
    halo_blocks = tm // HALO
    kernel = functools.partial(_ffn_kernel, tm=tm, tiles_per_seq=seq // tm)
    vec = pl.BlockSpec((1, D_MODEL), lambda i, c: (0, 0))
    return pl.pallas_call(
        kernel,
        grid=(m // tm, nc),
        in_specs=[
            pl.BlockSpec((tm, D_MODEL), lambda i, c: (i, 0)),
            pl.BlockSpec((HALO, D_MODEL), lambda i, c: (jnp.maximum(i * halo_blocks - 1, 0), 0)),
            pl.BlockSpec((D_MODEL, tc), lambda i, c: (0, c)),
            pl.BlockSpec((D_MODEL, tc), lambda i, c: (0, nc + c)),
            pl.BlockSpec((CONV_WIDTH, tc), lambda i, c: (0, c)),
            pl.BlockSpec((CONV_WIDTH, tc), lambda i, c: (0, nc + c)),
            pl.BlockSpec((1, tc), lambda i, c: (0, c)),
            pl.BlockSpec((1, tc), lambda i, c: (0, nc + c)),
            pl.BlockSpec((tc, D_MODEL), lambda i, c: (c, 0)),
            vec, vec,
        ],
        out_specs=pl.BlockSpec((tm, D_MODEL), lambda i, c: (i, 0)),
        out_shape=jax.ShapeDtypeStruct((m, D_MODEL), F32),
        scratch_shapes=[pltpu.VMEM((HALO + tm, D_MODEL), BF16), pltpu.VMEM((tm, D_MODEL), F32)],
        compiler_params=pltpu.CompilerParams(
            dimension_semantics=("parallel", "arbitrary"), vmem_limit_bytes=VMEM_LIMIT),
        name="ffn_ln2",
    )(x1, x1, w_up, w_up, w_conv, w_conv, b_conv, b_conv, w_down, g, b)


def kernel(x, w_in, b_gate, lambda_q1, lambda_k1, lambda_q2, lambda_k2, subln_w, w_pa, w_pb, w_o, ln1_g, ln1_b,
           w_up, w_conv, b_conv, w_down, ln2_g, ln2_b):
    bsz, seq, d = x.shape
    assert (seq, d) == (4096, D_MODEL) and w_in.shape[0] == DEPTH
    slopes_a = jnp.asarray(_alibi_slopes(A_HEADS))
    for l in range(DEPTH):
        lambda_init = 0.8 - 0.6 * math.exp(-0.3 * l)
        x2 = x.reshape(bsz * seq, d)
        xb = x2.astype(BF16)
        wb = w_in[l].astype(BF16)
        proj = _projection(xb, wb[:, :QKV_COLS], None, tm=2048, tn=512)
        gates = _projection(xb, wb[:, QKV_COLS:], b_gate[l][None, :], tm=2048, tn=512)
        p3 = proj.reshape(bsz, seq, QKV_COLS)

        oa = _diff_attention(p3, slopes_a, lambda_q1[l][None], lambda_k1[l][None], lambda_q2[l][None],
                             lambda_k2[l][None], subln_w[l][:, None], lambda_init, t=512)
        obs, lses = [], []
        for g, (window, dil) in enumerate(B_PATTERNS):
            assert window // dil == STEPS
            o, lse = _dilated_attention(p3, g, dil)
            obs.append(o)
            lses.append(lse)

        x1 = _merge(oa.reshape(bsz * seq, A_COLS), obs, lses, gates, x2,
                    w_pa[l].astype(BF16), w_pb[l].astype(BF16), w_o[l].astype(BF16),
                    ln1_g[l][None], ln1_b[l][None], tm=512)
        x2 = _ffn(x1, w_up[l].astype(BF16), w_conv[l], b_conv[l][None], w_down[l].astype(BF16),
                  ln2_g[l][None], ln2_b[l][None], seq, tm=512, tc=1408)
        x = x2.reshape(bsz, seq, d)
    return x
```

```python
import functools
import math

import numpy as np
import jax
import jax.numpy as jnp
from jax import lax
from jax.experimental import pallas as pl
from jax.experimental.pallas import tpu as pltpu

BF16 = jnp.bfloat16
F32 = jnp.float32

D_MODEL = 1024
A_HEADS = 8
A_HEAD_DIM = 64
B_PATTERNS = ((128, 1), (512, 4), (2048, 16))
B_HEADS = 8
B_HEAD_DIM = 64
D_FF = 2816
CONV_WIDTH = 3
EPS = 1e-5
DEPTH = 1
ALPHA = (2.0 * DEPTH) ** 0.25

A_COLS = A_HEADS * 2 * A_HEAD_DIM
B_COLS = B_HEADS * B_HEAD_DIM
N_GROUPS = len(B_PATTERNS)
QKV_COLS = 3 * A_COLS + 3 * N_GROUPS * B_COLS
PAIR = 2 * A_HEAD_DIM
STEPS = 128
BF16_ROWS = 16
LOG2E = math.log2(math.e)
PROJ_ROWS = 2048

VMEM_LIMIT = 56 * 1024 * 1024


def _alibi_slopes(n):
    return np.power(np.float32(2.0), -8.0 * (np.arange(n, dtype=np.float32) + 1) / n).astype(np.float32)


def _nt_dot(a, b):
    return lax.dot_general(a, b, (((1,), (1,)), ((), ())), preferred_element_type=F32)


def _layer_norm(z, g, b):
    mu = jnp.mean(z, axis=-1, keepdims=True)
    zc = z - mu
    var = jnp.mean(zc * zc, axis=-1, keepdims=True)
    return zc * lax.rsqrt(var + EPS) * g + b


def _proj_kernel(x_ref, w_ref, o_ref):
    o_ref[...] = jnp.dot(x_ref[...], w_ref[...], preferred_element_type=F32).astype(o_ref.dtype)


def _gate_kernel(x_ref, w_ref, b_ref, o_ref):
    z = jnp.dot(x_ref[...], w_ref[...], preferred_element_type=F32) + b_ref[...]
    o_ref[...] = (1.0 / (1.0 + jnp.exp(-z))).astype(o_ref.dtype)


def _projection(xb, wb, bias, tm, tn):
    m, k = xb.shape
    n = wb.shape[1]
    in_specs = [pl.BlockSpec((tm, k), lambda i, j: (i, 0)),
                pl.BlockSpec((k, tn), lambda i, j: (0, j))]
    args = [xb, wb]
    body = _proj_kernel
    if bias is not None:
        in_specs.append(pl.BlockSpec((1, tn), lambda i, j: (0, j)))
        args.append(bias)
        body = _gate_kernel
    return pl.pallas_call(
        body,
        grid=(m // tm, n // tn),
        in_specs=in_specs,
        out_specs=pl.BlockSpec((tm, tn), lambda i, j: (i, j)),
        out_shape=jax.ShapeDtypeStruct((m, n), BF16),
        compiler_params=pltpu.CompilerParams(
            dimension_semantics=("parallel", "arbitrary"), vmem_limit_bytes=VMEM_LIMIT),
        name="proj_gate" if bias is not None else "proj_qkv",
    )(*args)


def _proj_residue_kernel(x_ref, w_ref, o_ref, acc_ref, *, dil):
    acc = jnp.dot(x_ref[...], w_ref[...], preferred_element_type=F32)
    groups, tm, _ = acc_ref.shape
    for c in range(groups):
        acc_ref[c] = acc[:, c * PAIR:(c + 1) * PAIR]
    for r in range(dil):
        for c in range(groups):
            o_ref[0, r, :, c * PAIR:(c + 1) * PAIR] = \
                acc_ref[c, pl.ds(r, tm // dil, stride=dil), :].astype(o_ref.dtype)


def _projection_by_residue(xb, wb, dil, tm, tn):
    m, k = xb.shape
    n = wb.shape[1]
    if dil == 1:
        return _projection(xb, wb, None, tm, tn).reshape(m // tm, 1, tm, n)
    return pl.pallas_call(
        functools.partial(_proj_residue_kernel, dil=dil),
        grid=(m // tm, n // tn),
        in_specs=[pl.BlockSpec((tm, k), lambda i, j: (i, 0)),
                  pl.BlockSpec((k, tn), lambda i, j: (0, j))],
        out_specs=pl.BlockSpec((1, dil, tm // dil, tn), lambda i, j: (i, 0, 0, j)),
        out_shape=jax.ShapeDtypeStruct((m // tm, dil, tm // dil, n), BF16),
        scratch_shapes=[pltpu.VMEM((tn // PAIR, tm, PAIR), F32)],
        compiler_params=pltpu.CompilerParams(
            dimension_semantics=("parallel", "arbitrary"), vmem_limit_bytes=VMEM_LIMIT),
        name=f"proj_dilated_d{dil}",
    )(xb, wb)


def _diff_attn_kernel(slopes_ref, q_ref, k_ref, v_ref, lq1_ref, lk1_ref, lq2_ref, lk2_ref, subln_ref,
                      o_ref, vt_ref, bias_ref, sa_ref, sb_ref, acc_ref, *, t, lambda_init):
    h = pl.program_id(1)
    qi = pl.program_id(2)
    slope = slopes_ref[h] * LOG2E
    nchunks = vt_ref.shape[0]

    @pl.when(qi == 0)
    def _():
        def transpose_v(c, carry):
            r0 = pl.multiple_of(c * t, t)
            vt_ref[c, 0:PAIR, :] = v_ref[0, pl.ds(r0, t), :].astype(F32).T.astype(BF16)
            vt_ref[c, PAIR:, :] = jnp.ones((BF16_ROWS, t), BF16)
            return carry
        lax.fori_loop(0, nchunks, transpose_v, 0)
        krow = lax.broadcasted_iota(jnp.int32, (t, t), 0)
        qcol = lax.broadcasted_iota(jnp.int32, (t, t), 1)
        bias = slope * krow.astype(F32)
        bias_ref[0] = bias
        bias_ref[1] = jnp.where(krow <= qcol, bias, -jnp.inf)

    qt = q_ref[0].astype(F32).T * (A_HEAD_DIM ** -0.5 * LOG2E)
    dim = lax.broadcasted_iota(jnp.int32, (PAIR, t), 0)
    qt_maps = (jnp.where(dim < A_HEAD_DIM, qt, 0.0).astype(BF16),
               jnp.where(dim < A_HEAD_DIM, 0.0, qt).astype(BF16))
    acc_ref[...] = jnp.zeros(acc_ref.shape, F32)

    def scores(j, s_ref):
        kc = k_ref[0, pl.ds(pl.multiple_of(j * t, t), t), :]
        bias = bias_ref[(j == qi).astype(jnp.int32)]
        for mp in range(2):
            s_ref[mp] = jnp.dot(kc, qt_maps[mp], preferred_element_type=F32) + bias

    def update(j, s_ref, stats):
        vt = vt_ref[j]
        shift = slope * ((j - qi) * t).astype(F32)
        out = []
        for mp in range(2):
            s = s_ref[mp]
            m_new = jnp.maximum(stats[mp], jnp.max(s, axis=0, keepdims=True) + shift)
            p = jnp.exp2(s - (m_new - shift)).astype(BF16)
            alpha = jnp.exp2(stats[mp] - m_new)
            acc_ref[mp] = alpha * acc_ref[mp] + jnp.dot(vt, p, preferred_element_type=F32)
            out.append(m_new)
        return tuple(out)

    def finish():
        lam = (jnp.exp(jnp.sum(lq1_ref[...] * lk1_ref[...], axis=-1, keepdims=True))
               - jnp.exp(jnp.sum(lq2_ref[...] * lk2_ref[...], axis=-1, keepdims=True))
               + lambda_init)
        a1, a2 = acc_ref[0], acc_ref[1]
        ot = a1[:PAIR] / a1[PAIR:PAIR + 1] - lam * (a2[:PAIR] / a2[PAIR:PAIR + 1])
        ot = ot * lax.rsqrt(jnp.mean(ot * ot, axis=0, keepdims=True) + EPS)
        ot = ot * subln_ref[...] * (1.0 - lambda_init)
        o_ref[0] = ot.T.astype(o_ref.dtype)

    def pair(i, stats):
        j = 2 * i
        scores(j + 1, sb_ref)
        stats = update(j, sa_ref, stats)
        scores(j + 2, sa_ref)
        return update(j + 1, sb_ref, stats)

    neg = jnp.full((1, t), -jnp.inf, F32)
    scores(jnp.int32(0), sa_ref)
    stats = lax.fori_loop(0, qi // 2, pair, (neg, neg))

    @pl.when(qi % 2 == 1)
    def _():
        scores(qi, sb_ref)
        update(qi, sb_ref, update(qi - 1, sa_ref, stats))
        finish()

    @pl.when(qi % 2 == 0)
    def _():
        update(qi, sa_ref, stats)
        finish()


def _diff_attention(p3, slopes, lq1, lk1, lq2, lk2, subln_col, lambda_init, t):
    bsz, seq, _ = p3.shape
    vec = lambda n: pl.BlockSpec((1, n), lambda b, h, i: (0, 0))
    kernel = functools.partial(_diff_attn_kernel, t=t, lambda_init=lambda_init)
    return pl.pallas_call(
        kernel,
        grid=(bsz, A_HEADS, seq // t),
        in_specs=[
            pl.BlockSpec(memory_space=pltpu.SMEM),
            pl.BlockSpec((1, t, PAIR), lambda b, h, i: (b, i, h)),
            pl.BlockSpec((1, seq, PAIR), lambda b, h, i: (b, 0, A_HEADS + h)),
            pl.BlockSpec((1, seq, PAIR), lambda b, h, i: (b, 0, 2 * A_HEADS + h)),
            vec(A_HEAD_DIM), vec(A_HEAD_DIM), vec(A_HEAD_DIM), vec(A_HEAD_DIM),
            pl.BlockSpec((PAIR, 1), lambda b, h, i: (0, 0)),
        ],
        out_specs=pl.BlockSpec((1, t, PAIR), lambda b, h, i: (b, i, h)),
        out_shape=jax.ShapeDtypeStruct((bsz, seq, A_COLS), BF16),
        scratch_shapes=[pltpu.VMEM((seq // t, PAIR + BF16_ROWS, t), BF16), pltpu.VMEM((2, t, t), F32),
                        pltpu.VMEM((2, t, t), F32), pltpu.VMEM((2, t, t), F32),
                        pltpu.VMEM((2, PAIR + BF16_ROWS, t), F32)],
        compiler_params=pltpu.CompilerParams(
            dimension_semantics=("parallel", "parallel", "arbitrary"), vmem_limit_bytes=VMEM_LIMIT),
        name="diff_attn",
    )(slopes, p3, p3, p3, lq1, lk1, lq2, lk2, subln_col)


def _dilated_kernel(q_ref, kp_ref, kc_ref, vp_ref, vc_ref, o_ref, lse_ref, bias_ref, o_scr, lse_scr, *,
                    dil, slopes):
    n = pl.program_id(1)
    qi = lax.broadcasted_iota(jnp.int32, (STEPS, 2 * STEPS), 0)
    kj = lax.broadcasted_iota(jnp.int32, (STEPS, 2 * STEPS), 1)
    step = qi + STEPS - kj
    valid = (step >= 0) & (step <= STEPS) & (n * STEPS + kj - STEPS >= 0)
    dist = (step * dil).astype(F32)
    for hd in range(B_HEADS):
        bias_ref[hd] = jnp.where(valid, -float(slopes[hd]) * dist, -jnp.inf)
    lane = lax.broadcasted_iota(jnp.int32, (STEPS, PAIR), 1)
    low = lane < B_HEAD_DIM

    def residue(r, carry):
        q = q_ref[0, r].astype(F32) * (B_HEAD_DIM ** -0.5)
        kk = jnp.concatenate([kp_ref[0, r], kc_ref[0, r]], axis=0)
        vv = jnp.concatenate([vp_ref[0, r], vc_ref[0, r]], axis=0)
        rows = pl.ds(r, STEPS, stride=dil) if dil > 1 else slice(None)
        for pr in range(B_HEADS // 2):
            cols = slice(pr * PAIR, (pr + 1) * PAIR)
            qp, kp, vp = q[:, cols], kk[:, cols], vv[:, cols]
            o_halves, lse_halves = [], []
            for par in range(2):
                qm = jnp.where(low, qp, 0.0) if par == 0 else jnp.where(low, 0.0, qp)
                s = _nt_dot(qm.astype(BF16), kp) + bias_ref[2 * pr + par]
                m = jnp.max(s, axis=-1, keepdims=True)
                e = jnp.exp(s - m)
                den = jnp.sum(e, axis=-1, keepdims=True)
                o_halves.append(jnp.dot(e.astype(BF16), vp, preferred_element_type=F32) / den)
                lse_halves.append(m + jnp.log(den))
            o_scr[pr, rows, :] = jnp.where(low, o_halves[0], o_halves[1])
            lse_scr[pr, rows, :] = jnp.where(low, lse_halves[0], lse_halves[1])
        return carry

    lax.fori_loop(0, dil, residue, 0)
    for pr in range(B_HEADS // 2):
        cols = slice(pr * PAIR, (pr + 1) * PAIR)
        o_ref[:, cols] = o_scr[pr].astype(o_ref.dtype)
        lse_ref[:, cols] = lse_scr[pr]


def _dilated_attention(pb, bsz, seq, dil):
    tiles, _, per_res, _ = pb.shape
    span = STEPS * dil
    spans_per_tile = per_res // STEPS
    nspans = seq // span
    tiles_per_seq = tiles // bsz
    blk = (1, dil, STEPS, B_COLS)

    def at(b, n, col):
        return (b * tiles_per_seq + n // spans_per_tile, 0, n % spans_per_tile, col)

    prev = lambda n: jnp.maximum(n - 1, 0)
    kernel = functools.partial(_dilated_kernel, dil=dil, slopes=_alibi_slopes(B_HEADS))
    out_block = pl.BlockSpec((span, B_COLS), lambda b, n: (b * nspans + n, 0))
    return pl.pallas_call(
        kernel,
        grid=(bsz, nspans),
        in_specs=[
            pl.BlockSpec(blk, lambda b, n: at(b, n, 0)),
            pl.BlockSpec(blk, lambda b, n: at(b, prev(n), 1)),
            pl.BlockSpec(blk, lambda b, n: at(b, n, 1)),
            pl.BlockSpec(blk, lambda b, n: at(b, prev(n), 2)),
            pl.BlockSpec(blk, lambda b, n: at(b, n, 2)),
        ],
        out_specs=[out_block, out_block],
        out_shape=[jax.ShapeDtypeStruct((bsz * seq, B_COLS), BF16),
                   jax.ShapeDtypeStruct((bsz * seq, B_COLS), F32)],
        scratch_shapes=[pltpu.VMEM((B_HEADS, STEPS, 2 * STEPS), F32),
                        pltpu.VMEM((B_HEADS // 2, span, PAIR), F32), pltpu.VMEM((B_HEADS // 2, span, PAIR), F32)],
        compiler_params=pltpu.CompilerParams(
            dimension_semantics=("parallel", "arbitrary"), vmem_limit_bytes=VMEM_LIMIT),
        name=f"dilated_attn_d{dil}",
    )(pb, pb, pb, pb, pb)


def _merge_kernel(oa_ref, o0_ref, o1_ref, o2_ref, l0_ref, l1_ref, l2_ref, ga_ref, gb_ref, x_ref,
                  wpa_ref, wpb_ref, wo_ref, g_ref, b_ref, out_ref):
    l0, l1, l2 = l0_ref[...], l1_ref[...], l2_ref[...]
    mx = jnp.maximum(jnp.maximum(l0, l1), l2)
    e0, e1, e2 = jnp.exp(l0 - mx), jnp.exp(l1 - mx), jnp.exp(l2 - mx)
    ob = (e0 * o0_ref[...].astype(F32) + e1 * o1_ref[...].astype(F32) + e2 * o2_ref[...].astype(F32)) \
        / (e0 + e1 + e2)
    ya = jnp.dot(oa_ref[...], wpa_ref[...], preferred_element_type=F32)
    yb = jnp.dot(ob.astype(BF16), wpb_ref[...], preferred_element_type=F32)
    y = ga_ref[...].astype(F32) * ya + gb_ref[...].astype(F32) * yb
    z = ALPHA * x_ref[...] + jnp.dot(y.astype(BF16), wo_ref[...], preferred_element_type=F32)
    out_ref[...] = _layer_norm(z, g_ref[...], b_ref[...])


def _merge(oa, obs, lses, gates, x2, wpa, wpb, wo, g, b, tm):
    m = x2.shape[0]
    row = lambda w: pl.BlockSpec((tm, w), lambda i: (i, 0))
    full = lambda a: pl.BlockSpec(a.shape, lambda i: (0, 0))
    return pl.pallas_call(
        _merge_kernel,
        grid=(m // tm,),
        in_specs=[row(A_COLS), row(B_COLS), row(B_COLS), row(B_COLS), row(B_COLS), row(B_COLS), row(B_COLS),
                  pl.BlockSpec((tm, D_MODEL), lambda i: (i, 0)), pl.BlockSpec((tm, D_MODEL), lambda i: (i, 1)),
                  row(D_MODEL), full(wpa), full(wpb), full(wo), full(g), full(b)],
        out_specs=row(D_MODEL),
        out_shape=jax.ShapeDtypeStruct((m, D_MODEL), F32),
        compiler_params=pltpu.CompilerParams(dimension_semantics=("parallel",), vmem_limit_bytes=VMEM_LIMIT),
        name="merge_ln1",
    )(oa, *obs, *lses, gates, gates, x2, wpa, wpb, wo, g, b)


HALO = BF16_ROWS


def _ffn_kernel(x_ref, halo_ref, wa_ref, wg_ref, wca_ref, wcg_ref, bca_ref, bcg_ref, wd_ref, g_ref, b_ref,
                out_ref, xcat_ref, acc_ref, *, tm, tiles_per_seq):
    i = pl.program_id(0)
    c = pl.program_id(1)

    @pl.when(c == 0)
    def _():
        halo = jnp.where(i % tiles_per_seq == 0, 0.0, halo_ref[...])
        xcat_ref[0:HALO, :] = halo.astype(BF16)
        xcat_ref[HALO:, :] = x_ref[...].astype(BF16)
        acc_ref[...] = jnp.zeros(acc_ref.shape, F32)

    xcat = xcat_ref[...]

    def conv(w_ref, wc_ref, bc_ref):
        hfull = jnp.dot(xcat, w_ref[...], preferred_element_type=F32)
        out = bc_ref[...] + wc_ref[CONV_WIDTH - 1:CONV_WIDTH, :] * hfull[HALO:, :]
        for back in range(1, CONV_WIDTH):
            shifted = pltpu.roll(hfull, back, axis=0)[HALO:, :]
            out = out + wc_ref[CONV_WIDTH - 1 - back:CONV_WIDTH - back, :] * shifted
        return out

    a = conv(wa_ref, wca_ref, bca_ref)
    gv = conv(wg_ref, wcg_ref, bcg_ref)
    f = 0.5 * a * (1.0 + lax.erf(a * (2.0 ** -0.5))) * gv
    acc_ref[...] += jnp.dot(f.astype(BF16), wd_ref[...], preferred_element_type=F32)

    @pl.when(c == pl.num_programs(1) - 1)
    def _():
        z = ALPHA * x_ref[...] + acc_ref[...]
        out_ref[...] = _layer_norm(z, g_ref[...], b_ref[...])


def _ffn(x1, w_up, w_conv, b_conv, w_down, g, b, seq, tm, tc):
    m = x1.shape[0]
    nc = D_FF // tc
    halo_blocks = tm // HALO
    kernel = functools.partial(_ffn_kernel, tm=tm, tiles_per_seq=seq // tm)
    vec = pl.BlockSpec((1, D_MODEL), lambda i, c: (0, 0))
    return pl.pallas_call(
        kernel,
        grid=(m // tm, nc),
        in_specs=[
            pl.BlockSpec((tm, D_MODEL), lambda i, c: (i, 0)),
            pl.BlockSpec((HALO, D_MODEL), lambda i, c: (jnp.maximum(i * halo_blocks - 1, 0), 0)),
            pl.BlockSpec((D_MODEL, tc), lambda i, c: (0, c)),
            pl.BlockSpec((D_MODEL, tc), lambda i, c: (0, nc + c)),
            pl.BlockSpec((CONV_WIDTH, tc), lambda i, c: (0, c)),
            pl.BlockSpec((CONV_WIDTH, tc), lambda i, c: (0, nc + c)),
            pl.BlockSpec((1, tc), lambda i, c: (0, c)),
            pl.BlockSpec((1, tc), lambda i, c: (0, nc + c)),
            pl.BlockSpec((tc, D_MODEL), lambda i, c: (c, 0)),
            vec, vec,
        ],
        out_specs=pl.BlockSpec((tm, D_MODEL), lambda i, c: (i, 0)),
        out_shape=jax.ShapeDtypeStruct((m, D_MODEL), F32),
        scratch_shapes=[pltpu.VMEM((HALO + tm, D_MODEL), BF16), pltpu.VMEM((tm, D_MODEL), F32)],
        compiler_params=pltpu.CompilerParams(
            dimension_semantics=("parallel", "arbitrary"), vmem_limit_bytes=VMEM_LIMIT),
        name="ffn_ln2",
    )(x1, x1, w_up, w_up, w_conv, w_conv, b_conv, b_conv, w_down, g, b)


def kernel(x, w_in, b_gate, lambda_q1, lambda_k1, lambda_q2, lambda_k2, subln_w, w_pa, w_pb, w_o, ln1_g, ln1_b,
           w_up, w_conv, b_conv, w_down, ln2_g, ln2_b):
    bsz, seq, d = x.shape
    assert (seq, d) == (4096, D_MODEL) and w_in.shape[0] == DEPTH
    slopes_a = jnp.asarray(_alibi_slopes(A_HEADS))
    for l in range(DEPTH):
        lambda_init = 0.8 - 0.6 * math.exp(-0.3 * l)
        x2 = x.reshape(bsz * seq, d)
        xb = x2.astype(BF16)
        wb = w_in[l].astype(BF16)
        proj_a = _projection(xb, wb[:, :3 * A_COLS], None, tm=PROJ_ROWS, tn=512)
        gates = _projection(xb, wb[:, QKV_COLS:], b_gate[l][None, :], tm=PROJ_ROWS, tn=512)
        oa = _diff_attention(proj_a.reshape(bsz, seq, 3 * A_COLS), slopes_a, lambda_q1[l][None],
                             lambda_k1[l][None], lambda_q2[l][None], lambda_k2[l][None], subln_w[l][:, None],
                             lambda_init, t=512)
        obs, lses = [], []
        for g, (window, dil) in enumerate(B_PATTERNS):
            assert window // dil == STEPS
            starts = [3 * A_COLS + (part * N_GROUPS + g) * B_COLS for part in range(3)]
            wg = jnp.concatenate([wb[:, s:s + B_COLS] for s in starts], axis=1)
            pb = _projection_by_residue(xb, wg, dil, tm=PROJ_ROWS, tn=B_COLS)
            o, lse = _dilated_attention(pb, bsz, seq, dil)
            obs.append(o)
            lses.append(lse)

        x1 = _merge(oa.reshape(bsz * seq, A_COLS), obs, lses, gates, x2,
                    w_pa[l].astype(BF16), w_pb[l].astype(BF16), w_o[l].astype(BF16),
                    ln1_g[l][None], ln1_b[l][None], tm=512)
        x2 = _ffn(x1, w_up[l].astype(BF16), w_conv[l], b_conv[l][None], w_down[l].astype(BF16),
                  ln2_g[l][None], ln2_b[l][None], seq, tm=512, tc=1408)
        x = x2.reshape(bsz, seq, d)
    return x
```

```python
import functools
import math

import numpy as np
import jax
import jax.numpy as jnp
from jax import lax
from jax.experimental import pallas as pl
from jax.experimental.pallas import tpu as pltpu

BF16 = jnp.bfloat16
F32 = jnp.float32

D_MODEL = 1024
A_HEADS = 8
A_HEAD_DIM = 64
B_PATTERNS = ((128, 1), (512, 4), (2048, 16))
B_HEADS = 8
B_HEAD_DIM = 64
D_FF = 2816
CONV_WIDTH = 3
EPS = 1e-5
DEPTH = 1
ALPHA = (2.0 * DEPTH) ** 0.25

A_COLS = A_HEADS * 2 * A_HEAD_DIM
B_COLS = B_HEADS * B_HEAD_DIM
N_GROUPS = len(B_PATTERNS)
QKV_COLS = 3 * A_COLS + 3 * N_GROUPS * B_COLS
PAIR = 2 * A_HEAD_DIM
STEPS = 128
BF16_ROWS = 16
LOG2E = math.log2(math.e)
PROJ_ROWS = 2048
MXU_WIDTH = 256
FF_CHUNKS = ((0, 6 * MXU_WIDTH), (6 * MXU_WIDTH, 5 * MXU_WIDTH))

VMEM_LIMIT = 56 * 1024 * 1024


def _alibi_slopes(n):
    return np.power(np.float32(2.0), -8.0 * (np.arange(n, dtype=np.float32) + 1) / n).astype(np.float32)


def _nt_dot(a, b):
    return lax.dot_general(a, b, (((1,), (1,)), ((), ())), preferred_element_type=F32)


def _layer_norm(z, g, b):
    mu = jnp.mean(z, axis=-1, keepdims=True)
    zc = z - mu
    var = jnp.mean(zc * zc, axis=-1, keepdims=True)
    return zc * lax.rsqrt(var + EPS) * g + b


def _proj_kernel(x_ref, w_ref, o_ref):
    o_ref[...] = jnp.dot(x_ref[...], w_ref[...], preferred_element_type=F32).astype(o_ref.dtype)


def _gate_kernel(x_ref, w_ref, b_ref, o_ref):
    z = jnp.dot(x_ref[...], w_ref[...], preferred_element_type=F32) + b_ref[...]
    o_ref[...] = (1.0 / (1.0 + jnp.exp(-z))).astype(o_ref.dtype)


def _projection(xb, wb, bias, tm, tn):
    m, k = xb.shape
    n = wb.shape[1]
    in_specs = [pl.BlockSpec((tm, k), lambda i, j: (i, 0)),
                pl.BlockSpec((k, tn), lambda i, j: (0, j))]
    args = [xb, wb]
    body = _proj_kernel
    if bias is not None:
        in_specs.append(pl.BlockSpec((1, tn), lambda i, j: (0, j)))
        args.append(bias)
        body = _gate_kernel
    return pl.pallas_call(
        body,
        grid=(m // tm, n // tn),
        in_specs=in_specs,
        out_specs=pl.BlockSpec((tm, tn), lambda i, j: (i, j)),
        out_shape=jax.ShapeDtypeStruct((m, n), BF16),
        compiler_params=pltpu.CompilerParams(
            dimension_semantics=("parallel", "arbitrary"), vmem_limit_bytes=VMEM_LIMIT),
        name="proj_gate" if bias is not None else "proj_qkv",
    )(*args)


def _proj_residue_kernel(x_ref, w_ref, o_ref, acc_ref, *, dil):
    acc = jnp.dot(x_ref[...], w_ref[...], preferred_element_type=F32)
    groups, tm, _ = acc_ref.shape
    for c in range(groups):
        acc_ref[c] = acc[:, c * PAIR:(c + 1) * PAIR]
    for r in range(dil):
        for c in range(groups):
            o_ref[0, r, :, c * PAIR:(c + 1) * PAIR] = \
                acc_ref[c, pl.ds(r, tm // dil, stride=dil), :].astype(o_ref.dtype)


def _projection_by_residue(xb, wb, dil, tm, tn):
    m, k = xb.shape
    n = wb.shape[1]
    if dil == 1:
        return _projection(xb, wb, None, tm, tn).reshape(m // tm, 1, tm, n)
    return pl.pallas_call(
        functools.partial(_proj_residue_kernel, dil=dil),
        grid=(m // tm, n // tn),
        in_specs=[pl.BlockSpec((tm, k), lambda i, j: (i, 0)),
                  pl.BlockSpec((k, tn), lambda i, j: (0, j))],
        out_specs=pl.BlockSpec((1, dil, tm // dil, tn), lambda i, j: (i, 0, 0, j)),
        out_shape=jax.ShapeDtypeStruct((m // tm, dil, tm // dil, n), BF16),
        scratch_shapes=[pltpu.VMEM((tn // PAIR, tm, PAIR), F32)],
        compiler_params=pltpu.CompilerParams(
            dimension_semantics=("parallel", "arbitrary"), vmem_limit_bytes=VMEM_LIMIT),
        name=f"proj_dilated_d{dil}",
    )(xb, wb)


def _diff_attn_kernel(slopes_ref, q_ref, k_ref, v_ref, lq1_ref, lk1_ref, lq2_ref, lk2_ref, subln_ref,
                      o_ref, vt_ref, bias_ref, sa_ref, sb_ref, acc_ref, *, t, lambda_init):
    h = pl.program_id(1)
    qi = pl.program_id(2)
    slope = slopes_ref[h] * LOG2E
    nchunks = vt_ref.shape[0]

    @pl.when(qi == 0)
    def _():
        def transpose_v(c, carry):
            r0 = pl.multiple_of(c * t, t)
            vt_ref[c, 0:PAIR, :] = v_ref[0, pl.ds(r0, t), :].astype(F32).T.astype(BF16)
            vt_ref[c, PAIR:, :] = jnp.ones((BF16_ROWS, t), BF16)
            return carry
        lax.fori_loop(0, nchunks, transpose_v, 0)
        krow = lax.broadcasted_iota(jnp.int32, (t, t), 0)
        qcol = lax.broadcasted_iota(jnp.int32, (t, t), 1)
        bias = slope * krow.astype(F32)
        bias_ref[0] = bias
        bias_ref[1] = jnp.where(krow <= qcol, bias, -jnp.inf)

    qt = q_ref[0].astype(F32).T * (A_HEAD_DIM ** -0.5 * LOG2E)
    dim = lax.broadcasted_iota(jnp.int32, (PAIR, t), 0)
    qt_maps = (jnp.where(dim < A_HEAD_DIM, qt, 0.0).astype(BF16),
               jnp.where(dim < A_HEAD_DIM, 0.0, qt).astype(BF16))
    acc_ref[...] = jnp.zeros(acc_ref.shape, F32)

    def scores(j, s_ref):
        kc = k_ref[0, pl.ds(pl.multiple_of(j * t, t), t), :]
        bias = bias_ref[(j == qi).astype(jnp.int32)]
        for mp in range(2):
            s_ref[mp] = jnp.dot(kc, qt_maps[mp], preferred_element_type=F32) + bias

    def update(j, s_ref, stats):
        vt = vt_ref[j]
        shift = slope * ((j - qi) * t).astype(F32)
        out = []
        for mp in range(2):
            s = s_ref[mp]
            m_new = jnp.maximum(stats[mp], jnp.max(s, axis=0, keepdims=True) + shift)
            p = jnp.exp2(s - (m_new - shift)).astype(BF16)
            alpha = jnp.exp2(stats[mp] - m_new)
            acc_ref[mp] = alpha * acc_ref[mp] + jnp.dot(vt, p, preferred_element_type=F32)
            out.append(m_new)
        return tuple(out)

    def finish():
        lam = (jnp.exp(jnp.sum(lq1_ref[...] * lk1_ref[...], axis=-1, keepdims=True))
               - jnp.exp(jnp.sum(lq2_ref[...] * lk2_ref[...], axis=-1, keepdims=True))
               + lambda_init)
        a1, a2 = acc_ref[0], acc_ref[1]
        ot = a1[:PAIR] / a1[PAIR:PAIR + 1] - lam * (a2[:PAIR] / a2[PAIR:PAIR + 1])
        ot = ot * lax.rsqrt(jnp.mean(ot * ot, axis=0, keepdims=True) + EPS)
        ot = ot * subln_ref[...] * (1.0 - lambda_init)
        o_ref[0] = ot.T.astype(o_ref.dtype)

    def pair(i, stats):
        j = 2 * i
        scores(j + 1, sb_ref)
        stats = update(j, sa_ref, stats)
        scores(j + 2, sa_ref)
        return update(j + 1, sb_ref, stats)

    neg = jnp.full((1, t), -jnp.inf, F32)
    scores(jnp.int32(0), sa_ref)
    stats = lax.fori_loop(0, qi // 2, pair, (neg, neg))

    @pl.when(qi % 2 == 1)
    def _():
        scores(qi, sb_ref)
        update(qi, sb_ref, update(qi - 1, sa_ref, stats))
        finish()

    @pl.when(qi % 2 == 0)
    def _():
        update(qi, sa_ref, stats)
        finish()


def _diff_attention(p3, slopes, lq1, lk1, lq2, lk2, subln_col, lambda_init, t):
    bsz, seq, _ = p3.shape
    vec = lambda n: pl.BlockSpec((1, n), lambda b, h, i: (0, 0))
    kernel = functools.partial(_diff_attn_kernel, t=t, lambda_init=lambda_init)
    return pl.pallas_call(
        kernel,
        grid=(bsz, A_HEADS, seq // t),
        in_specs=[
            pl.BlockSpec(memory_space=pltpu.SMEM),
            pl.BlockSpec((1, t, PAIR), lambda b, h, i: (b, i, h)),
            pl.BlockSpec((1, seq, PAIR), lambda b, h, i: (b, 0, A_HEADS + h)),
            pl.BlockSpec((1, seq, PAIR), lambda b, h, i: (b, 0, 2 * A_HEADS + h)),
            vec(A_HEAD_DIM), vec(A_HEAD_DIM), vec(A_HEAD_DIM), vec(A_HEAD_DIM),
            pl.BlockSpec((PAIR, 1), lambda b, h, i: (0, 0)),
        ],
        out_specs=pl.BlockSpec((1, t, PAIR), lambda b, h, i: (b, i, h)),
        out_shape=jax.ShapeDtypeStruct((bsz, seq, A_COLS), BF16),
        scratch_shapes=[pltpu.VMEM((seq // t, PAIR + BF16_ROWS, t), BF16), pltpu.VMEM((2, t, t), F32),
                        pltpu.VMEM((2, t, t), F32), pltpu.VMEM((2, t, t), F32),
                        pltpu.VMEM((2, PAIR + BF16_ROWS, t), F32)],
        compiler_params=pltpu.CompilerParams(
            dimension_semantics=("parallel", "parallel", "arbitrary"), vmem_limit_bytes=VMEM_LIMIT),
        name="diff_attn",
    )(slopes, p3, p3, p3, lq1, lk1, lq2, lk2, subln_col)


def _dilated_kernel(q_ref, k_ref, v_ref, kprev_ref, vprev_ref, o_ref, lse_ref, bias_ref, o_scr, lse_scr, *,
                    dil, slopes, tiles_per_seq):
    i = pl.program_id(0)
    blocks = q_ref.shape[2] // STEPS

    @pl.when(i == 0)
    def _():
        qi = lax.broadcasted_iota(jnp.int32, (STEPS, 2 * STEPS), 0)
        kj = lax.broadcasted_iota(jnp.int32, (STEPS, 2 * STEPS), 1)
        step = qi + STEPS - kj
        window = (step >= 0) & (step <= STEPS)
        dist = (step * dil).astype(F32)
        for hd in range(B_HEADS):
            alibi = -float(slopes[hd]) * dist
            bias_ref[0, hd] = jnp.where(window & (kj >= STEPS), alibi, -jnp.inf)
            bias_ref[1, hd] = jnp.where(window, alibi, -jnp.inf)

    lane = lax.broadcasted_iota(jnp.int32, (STEPS, PAIR), 1)
    low = lane < B_HEAD_DIM
    ones = jnp.ones((2 * STEPS, PAIR), BF16)

    def unit(u, carry):
        r = u // blocks
        nl = u % blocks
        has_prev = jnp.logical_or(nl > 0, i % tiles_per_seq != 0)
        variant = has_prev.astype(jnp.int32)
        cur = pl.ds(pl.multiple_of(nl * STEPS, STEPS), STEPS)
        before = pl.ds(pl.multiple_of(jnp.maximum(nl - 1, 0) * STEPS, STEPS), STEPS)
        q = q_ref[0, r, cur, :].astype(F32) * (B_HEAD_DIM ** -0.5)
        k_before = jnp.where(nl > 0, k_ref[0, r, before, :], kprev_ref[0, r])
        v_before = jnp.where(nl > 0, v_ref[0, r, before, :], vprev_ref[0, r])
        kk = jnp.concatenate([k_before, k_ref[0, r, cur, :]], axis=0)
        vv = jnp.concatenate([v_before, v_ref[0, r, cur, :]], axis=0)
        start = nl * (STEPS * dil) + r
        rows = pl.ds(start, STEPS, stride=dil) if dil > 1 else pl.ds(pl.multiple_of(start, STEPS), STEPS)
        for pr in range(B_HEADS // 2):
            cols = slice(pr * PAIR, (pr + 1) * PAIR)
            qp, kp = q[:, cols], kk[:, cols]
            v_ones = jnp.concatenate([vv[:, cols], ones], axis=1)
            o_halves, lse_halves = [], []
            for par in range(2):
                qm = jnp.where(low, qp, 0.0) if par == 0 else jnp.where(low, 0.0, qp)
                s = _nt_dot(qm.astype(BF16), kp) + bias_ref[variant, 2 * pr + par]
                m = jnp.max(s, axis=-1, keepdims=True)
                e = jnp.exp(s - m).astype(BF16)
                o_den = jnp.dot(e, v_ones, preferred_element_type=F32)
                den = o_den[:, PAIR:]
                o_halves.append(o_den[:, :PAIR] / den)
                lse_halves.append(m + jnp.log(den))
            o_scr[pr, rows, :] = jnp.where(low, o_halves[0], o_halves[1])
            lse_scr[pr, rows, :] = jnp.where(low, lse_halves[0], lse_halves[1])
        return carry

    lax.fori_loop(0, dil * blocks, unit, 0, unroll=4)
    for pr in range(B_HEADS // 2):
        cols = slice(pr * PAIR, (pr + 1) * PAIR)
        o_ref[:, cols] = o_scr[pr].astype(o_ref.dtype)
        lse_ref[:, cols] = lse_scr[pr]


def _dilated_attention(pb, bsz, seq, dil):
    tiles, _, per_res, _ = pb.shape
    tm = dil * per_res
    tile_blk = (1, dil, per_res, B_COLS)
    prev_blk = (1, dil, STEPS, B_COLS)
    last = per_res // STEPS - 1
    prev = lambda i: jnp.maximum(i - 1, 0)
    kernel = functools.partial(_dilated_kernel, dil=dil, slopes=_alibi_slopes(B_HEADS),
                               tiles_per_seq=tiles // bsz)
    out_block = pl.BlockSpec((tm, B_COLS), lambda i: (i, 0))
    return pl.pallas_call(
        kernel,
        grid=(tiles,),
        in_specs=[
            pl.BlockSpec(tile_blk, lambda i: (i, 0, 0, 0)),
            pl.BlockSpec(tile_blk, lambda i: (i, 0, 0, 1)),
            pl.BlockSpec(tile_blk, lambda i: (i, 0, 0, 2)),
            pl.BlockSpec(prev_blk, lambda i: (prev(i), 0, last, 1)),
            pl.BlockSpec(prev_blk, lambda i: (prev(i), 0, last, 2)),
        ],
        out_specs=[out_block, out_block],
        out_shape=[jax.ShapeDtypeStruct((bsz * seq, B_COLS), BF16),
                   jax.ShapeDtypeStruct((bsz * seq, B_COLS), F32)],
        scratch_shapes=[pltpu.VMEM((2, B_HEADS, STEPS, 2 * STEPS), F32),
                        pltpu.VMEM((B_HEADS // 2, tm, PAIR), F32), pltpu.VMEM((B_HEADS // 2, tm, PAIR), F32)],
        compiler_params=pltpu.CompilerParams(
            dimension_semantics=("arbitrary",), vmem_limit_bytes=VMEM_LIMIT),
        name=f"dilated_attn_d{dil}",
    )(pb, pb, pb, pb, pb)


def _merge_kernel(oa_ref, o0_ref, o1_ref, o2_ref, l0_ref, l1_ref, l2_ref, ga_ref, gb_ref, x_ref,
                  wpa_ref, wpb_ref, wo_ref, g_ref, b_ref, out_ref):
    l0, l1, l2 = l0_ref[...], l1_ref[...], l2_ref[...]
    mx = jnp.maximum(jnp.maximum(l0, l1), l2)
    e0, e1, e2 = jnp.exp(l0 - mx), jnp.exp(l1 - mx), jnp.exp(l2 - mx)
    ob = (e0 * o0_ref[...].astype(F32) + e1 * o1_ref[...].astype(F32) + e2 * o2_ref[...].astype(F32)) \
        / (e0 + e1 + e2)
    ya = jnp.dot(oa_ref[...], wpa_ref[...], preferred_element_type=F32)
    yb = jnp.dot(ob.astype(BF16), wpb_ref[...], preferred_element_type=F32)
    y = ga_ref[...].astype(F32) * ya + gb_ref[...].astype(F32) * yb
    z = ALPHA * x_ref[...] + jnp.dot(y.astype(BF16), wo_ref[...], preferred_element_type=F32)
    out_ref[...] = _layer_norm(z, g_ref[...], b_ref[...])


def _merge(oa, obs, lses, gates, x2, wpa, wpb, wo, g, b, tm):
    m = x2.shape[0]
    row = lambda w: pl.BlockSpec((tm, w), lambda i: (i, 0))
    full = lambda a: pl.BlockSpec(a.shape, lambda i: (0, 0))
    return pl.pallas_call(
        _merge_kernel,
        grid=(m // tm,),
        in_specs=[row(A_COLS), row(B_COLS), row(B_COLS), row(B_COLS), row(B_COLS), row(B_COLS), row(B_COLS),
                  pl.BlockSpec((tm, D_MODEL), lambda i: (i, 0)), pl.BlockSpec((tm, D_MODEL), lambda i: (i, 1)),
                  row(D_MODEL), full(wpa), full(wpb), full(wo), full(g), full(b)],
        out_specs=row(D_MODEL),
        out_shape=jax.ShapeDtypeStruct((m, D_MODEL), F32),
        compiler_params=pltpu.CompilerParams(dimension_semantics=("parallel",), vmem_limit_bytes=VMEM_LIMIT),
        name="merge_ln1",
    )(oa, *obs, *lses, gates, gates, x2, wpa, wpb, wo, g, b)


HALO = BF16_ROWS


def _ffn_kernel(x_ref, halo_ref, wup_ref, wconv_ref, bconv_ref, wd_ref, g_ref, b_ref,
                out_ref, xcat_ref, acc_ref, *, tiles_per_seq):
    i = pl.program_id(0)
    halo = jnp.where(i % tiles_per_seq == 0, 0.0, halo_ref[...])
    xcat_ref[0:HALO, :] = halo.astype(BF16)
    xcat_ref[HALO:, :] = x_ref[...].astype(BF16)
    xcat = xcat_ref[...]

    def conv(col0, width):
        cols = slice(col0, col0 + width)
        hfull = jnp.dot(xcat, wup_ref[:, cols], preferred_element_type=F32)
        out = bconv_ref[:, cols] + wconv_ref[CONV_WIDTH - 1:CONV_WIDTH, cols] * hfull[HALO:, :]
        for back in range(1, CONV_WIDTH):
            shifted = pltpu.roll(hfull, back, axis=0)[HALO:, :]
            out = out + wconv_ref[CONV_WIDTH - 1 - back:CONV_WIDTH - back, cols] * shifted
        return out

    for col0, width in FF_CHUNKS:
        a = conv(col0, width)
        gv = conv(D_FF + col0, width)
        f = 0.5 * a * (1.0 + lax.erf(a * (2.0 ** -0.5))) * gv
        y = jnp.dot(f.astype(BF16), wd_ref[col0:col0 + width, :], preferred_element_type=F32)
        if col0 == 0:
            acc_ref[...] = y
        else:
            acc_ref[...] += y

    z = ALPHA * x_ref[...] + acc_ref[...]
    out_ref[...] = _layer_norm(z, g_ref[...], b_ref[...])


def _ffn(x1, w_up, w_conv, b_conv, w_down, g, b, seq, tm):
    m = x1.shape[0]
    halo_blocks = tm // HALO
    kernel = functools.partial(_ffn_kernel, tiles_per_seq=seq // tm)
    full = lambda a: pl.BlockSpec(a.shape, lambda i: (0, 0))
    return pl.pallas_call(
        kernel,
        grid=(m // tm,),
        in_specs=[
            pl.BlockSpec((tm, D_MODEL), lambda i: (i, 0)),
            pl.BlockSpec((HALO, D_MODEL), lambda i: (jnp.maximum(i * halo_blocks - 1, 0), 0)),
            full(w_up), full(w_conv), full(b_conv), full(w_down), full(g), full(b),
        ],
        out_specs=pl.BlockSpec((tm, D_MODEL), lambda i: (i, 0)),
        out_shape=jax.ShapeDtypeStruct((m, D_MODEL), F32),
        scratch_shapes=[pltpu.VMEM((HALO + tm, D_MODEL), BF16), pltpu.VMEM((tm, D_MODEL), F32)],
        compiler_params=pltpu.CompilerParams(dimension_semantics=("parallel",), vmem_limit_bytes=VMEM_LIMIT),
        name="ffn_ln2",
    )(x1, x1, w_up, w_conv, b_conv, w_down, g, b)


def kernel(x, w_in, b_gate, lambda_q1, lambda_k1, lambda_q2, lambda_k2, subln_w, w_pa, w_pb, w_o, ln1_g, ln1_b,
           w_up, w_conv, b_conv, w_down, ln2_g, ln2_b):
    bsz, seq, d = x.shape
    assert (seq, d) == (4096, D_MODEL) and w_in.shape[0] == DEPTH
    slopes_a = jnp.asarray(_alibi_slopes(A_HEADS))
    for l in range(DEPTH):
        lambda_init = 0.8 - 0.6 * math.exp(-0.3 * l)
        x2 = x.reshape(bsz * seq, d)
        xb = x2.astype(BF16)
        wb = w_in[l].astype(BF16)
        proj_a = _projection(xb, wb[:, :3 * A_COLS], None, tm=PROJ_ROWS, tn=1024)
        gates = _projection(xb, wb[:, QKV_COLS:], b_gate[l][None, :], tm=PROJ_ROWS, tn=1024)
        oa = _diff_attention(proj_a.reshape(bsz, seq, 3 * A_COLS), slopes_a, lambda_q1[l][None],
                             lambda_k1[l][None], lambda_q2[l][None], lambda_k2[l][None], subln_w[l][:, None],
                             lambda_init, t=512)
        obs, lses = [], []
        for g, (window, dil) in enumerate(B_PATTERNS):
            assert window // dil == STEPS
            starts = [3 * A_COLS + (part * N_GROUPS + g) * B_COLS for part in range(3)]
            wg = jnp.concatenate([wb[:, s:s + B_COLS] for s in starts], axis=1)
            pb = _projection_by_residue(xb, wg, dil, tm=PROJ_ROWS, tn=B_COLS)
            o, lse = _dilated_attention(pb, bsz, seq, dil)
            obs.append(o)
            lses.append(lse)

        x1 = _merge(oa.reshape(bsz * seq, A_COLS), obs, lses, gates, x2,
                    w_pa[l].astype(BF16), w_pb[l].astype(BF16), w_o[l].astype(BF16),
                    ln1_g[l][None], ln1_b[l][None], tm=512)
        x2 = _ffn(x1, w_up[l].astype(BF16), w_conv[l], b_conv[l][None], w_down[l].astype(BF16),
                  ln2_g[l][None], ln2_b[l][None], seq, tm=512)
        x = x2.reshape(bsz, seq, d)
    return x
```

```python
import functools
import math

import numpy as np
import jax
import jax.numpy as jnp
from jax import lax
from jax.experimental import pallas as pl
from jax.experimental.pallas import tpu as pltpu

BF16 = jnp.bfloat16
F32 = jnp.float32

D_MODEL = 1024
A_HEADS = 8
A_HEAD_DIM = 64
B_PATTERNS = ((128, 1), (512, 4), (2048, 16))
B_HEADS = 8
B_HEAD_DIM = 64
D_FF = 2816
CONV_WIDTH = 3
EPS = 1e-5
DEPTH = 1
ALPHA = (2.0 * DEPTH) ** 0.25

A_COLS = A_HEADS * 2 * A_HEAD_DIM
B_COLS = B_HEADS * B_HEAD_DIM
N_GROUPS = len(B_PATTERNS)
QKV_COLS = 3 * A_COLS + 3 * N_GROUPS * B_COLS
PAIR = 2 * A_HEAD_DIM
STEPS = 128
BF16_ROWS = 16
LOG2E = math.log2(math.e)
PROJ_ROWS = 2048
MXU_WIDTH = 256
FF_CHUNKS = ((0, 6 * MXU_WIDTH), (6 * MXU_WIDTH, 5 * MXU_WIDTH))

VMEM_LIMIT = 56 * 1024 * 1024


def _alibi_slopes(n):
    return np.power(np.float32(2.0), -8.0 * (np.arange(n, dtype=np.float32) + 1) / n).astype(np.float32)


def _nt_dot(a, b):
    return lax.dot_general(a, b, (((1,), (1,)), ((), ())), preferred_element_type=F32)


def _layer_norm(z, g, b):
    mu = jnp.mean(z, axis=-1, keepdims=True)
    zc = z - mu
    var = jnp.mean(zc * zc, axis=-1, keepdims=True)
    return zc * lax.rsqrt(var + EPS) * g + b


def _proj_kernel(x_ref, w_ref, o_ref):
    o_ref[...] = jnp.dot(x_ref[...], w_ref[...], preferred_element_type=F32).astype(o_ref.dtype)


def _gate_kernel(x_ref, w_ref, b_ref, o_ref):
    z = jnp.dot(x_ref[...], w_ref[...], preferred_element_type=F32) + b_ref[...]
    o_ref[...] = (1.0 / (1.0 + jnp.exp(-z))).astype(o_ref.dtype)


def _projection(xb, wb, bias, tm, tn):
    m, k = xb.shape
    n = wb.shape[1]
    in_specs = [pl.BlockSpec((tm, k), lambda i, j: (i, 0)),
                pl.BlockSpec((k, tn), lambda i, j: (0, j))]
    args = [xb, wb]
    body = _proj_kernel
    if bias is not None:
        in_specs.append(pl.BlockSpec((1, tn), lambda i, j: (0, j)))
        args.append(bias)
        body = _gate_kernel
    return pl.pallas_call(
        body,
        grid=(m // tm, n // tn),
        in_specs=in_specs,
        out_specs=pl.BlockSpec((tm, tn), lambda i, j: (i, j)),
        out_shape=jax.ShapeDtypeStruct((m, n), BF16),
        compiler_params=pltpu.CompilerParams(
            dimension_semantics=("parallel", "arbitrary"), vmem_limit_bytes=VMEM_LIMIT),
        name="proj_gate" if bias is not None else "proj_qkv",
    )(*args)


def _proj_residue_kernel(x_ref, w_ref, o_ref, acc_ref, *, dil):
    acc = jnp.dot(x_ref[...], w_ref[...], preferred_element_type=F32)
    groups, tm, _ = acc_ref.shape
    for c in range(groups):
        acc_ref[c] = acc[:, c * PAIR:(c + 1) * PAIR]
    for r in range(dil):
        for c in range(groups):
            o_ref[0, r, :, c * PAIR:(c + 1) * PAIR] = \
                acc_ref[c, pl.ds(r, tm // dil, stride=dil), :].astype(o_ref.dtype)


def _projection_by_residue(xb, wb, dil, tm, tn):
    m, k = xb.shape
    n = wb.shape[1]
    if dil == 1:
        return _projection(xb, wb, None, tm, tn).reshape(m // tm, 1, tm, n)
    return pl.pallas_call(
        functools.partial(_proj_residue_kernel, dil=dil),
        grid=(m // tm, n // tn),
        in_specs=[pl.BlockSpec((tm, k), lambda i, j: (i, 0)),
                  pl.BlockSpec((k, tn), lambda i, j: (0, j))],
        out_specs=pl.BlockSpec((1, dil, tm // dil, tn), lambda i, j: (i, 0, 0, j)),
        out_shape=jax.ShapeDtypeStruct((m // tm, dil, tm // dil, n), BF16),
        scratch_shapes=[pltpu.VMEM((tn // PAIR, tm, PAIR), F32)],
        compiler_params=pltpu.CompilerParams(
            dimension_semantics=("parallel", "arbitrary"), vmem_limit_bytes=VMEM_LIMIT),
        name=f"proj_dilated_d{dil}",
    )(xb, wb)


def _diff_attn_kernel(slopes_ref, q_ref, k_ref, v_ref, lq1_ref, lk1_ref, lq2_ref, lk2_ref, subln_ref,
                      o_ref, vt_ref, bias_ref, sa_ref, sb_ref, acc_ref, *, t, lambda_init):
    slope = slopes_ref[pl.program_id(1)] * LOG2E
    nchunks = vt_ref.shape[0]

    def transpose_v(c, carry):
        r0 = pl.multiple_of(c * t, t)
        vt_ref[c, 0:PAIR, :] = v_ref[0, pl.ds(r0, t), :].astype(F32).T.astype(BF16)
        vt_ref[c, PAIR:, :] = jnp.ones((BF16_ROWS, t), BF16)
        return carry
    lax.fori_loop(0, nchunks, transpose_v, 0)
    krow = lax.broadcasted_iota(jnp.int32, (t, t), 0)
    qcol = lax.broadcasted_iota(jnp.int32, (t, t), 1)
    bias = slope * krow.astype(F32)
    bias_ref[0] = bias
    bias_ref[1] = jnp.where(krow <= qcol, bias, -jnp.inf)

    lam = (jnp.exp(jnp.sum(lq1_ref[...] * lk1_ref[...], axis=-1, keepdims=True))
           - jnp.exp(jnp.sum(lq2_ref[...] * lk2_ref[...], axis=-1, keepdims=True))
           + lambda_init)
    dim = lax.broadcasted_iota(jnp.int32, (PAIR, t), 0)
    neg = jnp.full((1, t), -jnp.inf, F32)
    s_refs = (sa_ref, sb_ref)

    for qi in range(nchunks):
        rows = slice(qi * t, (qi + 1) * t)
        acc = acc_ref.at[qi % 2]
        qt = q_ref[0, rows, :].astype(F32).T * (A_HEAD_DIM ** -0.5 * LOG2E)
        qt_maps = (jnp.where(dim < A_HEAD_DIM, qt, 0.0).astype(BF16),
                   jnp.where(dim < A_HEAD_DIM, 0.0, qt).astype(BF16))
        acc[...] = jnp.zeros(acc.shape, F32)

        def scores(j, s_ref, qi=qi, qt_maps=qt_maps):
            kc = k_ref[0, pl.ds(pl.multiple_of(j * t, t), t), :]
            bias = bias_ref[(j == qi).astype(jnp.int32)]
            for mp in range(2):
                s_ref[mp] = jnp.dot(kc, qt_maps[mp], preferred_element_type=F32) + bias

        def update(j, s_ref, stats, qi=qi, acc=acc):
            vt = vt_ref[j]
            shift = slope * ((j - qi) * t).astype(F32)
            out = []
            for mp in range(2):
                s = s_ref[mp]
                m_new = jnp.maximum(stats[mp], jnp.max(s, axis=0, keepdims=True) + shift)
                p = jnp.exp2(s - (m_new - shift)).astype(BF16)
                alpha = jnp.exp2(stats[mp] - m_new)
                acc[mp] = alpha * acc[mp] + jnp.dot(vt, p, preferred_element_type=F32)
                out.append(m_new)
            return tuple(out)

        first, second = s_refs

        def pair(i, stats, scores=scores, update=update, first=first, second=second):
            j = 2 * i
            scores(j + 1, second)
            stats = update(j, first, stats)
            scores(j + 2, first)
            return update(j + 1, second, stats)

        scores(jnp.int32(0), first)
        stats = lax.fori_loop(0, qi // 2, pair, (neg, neg))
        if qi % 2 == 1:
            scores(jnp.int32(qi), second)
            update(jnp.int32(qi), second, update(jnp.int32(qi - 1), first, stats))
        else:
            update(jnp.int32(qi), first, stats)
            s_refs = (second, first)

        a1, a2 = acc[0], acc[1]
        ot = a1[:PAIR] / a1[PAIR:PAIR + 1] - lam * (a2[:PAIR] / a2[PAIR:PAIR + 1])
        ot = ot * lax.rsqrt(jnp.mean(ot * ot, axis=0, keepdims=True) + EPS)
        ot = ot * subln_ref[...] * (1.0 - lambda_init)
        o_ref[0, rows, :] = ot.T.astype(o_ref.dtype)


def _diff_attention(p3, slopes, lq1, lk1, lq2, lk2, subln_col, lambda_init, t):
    bsz, seq, _ = p3.shape
    vec = lambda n: pl.BlockSpec((1, n), lambda b, h: (0, 0))
    kernel = functools.partial(_diff_attn_kernel, t=t, lambda_init=lambda_init)
    return pl.pallas_call(
        kernel,
        grid=(bsz, A_HEADS),
        in_specs=[
            pl.BlockSpec(memory_space=pltpu.SMEM),
            pl.BlockSpec((1, seq, PAIR), lambda b, h: (b, 0, h)),
            pl.BlockSpec((1, seq, PAIR), lambda b, h: (b, 0, A_HEADS + h)),
            pl.BlockSpec((1, seq, PAIR), lambda b, h: (b, 0, 2 * A_HEADS + h)),
            vec(A_HEAD_DIM), vec(A_HEAD_DIM), vec(A_HEAD_DIM), vec(A_HEAD_DIM),
            pl.BlockSpec((PAIR, 1), lambda b, h: (0, 0)),
        ],
        out_specs=pl.BlockSpec((1, seq, PAIR), lambda b, h: (b, 0, h)),
        out_shape=jax.ShapeDtypeStruct((bsz, seq, A_COLS), BF16),
        scratch_shapes=[pltpu.VMEM((seq // t, PAIR + BF16_ROWS, t), BF16), pltpu.VMEM((2, t, t), F32),
                        pltpu.VMEM((2, t, t), F32), pltpu.VMEM((2, t, t), F32),
                        pltpu.VMEM((2, 2, PAIR + BF16_ROWS, t), F32)],
        compiler_params=pltpu.CompilerParams(
            dimension_semantics=("parallel", "parallel"), vmem_limit_bytes=VMEM_LIMIT),
        name="diff_attn",
    )(slopes, p3, p3, p3, lq1, lk1, lq2, lk2, subln_col)


def _dilated_kernel(q_ref, k_ref, v_ref, kprev_ref, vprev_ref, o_ref, lse_ref, bias_ref, o_scr, lse_scr, *,
                    dil, slopes, tiles_per_seq):
    i = pl.program_id(0)
    blocks = q_ref.shape[2] // STEPS

    @pl.when(i == 0)
    def _():
        qi = lax.broadcasted_iota(jnp.int32, (STEPS, 2 * STEPS), 0)
        kj = lax.broadcasted_iota(jnp.int32, (STEPS, 2 * STEPS), 1)
        step = qi + STEPS - kj
        window = (step >= 0) & (step <= STEPS)
        dist = (step * dil).astype(F32)
        for hd in range(B_HEADS):
            alibi = -float(slopes[hd]) * dist
            bias_ref[0, hd] = jnp.where(window & (kj >= STEPS), alibi, -jnp.inf)
            bias_ref[1, hd] = jnp.where(window, alibi, -jnp.inf)

    lane = lax.broadcasted_iota(jnp.int32, (STEPS, PAIR), 1)
    low = lane < B_HEAD_DIM
    ones = jnp.ones((2 * STEPS, PAIR), BF16)

    def unit(u, carry):
        r = u // blocks
        nl = u % blocks
        has_prev = jnp.logical_or(nl > 0, i % tiles_per_seq != 0)
        variant = has_prev.astype(jnp.int32)
        cur = pl.ds(pl.multiple_of(nl * STEPS, STEPS), STEPS)
        before = pl.ds(pl.multiple_of(jnp.maximum(nl - 1, 0) * STEPS, STEPS), STEPS)
        q = q_ref[0, r, cur, :].astype(F32) * (B_HEAD_DIM ** -0.5)
        k_before = jnp.where(nl > 0, k_ref[0, r, before, :], kprev_ref[0, r])
        v_before = jnp.where(nl > 0, v_ref[0, r, before, :], vprev_ref[0, r])
        kk = jnp.concatenate([k_before, k_ref[0, r, cur, :]], axis=0)
        vv = jnp.concatenate([v_before, v_ref[0, r, cur, :]], axis=0)
        start = nl * (STEPS * dil) + r
        rows = pl.ds(start, STEPS, stride=dil) if dil > 1 else pl.ds(pl.multiple_of(start, STEPS), STEPS)
        for pr in range(B_HEADS // 2):
            cols = slice(pr * PAIR, (pr + 1) * PAIR)
            qp, kp = q[:, cols], kk[:, cols]
            v_ones = jnp.concatenate([vv[:, cols], ones], axis=1)
            o_halves, lse_halves = [], []
            for par in range(2):
                qm = jnp.where(low, qp, 0.0) if par == 0 else jnp.where(low, 0.0, qp)
                s = _nt_dot(qm.astype(BF16), kp) + bias_ref[variant, 2 * pr + par]
                m = jnp.max(s, axis=-1, keepdims=True)
                e = jnp.exp(s - m).astype(BF16)
                o_den = jnp.dot(e, v_ones, preferred_element_type=F32)
                den = o_den[:, PAIR:]
                o_halves.append(o_den[:, :PAIR] / den)
                lse_halves.append(m + jnp.log(den))
            o_scr[pr, rows, :] = jnp.where(low, o_halves[0], o_halves[1])
            lse_scr[pr, rows, :] = jnp.where(low, lse_halves[0], lse_halves[1])
        return carry

    lax.fori_loop(0, dil * blocks, unit, 0, unroll=4)
    for pr in range(B_HEADS // 2):
        cols = slice(pr * PAIR, (pr + 1) * PAIR)
        o_ref[:, cols] = o_scr[pr].astype(o_ref.dtype)
        lse_ref[:, cols] = lse_scr[pr]


def _dilated_attention(pb, bsz, seq, dil):
    tiles, _, per_res, _ = pb.shape
    tm = dil * per_res
    tile_blk = (1, dil, per_res, B_COLS)
    prev_blk = (1, dil, STEPS, B_COLS)
    last = per_res // STEPS - 1
    prev = lambda i: jnp.maximum(i - 1, 0)
    kernel = functools.partial(_dilated_kernel, dil=dil, slopes=_alibi_slopes(B_HEADS),
                               tiles_per_seq=tiles // bsz)
    out_block = pl.BlockSpec((tm, B_COLS), lambda i: (i, 0))
    return pl.pallas_call(
        kernel,
        grid=(tiles,),
        in_specs=[
            pl.BlockSpec(tile_blk, lambda i: (i, 0, 0, 0)),
            pl.BlockSpec(tile_blk, lambda i: (i, 0, 0, 1)),
            pl.BlockSpec(tile_blk, lambda i: (i, 0, 0, 2)),
            pl.BlockSpec(prev_blk, lambda i: (prev(i), 0, last, 1)),
            pl.BlockSpec(prev_blk, lambda i: (prev(i), 0, last, 2)),
        ],
        out_specs=[out_block, out_block],
        out_shape=[jax.ShapeDtypeStruct((bsz * seq, B_COLS), BF16),
                   jax.ShapeDtypeStruct((bsz * seq, B_COLS), F32)],
        scratch_shapes=[pltpu.VMEM((2, B_HEADS, STEPS, 2 * STEPS), F32),
                        pltpu.VMEM((B_HEADS // 2, tm, PAIR), F32), pltpu.VMEM((B_HEADS // 2, tm, PAIR), F32)],
        compiler_params=pltpu.CompilerParams(
            dimension_semantics=("arbitrary",), vmem_limit_bytes=VMEM_LIMIT),
        name=f"dilated_attn_d{dil}",
    )(pb, pb, pb, pb, pb)


def _merge_kernel(oa_ref, o0_ref, o1_ref, o2_ref, l0_ref, l1_ref, l2_ref, ga_ref, gb_ref, x_ref,
                  wpa_ref, wpb_ref, wo_ref, g_ref, b_ref, out_ref):
    l0, l1, l2 = l0_ref[...], l1_ref[...], l2_ref[...]
    mx = jnp.maximum(jnp.maximum(l0, l1), l2)
    e0, e1, e2 = jnp.exp(l0 - mx), jnp.exp(l1 - mx), jnp.exp(l2 - mx)
    ob = (e0 * o0_ref[...].astype(F32) + e1 * o1_ref[...].astype(F32) + e2 * o2_ref[...].astype(F32)) \
        / (e0 + e1 + e2)
    ya = jnp.dot(oa_ref[...], wpa_ref[...], preferred_element_type=F32)
    yb = jnp.dot(ob.astype(BF16), wpb_ref[...], preferred_element_type=F32)
    y = ga_ref[...].astype(F32) * ya + gb_ref[...].astype(F32) * yb
    z = ALPHA * x_ref[...] + jnp.dot(y.astype(BF16), wo_ref[...], preferred_element_type=F32)
    out_ref[...] = _layer_norm(z, g_ref[...], b_ref[...])


def _merge(oa, obs, lses, gates, x2, wpa, wpb, wo, g, b, tm):
    m = x2.shape[0]
    row = lambda w: pl.BlockSpec((tm, w), lambda i: (i, 0))
    full = lambda a: pl.BlockSpec(a.shape, lambda i: (0, 0))
    return pl.pallas_call(
        _merge_kernel,
        grid=(m // tm,),
        in_specs=[row(A_COLS), row(B_COLS), row(B_COLS), row(B_COLS), row(B_COLS), row(B_COLS), row(B_COLS),
                  pl.BlockSpec((tm, D_MODEL), lambda i: (i, 0)), pl.BlockSpec((tm, D_MODEL), lambda i: (i, 1)),
                  row(D_MODEL), full(wpa), full(wpb), full(wo), full(g), full(b)],
        out_specs=row(D_MODEL),
        out_shape=jax.ShapeDtypeStruct((m, D_MODEL), F32),
        compiler_params=pltpu.CompilerParams(dimension_semantics=("parallel",), vmem_limit_bytes=VMEM_LIMIT),
        name="merge_ln1",
    )(oa, *obs, *lses, gates, gates, x2, wpa, wpb, wo, g, b)


HALO = BF16_ROWS


def _ffn_kernel(x_ref, halo_ref, wup_ref, wconv_ref, bconv_ref, wd_ref, g_ref, b_ref,
                out_ref, xcat_ref, acc_ref, *, tiles_per_seq):
    i = pl.program_id(0)
    halo = jnp.where(i % tiles_per_seq == 0, 0.0, halo_ref[...])
    xcat_ref[0:HALO, :] = halo.astype(BF16)
    xcat_ref[HALO:, :] = x_ref[...].astype(BF16)
    xcat = xcat_ref[...]

    def conv(col0, width):
        cols = slice(col0, col0 + width)
        hfull = jnp.dot(xcat, wup_ref[:, cols], preferred_element_type=F32)
        out = bconv_ref[:, cols] + wconv_ref[CONV_WIDTH - 1:CONV_WIDTH, cols] * hfull[HALO:, :]
        for back in range(1, CONV_WIDTH):
            shifted = pltpu.roll(hfull, back, axis=0)[HALO:, :]
            out = out + wconv_ref[CONV_WIDTH - 1 - back:CONV_WIDTH - back, cols] * shifted
        return out

    for col0, width in FF_CHUNKS:
        a = conv(col0, width)
        gv = conv(D_FF + col0, width)
        f = 0.5 * a * (1.0 + lax.erf(a * (2.0 ** -0.5))) * gv
        y = jnp.dot(f.astype(BF16), wd_ref[col0:col0 + width, :], preferred_element_type=F32)
        if col0 == 0:
            acc_ref[...] = y
        else:
            acc_ref[...] += y

    z = ALPHA * x_ref[...] + acc_ref[...]
    out_ref[...] = _layer_norm(z, g_ref[...], b_ref[...])


def _ffn(x1, w_up, w_conv, b_conv, w_down, g, b, seq, tm):
    m = x1.shape[0]
    halo_blocks = tm // HALO
    kernel = functools.partial(_ffn_kernel, tiles_per_seq=seq // tm)
    full = lambda a: pl.BlockSpec(a.shape, lambda i: (0, 0))
    return pl.pallas_call(
        kernel,
        grid=(m // tm,),
        in_specs=[
            pl.BlockSpec((tm, D_MODEL), lambda i: (i, 0)),
            pl.BlockSpec((HALO, D_MODEL), lambda i: (jnp.maximum(i * halo_blocks - 1, 0), 0)),
            full(w_up), full(w_conv), full(b_conv), full(w_down), full(g), full(b),
        ],
        out_specs=pl.BlockSpec((tm, D_MODEL), lambda i: (i, 0)),
        out_shape=jax.ShapeDtypeStruct((m, D_MODEL), F32),
        scratch_shapes=[pltpu.VMEM((HALO + tm, D_MODEL), BF16), pltpu.VMEM((tm, D_MODEL), F32)],
        compiler_params=pltpu.CompilerParams(dimension_semantics=("parallel",), vmem_limit_bytes=VMEM_LIMIT),
        name="ffn_ln2",
    )(x1, x1, w_up, w_conv, b_conv, w_down, g, b)


def kernel(x, w_in, b_gate, lambda_q1, lambda_k1, lambda_q2, lambda_k2, subln_w, w_pa, w_pb, w_o, ln1_g, ln1_b,
           w_up, w_conv, b_conv, w_down, ln2_g, ln2_b):
    bsz, seq, d = x.shape
    assert (seq, d) == (4096, D_MODEL) and w_in.shape[0] == DEPTH
    slopes_a = jnp.asarray(_alibi_slopes(A_HEADS))
    for l in range(DEPTH):
        lambda_init = 0.8 - 0.6 * math.exp(-0.3 * l)
        x2 = x.reshape(bsz * seq, d)
        xb = x2.astype(BF16)
        wb = w_in[l].astype(BF16)
        proj_a = _projection(xb, wb[:, :3 * A_COLS], None, tm=PROJ_ROWS, tn=1024)
        gates = _projection(xb, wb[:, QKV_COLS:], b_gate[l][None, :], tm=PROJ_ROWS, tn=1024)
        oa = _diff_attention(proj_a.reshape(bsz, seq, 3 * A_COLS), slopes_a, lambda_q1[l][None],
                             lambda_k1[l][None], lambda_q2[l][None], lambda_k2[l][None], subln_w[l][:, None],
                             lambda_init, t=512)
        obs, lses = [], []
        for g, (window, dil) in enumerate(B_PATTERNS):
            assert window // dil == STEPS
            starts = [3 * A_COLS + (part * N_GROUPS + g) * B_COLS for part in range(3)]
            wg = jnp.concatenate([wb[:, s:s + B_COLS] for s in starts], axis=1)
            pb = _projection_by_residue(xb, wg, dil, tm=PROJ_ROWS, tn=B_COLS)
            o, lse = _dilated_attention(pb, bsz, seq, dil)
            obs.append(o)
            lses.append(lse)

        x1 = _merge(oa.reshape(bsz * seq, A_COLS), obs, lses, gates, x2,
                    w_pa[l].astype(BF16), w_pb[l].astype(BF16), w_o[l].astype(BF16),
                    ln1_g[l][None], ln1_b[l][None], tm=512)
        x2 = _ffn(x1, w_up[l].astype(BF16), w_conv[l], b_conv[l][None], w_down[l].astype(BF16),
                  ln2_g[l][None], ln2_b[l][None], seq, tm=512)
        x = x2.reshape(bsz, seq, d)
    return x
```

```python
import functools
import math

import numpy as np
import jax
import jax.numpy as jnp
from jax import lax
from jax.experimental import pallas as pl
from jax.experimental.pallas import tpu as pltpu

BF16 = jnp.bfloat16
F32 = jnp.float32

D_MODEL = 1024
A_HEADS = 8
A_HEAD_DIM = 64
B_PATTERNS = ((128, 1), (512, 4), (2048, 16))
B_HEADS = 8
B_HEAD_DIM = 64
D_FF = 2816
CONV_WIDTH = 3
EPS = 1e-5
DEPTH = 1
ALPHA = (2.0 * DEPTH) ** 0.25

A_COLS = A_HEADS * 2 * A_HEAD_DIM
B_COLS = B_HEADS * B_HEAD_DIM
N_GROUPS = len(B_PATTERNS)
QKV_COLS = 3 * A_COLS + 3 * N_GROUPS * B_COLS
PAIR = 2 * A_HEAD_DIM
STEPS = 128
BF16_ROWS = 16
LOG2E = math.log2(math.e)
PROJ_ROWS = 2048
MXU_WIDTH = 256
FF_CHUNKS = ((0, 6 * MXU_WIDTH), (6 * MXU_WIDTH, 5 * MXU_WIDTH))

VMEM_LIMIT = 56 * 1024 * 1024


def _alibi_slopes(n):
    return np.power(np.float32(2.0), -8.0 * (np.arange(n, dtype=np.float32) + 1) / n).astype(np.float32)


def _nt_dot(a, b):
    return lax.dot_general(a, b, (((1,), (1,)), ((), ())), preferred_element_type=F32)


def _layer_norm(z, g, b):
    mu = jnp.mean(z, axis=-1, keepdims=True)
    zc = z - mu
    var = jnp.mean(zc * zc, axis=-1, keepdims=True)
    return zc * lax.rsqrt(var + EPS) * g + b


PROJ_COLS = B_COLS
A_TILES = 3 * A_COLS // PROJ_COLS
GROUP_TILES = 3
GATE_TILES = 2 * D_MODEL // PROJ_COLS


def _proj_kernel(x_ref, w_ref, b_ref, pa_ref, *rest):
    group_refs, (g_ref, xb_ref, acc_ref) = rest[:N_GROUPS], rest[N_GROUPS:]
    j = pl.program_id(1)

    @pl.when(j == 0)
    def _():
        xb_ref[...] = x_ref[...].astype(BF16)

    def product():
        return jnp.dot(xb_ref[...], w_ref[...], preferred_element_type=F32)

    @pl.when(j < A_TILES)
    def _():
        pa_ref[...] = product().astype(pa_ref.dtype)

    for g, (o_ref, (_, dil)) in enumerate(zip(group_refs, B_PATTERNS)):
        first = A_TILES + g * GROUP_TILES

        @pl.when((j >= first) & (j < first + GROUP_TILES))
        def _(o_ref=o_ref, dil=dil):
            acc = product()
            if dil == 1:
                o_ref[0, 0] = acc.astype(o_ref.dtype)
                return
            groups, tm, _ = acc_ref.shape
            for c in range(groups):
                acc_ref[c] = acc[:, c * PAIR:(c + 1) * PAIR]
            for r in range(dil):
                for c in range(groups):
                    o_ref[0, r, :, c * PAIR:(c + 1) * PAIR] = \
                        acc_ref[c, pl.ds(r, tm // dil, stride=dil), :].astype(o_ref.dtype)

    @pl.when(j >= A_TILES + N_GROUPS * GROUP_TILES)
    def _():
        z = product() + b_ref[...]
        g_ref[...] = (1.0 / (1.0 + jnp.exp(-z))).astype(g_ref.dtype)


def _projection(x2, wb, b_gate, tm):
    m, k = x2.shape
    tn = PROJ_COLS
    group_first = [A_TILES + g * GROUP_TILES for g in range(N_GROUPS)]
    gate_first = A_TILES + N_GROUPS * GROUP_TILES
    clamp = lambda j, first, count: jnp.clip(j - first, 0, count - 1)
    group_specs = [
        pl.BlockSpec((1, dil, tm // dil, tn), lambda i, j, f=f: (i, 0, 0, clamp(j, f, GROUP_TILES)))
        for f, (_, dil) in zip(group_first, B_PATTERNS)]
    group_shapes = [jax.ShapeDtypeStruct((m // tm, dil, tm // dil, GROUP_TILES * tn), BF16)
                    for _, dil in B_PATTERNS]
    return pl.pallas_call(
        _proj_kernel,
        grid=(m // tm, wb.shape[1] // tn),
        in_specs=[pl.BlockSpec((tm, k), lambda i, j: (i, 0)),
                  pl.BlockSpec((k, tn), lambda i, j: (0, j)),
                  pl.BlockSpec((1, tn), lambda i, j: (0, clamp(j, gate_first, GATE_TILES)))],
        out_specs=[pl.BlockSpec((tm, tn), lambda i, j: (i, clamp(j, 0, A_TILES)))] + group_specs
        + [pl.BlockSpec((tm, tn), lambda i, j: (i, clamp(j, gate_first, GATE_TILES)))],
        out_shape=[jax.ShapeDtypeStruct((m, A_TILES * tn), BF16)] + group_shapes
        + [jax.ShapeDtypeStruct((m, GATE_TILES * tn), BF16)],
        scratch_shapes=[pltpu.VMEM((tm, k), BF16), pltpu.VMEM((tn // PAIR, tm, PAIR), F32)],
        compiler_params=pltpu.CompilerParams(
            dimension_semantics=("parallel", "arbitrary"), vmem_limit_bytes=VMEM_LIMIT),
        name="proj_in",
    )(x2, wb, b_gate)


def _diff_attn_kernel(slopes_ref, q_ref, k_ref, v_ref, lq1_ref, lk1_ref, lq2_ref, lk2_ref, subln_ref,
                      o_ref, vt_ref, bias_ref, sa_ref, sb_ref, ma_ref, mb_ref, acc_ref, *, t, lambda_init):
    slope = slopes_ref[pl.program_id(1)] * LOG2E
    nchunks = vt_ref.shape[0]

    def transpose_v(c, carry):
        r0 = pl.multiple_of(c * t, t)
        vt_ref[c, 0:PAIR, :] = v_ref[0, pl.ds(r0, t), :].astype(F32).T.astype(BF16)
        vt_ref[c, PAIR:, :] = jnp.ones((BF16_ROWS, t), BF16)
        return carry
    lax.fori_loop(0, nchunks, transpose_v, 0)
    krow = lax.broadcasted_iota(jnp.int32, (t, t), 0)
    qcol = lax.broadcasted_iota(jnp.int32, (t, t), 1)
    bias = slope * krow.astype(F32)
    bias_ref[0] = bias
    bias_ref[1] = jnp.where(krow <= qcol, bias, -jnp.inf)

    lam = (jnp.exp(jnp.sum(lq1_ref[...] * lk1_ref[...], axis=-1, keepdims=True))
           - jnp.exp(jnp.sum(lq2_ref[...] * lk2_ref[...], axis=-1, keepdims=True))
           + lambda_init)
    dim = lax.broadcasted_iota(jnp.int32, (PAIR, t), 0)
    neg = jnp.full((1, t), -jnp.inf, F32)
    half = t // 2
    s_refs = ((sa_ref, ma_ref), (sb_ref, mb_ref))

    for qi in range(nchunks):
        rows = slice(qi * t, (qi + 1) * t)
        acc = acc_ref.at[qi % 2]
        qt = q_ref[0, rows, :].astype(F32).T * (A_HEAD_DIM ** -0.5 * LOG2E)
        qt_maps = (jnp.where(dim < A_HEAD_DIM, qt, 0.0).astype(BF16),
                   jnp.where(dim < A_HEAD_DIM, 0.0, qt).astype(BF16))
        acc[...] = jnp.zeros(acc.shape, F32)

        def scores(j, buf, qt_maps=qt_maps):
            s_ref, max_ref = buf
            kc = k_ref[0, pl.ds(pl.multiple_of(j * t, t), t), :]
            for mp in range(2):
                s = jnp.dot(kc, qt_maps[mp], preferred_element_type=F32) + bias_ref[0]
                s_ref[mp] = s
                max_ref[mp] = jnp.max(s, axis=0, keepdims=True)

        def update(j, buf, stats, qi=qi, acc=acc):
            s_ref, max_ref = buf
            vt = vt_ref[j]
            shift = slope * ((j - qi) * t).astype(F32)
            out = []
            for mp in range(2):
                m_new = jnp.maximum(stats[mp], max_ref[mp] + shift)
                p = jnp.exp2(s_ref[mp] - (m_new - shift)).astype(BF16)
                alpha = jnp.exp2(stats[mp] - m_new)
                acc[mp] = alpha * acc[mp] + jnp.dot(vt, p, preferred_element_type=F32)
                out.append(m_new)
            return tuple(out)

        def scores_diag(buf, qi=qi, qt_maps=qt_maps):
            s_ref, max_ref = buf
            k_lo = k_ref[0, qi * t:qi * t + half, :]
            k_hi = k_ref[0, qi * t + half:(qi + 1) * t, :]
            for mp in range(2):
                s_lo = jnp.dot(k_lo, qt_maps[mp], preferred_element_type=F32) + bias_ref[1, 0:half, :]
                s_hi = jnp.dot(k_hi, qt_maps[mp][:, half:], preferred_element_type=F32) \
                    + bias_ref[1, half:, half:]
                s_ref[mp, 0:half, :] = s_lo
                s_ref[mp, half:, half:] = s_hi
                max_lo = jnp.max(s_lo, axis=0, keepdims=True)
                max_ref[mp, :, 0:half] = max_lo[:, :half]
                max_ref[mp, :, half:] = jnp.maximum(max_lo[:, half:], jnp.max(s_hi, axis=0, keepdims=True))

        def update_diag(buf, stats, qi=qi, acc=acc):
            s_ref, max_ref = buf
            for mp in range(2):
                m_new = jnp.maximum(stats[mp], max_ref[mp])
                p_lo = jnp.exp2(s_ref[mp, 0:half, :] - m_new).astype(BF16)
                p_hi = jnp.exp2(s_ref[mp, half:, half:] - m_new[:, half:]).astype(BF16)
                alpha = jnp.exp2(stats[mp] - m_new)
                pv = jnp.dot(vt_ref[qi, :, 0:half], p_lo, preferred_element_type=F32)
                pv_hi = jnp.dot(vt_ref[qi, :, half:], p_hi, preferred_element_type=F32)
                acc[mp, :, 0:half] = alpha[:, :half] * acc[mp, :, 0:half] + pv[:, :half]
                acc[mp, :, half:] = alpha[:, half:] * acc[mp, :, half:] + (pv[:, half:] + pv_hi)

        first, second = s_refs

        def pair(i, stats, scores=scores, update=update, first=first, second=second):
            j = 2 * i
            scores(j + 1, second)
            stats = update(j, first, stats)
            scores(j + 2, first)
            return update(j + 1, second, stats)

        idx = jnp.int32
        if qi == 0:
            scores_diag(first)
            update_diag(first, (neg, neg))
        else:
            scores(idx(0), first)
            stats = lax.fori_loop(0, (qi - 1) // 2, pair, (neg, neg))
            if qi % 2 == 1:
                scores_diag(second)
                update_diag(second, update(idx(qi - 1), first, stats))
            else:
                scores(idx(qi - 1), second)
                stats = update(idx(qi - 2), first, stats)
                scores_diag(first)
                update_diag(first, update(idx(qi - 1), second, stats))
        if qi % 2 == 0:
            s_refs = (second, first)

        a1, a2 = acc[0], acc[1]
        ot = a1[:PAIR] / a1[PAIR:PAIR + 1] - lam * (a2[:PAIR] / a2[PAIR:PAIR + 1])
        ot = ot * lax.rsqrt(jnp.mean(ot * ot, axis=0, keepdims=True) + EPS)
        ot = ot * subln_ref[...] * (1.0 - lambda_init)
        o_ref[0, rows, :] = ot.T.astype(o_ref.dtype)


def _diff_attention(p3, slopes, lq1, lk1, lq2, lk2, subln_col, lambda_init, t):
    bsz, seq, _ = p3.shape
    vec = lambda n: pl.BlockSpec((1, n), lambda b, h: (0, 0))
    kernel = functools.partial(_diff_attn_kernel, t=t, lambda_init=lambda_init)
    return pl.pallas_call(
        kernel,
        grid=(bsz, A_HEADS),
        in_specs=[
            pl.BlockSpec(memory_space=pltpu.SMEM),
            pl.BlockSpec((1, seq, PAIR), lambda b, h: (b, 0, h)),
            pl.BlockSpec((1, seq, PAIR), lambda b, h: (b, 0, A_HEADS + h)),
            pl.BlockSpec((1, seq, PAIR), lambda b, h: (b, 0, 2 * A_HEADS + h)),
            vec(A_HEAD_DIM), vec(A_HEAD_DIM), vec(A_HEAD_DIM), vec(A_HEAD_DIM),
            pl.BlockSpec((PAIR, 1), lambda b, h: (0, 0)),
        ],
        out_specs=pl.BlockSpec((1, seq, PAIR), lambda b, h: (b, 0, h)),
        out_shape=jax.ShapeDtypeStruct((bsz, seq, A_COLS), BF16),
        scratch_shapes=[pltpu.VMEM((seq // t, PAIR + BF16_ROWS, t), BF16), pltpu.VMEM((2, t, t), F32),
                        pltpu.VMEM((2, t, t), F32), pltpu.VMEM((2, t, t), F32),
                        pltpu.VMEM((2, 1, t), F32), pltpu.VMEM((2, 1, t), F32),
                        pltpu.VMEM((2, 2, PAIR + BF16_ROWS, t), F32)],
        compiler_params=pltpu.CompilerParams(
            dimension_semantics=("parallel", "parallel"), vmem_limit_bytes=VMEM_LIMIT),
        name="diff_attn",
    )(slopes, p3, p3, p3, lq1, lk1, lq2, lk2, subln_col)


def _dilated_kernel(q_ref, k_ref, v_ref, kprev_ref, vprev_ref, o_ref, lse_ref, bias_ref, o_scr, lse_scr, *,
                    dil, slopes, tiles_per_seq):
    i = pl.program_id(0)
    blocks = q_ref.shape[2] // STEPS

    @pl.when(i == 0)
    def _():
        qi = lax.broadcasted_iota(jnp.int32, (STEPS, 2 * STEPS), 0)
        kj = lax.broadcasted_iota(jnp.int32, (STEPS, 2 * STEPS), 1)
        step = qi + STEPS - kj
        window = (step >= 0) & (step <= STEPS)
        dist = (step * dil).astype(F32)
        for hd in range(B_HEADS):
            alibi = -float(slopes[hd]) * dist
            bias_ref[0, hd] = jnp.where(window & (kj >= STEPS), alibi, -jnp.inf)
            bias_ref[1, hd] = jnp.where(window, alibi, -jnp.inf)

    lane = lax.broadcasted_iota(jnp.int32, (STEPS, PAIR), 1)
    low = lane < B_HEAD_DIM
    ones = jnp.ones((2 * STEPS, PAIR), BF16)

    def unit(u, carry):
        r = u // blocks
        nl = u % blocks
        has_prev = jnp.logical_or(nl > 0, i % tiles_per_seq != 0)
        variant = has_prev.astype(jnp.int32)
        cur = pl.ds(pl.multiple_of(nl * STEPS, STEPS), STEPS)
        before = pl.ds(pl.multiple_of(jnp.maximum(nl - 1, 0) * STEPS, STEPS), STEPS)
        q = q_ref[0, r, cur, :].astype(F32) * (B_HEAD_DIM ** -0.5)
        k_before = jnp.where(nl > 0, k_ref[0, r, before, :], kprev_ref[0, r])
        v_before = jnp.where(nl > 0, v_ref[0, r, before, :], vprev_ref[0, r])
        kk = jnp.concatenate([k_before, k_ref[0, r, cur, :]], axis=0)
        vv = jnp.concatenate([v_before, v_ref[0, r, cur, :]], axis=0)
        start = nl * (STEPS * dil) + r
        rows = pl.ds(start, STEPS, stride=dil) if dil > 1 else pl.ds(pl.multiple_of(start, STEPS), STEPS)
        for pr in range(B_HEADS // 2):
            cols = slice(pr * PAIR, (pr + 1) * PAIR)
            qp, kp = q[:, cols], kk[:, cols]
            v_ones = jnp.concatenate([vv[:, cols], ones], axis=1)
            o_halves, lse_halves = [], []
            for par in range(2):
                qm = jnp.where(low, qp, 0.0) if par == 0 else jnp.where(low, 0.0, qp)
                s = _nt_dot(qm.astype(BF16), kp) + bias_ref[variant, 2 * pr + par]
                m = jnp.max(s, axis=-1, keepdims=True)
                e = jnp.exp(s - m).astype(BF16)
                o_den = jnp.dot(e, v_ones, preferred_element_type=F32)
                den = o_den[:, PAIR:]
                o_halves.append(o_den[:, :PAIR] / den)
                lse_halves.append(m + jnp.log(den))
            o_scr[pr, rows, :] = jnp.where(low, o_halves[0], o_halves[1])
            lse_scr[pr, rows, :] = jnp.where(low, lse_halves[0], lse_halves[1])
        return carry

    lax.fori_loop(0, dil * blocks, unit, 0, unroll=4)
    for pr in range(B_HEADS // 2):
        cols = slice(pr * PAIR, (pr + 1) * PAIR)
        o_ref[:, cols] = o_scr[pr].astype(o_ref.dtype)
        lse_ref[:, cols] = lse_scr[pr]


def _dilated_attention(pb, bsz, seq, dil):
    tiles, _, per_res, _ = pb.shape
    tm = dil * per_res
    tile_blk = (1, dil, per_res, B_COLS)
    prev_blk = (1, dil, STEPS, B_COLS)
    last = per_res // STEPS - 1
    prev = lambda i: jnp.maximum(i - 1, 0)
    kernel = functools.partial(_dilated_kernel, dil=dil, slopes=_alibi_slopes(B_HEADS),
                               tiles_per_seq=tiles // bsz)
    out_block = pl.BlockSpec((tm, B_COLS), lambda i: (i, 0))
    return pl.pallas_call(
        kernel,
        grid=(tiles,),
        in_specs=[
            pl.BlockSpec(tile_blk, lambda i: (i, 0, 0, 0)),
            pl.BlockSpec(tile_blk, lambda i: (i, 0, 0, 1)),
            pl.BlockSpec(tile_blk, lambda i: (i, 0, 0, 2)),
            pl.BlockSpec(prev_blk, lambda i: (prev(i), 0, last, 1)),
            pl.BlockSpec(prev_blk, lambda i: (prev(i), 0, last, 2)),
        ],
        out_specs=[out_block, out_block],
        out_shape=[jax.ShapeDtypeStruct((bsz * seq, B_COLS), BF16),
                   jax.ShapeDtypeStruct((bsz * seq, B_COLS), F32)],
        scratch_shapes=[pltpu.VMEM((2, B_HEADS, STEPS, 2 * STEPS), F32),
                        pltpu.VMEM((B_HEADS // 2, tm, PAIR), F32), pltpu.VMEM((B_HEADS // 2, tm, PAIR), F32)],
        compiler_params=pltpu.CompilerParams(
            dimension_semantics=("arbitrary",), vmem_limit_bytes=VMEM_LIMIT),
        name=f"dilated_attn_d{dil}",
    )(pb, pb, pb, pb, pb)


def _merge_kernel(oa_ref, o0_ref, o1_ref, o2_ref, l0_ref, l1_ref, l2_ref, ga_ref, gb_ref, x_ref,
                  wpa_ref, wpb_ref, wo_ref, g_ref, b_ref, out_ref):
    l0, l1, l2 = l0_ref[...], l1_ref[...], l2_ref[...]
    mx = jnp.maximum(jnp.maximum(l0, l1), l2)
    e0, e1, e2 = jnp.exp(l0 - mx), jnp.exp(l1 - mx), jnp.exp(l2 - mx)
    ob = (e0 * o0_ref[...].astype(F32) + e1 * o1_ref[...].astype(F32) + e2 * o2_ref[...].astype(F32)) \
        / (e0 + e1 + e2)
    ya = jnp.dot(oa_ref[...], wpa_ref[...], preferred_element_type=F32)
    yb = jnp.dot(ob.astype(BF16), wpb_ref[...], preferred_element_type=F32)
    y = ga_ref[...].astype(F32) * ya + gb_ref[...].astype(F32) * yb
    z = ALPHA * x_ref[...] + jnp.dot(y.astype(BF16), wo_ref[...], preferred_element_type=F32)
    out_ref[...] = _layer_norm(z, g_ref[...], b_ref[...])


def _merge(oa, obs, lses, gates, x2, wpa, wpb, wo, g, b, tm):
    m = x2.shape[0]
    row = lambda w: pl.BlockSpec((tm, w), lambda i: (i, 0))
    full = lambda a: pl.BlockSpec(a.shape, lambda i: (0, 0))
    return pl.pallas_call(
        _merge_kernel,
        grid=(m // tm,),
        in_specs=[row(A_COLS), row(B_COLS), row(B_COLS), row(B_COLS), row(B_COLS), row(B_COLS), row(B_COLS),
                  pl.BlockSpec((tm, D_MODEL), lambda i: (i, 0)), pl.BlockSpec((tm, D_MODEL), lambda i: (i, 1)),
                  row(D_MODEL), full(wpa), full(wpb), full(wo), full(g), full(b)],
        out_specs=row(D_MODEL),
        out_shape=jax.ShapeDtypeStruct((m, D_MODEL), F32),
        compiler_params=pltpu.CompilerParams(dimension_semantics=("parallel",), vmem_limit_bytes=VMEM_LIMIT),
        name="merge_ln1",
    )(oa, *obs, *lses, gates, gates, x2, wpa, wpb, wo, g, b)


HALO = BF16_ROWS


def _ffn_kernel(x_ref, halo_ref, wup_ref, wconv_ref, bconv_ref, wd_ref, g_ref, b_ref,
                out_ref, xcat_ref, acc_ref, *, tiles_per_seq):
    i = pl.program_id(0)
    halo = jnp.where(i % tiles_per_seq == 0, 0.0, halo_ref[...])
    xcat_ref[0:HALO, :] = halo.astype(BF16)
    xcat_ref[HALO:, :] = x_ref[...].astype(BF16)
    xcat = xcat_ref[...]

    def conv(col0, width):
        cols = slice(col0, col0 + width)
        hfull = jnp.dot(xcat, wup_ref[:, cols], preferred_element_type=F32)
        out = bconv_ref[:, cols] + wconv_ref[CONV_WIDTH - 1:CONV_WIDTH, cols] * hfull[HALO:, :]
        for back in range(1, CONV_WIDTH):
            shifted = pltpu.roll(hfull, back, axis=0)[HALO:, :]
            out = out + wconv_ref[CONV_WIDTH - 1 - back:CONV_WIDTH - back, cols] * shifted
        return out

    for col0, width in FF_CHUNKS:
        a = conv(col0, width)
        gv = conv(D_FF + col0, width)
        f = 0.5 * a * (1.0 + lax.erf(a * (2.0 ** -0.5))) * gv
        y = jnp.dot(f.astype(BF16), wd_ref[col0:col0 + width, :], preferred_element_type=F32)
        if col0 == 0:
            acc_ref[...] = y
        else:
            acc_ref[...] += y

    z = ALPHA * x_ref[...] + acc_ref[...]
    out_ref[...] = _layer_norm(z, g_ref[...], b_ref[...])


def _ffn(x1, w_up, w_conv, b_conv, w_down, g, b, seq, tm):
    m = x1.shape[0]
    halo_blocks = tm // HALO
    kernel = functools.partial(_ffn_kernel, tiles_per_seq=seq // tm)
    full = lambda a: pl.BlockSpec(a.shape, lambda i: (0, 0))
    return pl.pallas_call(
        kernel,
        grid=(m // tm,),
        in_specs=[
            pl.BlockSpec((tm, D_MODEL), lambda i: (i, 0)),
            pl.BlockSpec((HALO, D_MODEL), lambda i: (jnp.maximum(i * halo_blocks - 1, 0), 0)),
            full(w_up), full(w_conv), full(b_conv), full(w_down), full(g), full(b),
        ],
        out_specs=pl.BlockSpec((tm, D_MODEL), lambda i: (i, 0)),
        out_shape=jax.ShapeDtypeStruct((m, D_MODEL), F32),
        scratch_shapes=[pltpu.VMEM((HALO + tm, D_MODEL), BF16), pltpu.VMEM((tm, D_MODEL), F32)],
        compiler_params=pltpu.CompilerParams(dimension_semantics=("parallel",), vmem_limit_bytes=VMEM_LIMIT),
        name="ffn_ln2",
    )(x1, x1, w_up, w_conv, b_conv, w_down, g, b)


def kernel(x, w_in, b_gate, lambda_q1, lambda_k1, lambda_q2, lambda_k2, subln_w, w_pa, w_pb, w_o, ln1_g, ln1_b,
           w_up, w_conv, b_conv, w_down, ln2_g, ln2_b):
    bsz, seq, d = x.shape
    assert (seq, d) == (4096, D_MODEL) and w_in.shape[0] == DEPTH
    slopes_a = jnp.asarray(_alibi_slopes(A_HEADS))
    for l in range(DEPTH):
        lambda_init = 0.8 - 0.6 * math.exp(-0.3 * l)
        x2 = x.reshape(bsz * seq, d)
        group_cols = [3 * A_COLS + (part * N_GROUPS + g) * B_COLS for g in range(N_GROUPS) for part in range(3)]
        wb = jnp.concatenate([w_in[l][:, :3 * A_COLS]] + [w_in[l][:, s:s + B_COLS] for s in group_cols]
                             + [w_in[l][:, QKV_COLS:]], axis=1).astype(BF16)
        proj_a, *proj_groups, gates = _projection(x2, wb, b_gate[l][None, :], tm=PROJ_ROWS)
        oa = _diff_attention(proj_a.reshape(bsz, seq, 3 * A_COLS), slopes_a, lambda_q1[l][None],
                             lambda_k1[l][None], lambda_q2[l][None], lambda_k2[l][None], subln_w[l][:, None],
                             lambda_init, t=512)
        obs, lses = [], []
        for pb, (window, dil) in zip(proj_groups, B_PATTERNS):
            assert window // dil == STEPS
            o, lse = _dilated_attention(pb, bsz, seq, dil)
            obs.append(o)
            lses.append(lse)

        x1 = _merge(oa.reshape(bsz * seq, A_COLS), obs, lses, gates, x2,
                    w_pa[l].astype(BF16), w_pb[l].astype(BF16), w_o[l].astype(BF16),
                    ln1_g[l][None], ln1_b[l][None], tm=512)
        x2 = _ffn(x1, w_up[l].astype(BF16), w_conv[l], b_conv[l][None], w_down[l].astype(BF16),
                  ln2_g[l][None], ln2_b[l][None], seq, tm=512)
        x = x2.reshape(bsz, seq, d)
    return x
```

```python
import functools
import math

import numpy as np
import jax
import jax.numpy as jnp
from jax import lax
from jax.experimental import pallas as pl
from jax.experimental.pallas import tpu as pltpu

BF16 = jnp.bfloat16
F32 = jnp.float32

D_MODEL = 1024
A_HEADS = 8
A_HEAD_DIM = 64
B_PATTERNS = ((128, 1), (512, 4), (2048, 16))
B_HEADS = 8
B_HEAD_DIM = 64
D_FF = 2816
CONV_WIDTH = 3
EPS = 1e-5
DEPTH = 1
ALPHA = (2.0 * DEPTH) ** 0.25

A_COLS = A_HEADS * 2 * A_HEAD_DIM
B_COLS = B_HEADS * B_HEAD_DIM
N_GROUPS = len(B_PATTERNS)
QKV_COLS = 3 * A_COLS + 3 * N_GROUPS * B_COLS
PAIR = 2 * A_HEAD_DIM
STEPS = 128
BF16_ROWS = 16
LOG2E = math.log2(math.e)
PROJ_ROWS = 2048
MXU_WIDTH = 256
FF_CHUNKS = ((0, 6 * MXU_WIDTH), (6 * MXU_WIDTH, 5 * MXU_WIDTH))

VMEM_LIMIT = 56 * 1024 * 1024


def _alibi_slopes(n):
    return np.power(np.float32(2.0), -8.0 * (np.arange(n, dtype=np.float32) + 1) / n).astype(np.float32)


def _nt_dot(a, b):
    return lax.dot_general(a, b, (((1,), (1,)), ((), ())), preferred_element_type=F32)


def _layer_norm(z, g, b):
    mu = jnp.mean(z, axis=-1, keepdims=True)
    zc = z - mu
    var = jnp.mean(zc * zc, axis=-1, keepdims=True)
    return zc * lax.rsqrt(var + EPS) * g + b


WIDE_COLS = 1024
A_TILES = 3 * A_COLS // WIDE_COLS
GROUP_TILES = 3
SPLIT = 4


def _proj_wide_kernel(x_ref, w_ref, b_ref, pa_ref, g_ref, xb_ref):
    j = pl.program_id(1)

    @pl.when(j == 0)
    def _():
        xb_ref[...] = x_ref[...].astype(BF16)

    @pl.when(j < A_TILES)
    def _():
        pa_ref[...] = jnp.dot(xb_ref[...], w_ref[...], preferred_element_type=F32).astype(pa_ref.dtype)

    @pl.when(j >= A_TILES)
    def _():
        z = jnp.dot(xb_ref[...], w_ref[...], preferred_element_type=F32) + b_ref[...]
        g_ref[...] = (1.0 / (1.0 + jnp.exp(-z))).astype(g_ref.dtype)


def _projection_wide(x2, wb, b_gate, tm):
    m, k = x2.shape
    tn = WIDE_COLS
    gate_tiles = b_gate.shape[1] // tn
    clamp = lambda j, first, count: jnp.clip(j - first, 0, count - 1)
    return pl.pallas_call(
        _proj_wide_kernel,
        grid=(m // tm, A_TILES + gate_tiles),
        in_specs=[pl.BlockSpec((tm, k), lambda i, j: (i, 0)),
                  pl.BlockSpec((k, tn), lambda i, j: (0, j)),
                  pl.BlockSpec((1, tn), lambda i, j: (0, clamp(j, A_TILES, gate_tiles)))],
        out_specs=[pl.BlockSpec((tm, tn), lambda i, j: (i, clamp(j, 0, A_TILES))),
                   pl.BlockSpec((tm, tn), lambda i, j: (i, clamp(j, A_TILES, gate_tiles))),
                   pl.BlockSpec((tm, k), lambda i, j: (i, 0))],
        out_shape=[jax.ShapeDtypeStruct((m, A_TILES * tn), BF16),
                   jax.ShapeDtypeStruct((m, gate_tiles * tn), BF16),
                   jax.ShapeDtypeStruct((m, k), BF16)],
        compiler_params=pltpu.CompilerParams(
            dimension_semantics=("parallel", "arbitrary"), vmem_limit_bytes=VMEM_LIMIT),
        name="proj_wide",
    )(x2, wb, b_gate)


def _proj_groups_kernel(x_ref, w_ref, *rest):
    group_refs, (acc_ref, tmp_ref) = rest[:N_GROUPS], rest[N_GROUPS:]
    j = pl.program_id(1)
    lane_tiles, tm, _ = acc_ref.shape
    for g, (o_ref, (_, dil)) in enumerate(zip(group_refs, B_PATTERNS)):
        @pl.when((j >= g * GROUP_TILES) & (j < (g + 1) * GROUP_TILES))
        def _(o_ref=o_ref, dil=dil):
            acc = jnp.dot(x_ref[...], w_ref[...], preferred_element_type=F32)
            if dil == 1:
                o_ref[0, 0] = acc.astype(o_ref.dtype)
                return
            for c in range(lane_tiles):
                acc_ref[c] = acc[:, c * PAIR:(c + 1) * PAIR]
            for c in range(lane_tiles):
                cols = slice(c * PAIR, (c + 1) * PAIR)
                if dil == SPLIT:
                    for r in range(dil):
                        o_ref[0, r, :, cols] = acc_ref[c, pl.ds(r, tm // dil, stride=dil), :].astype(o_ref.dtype)
                else:
                    assert dil == SPLIT * SPLIT
                    part = tm // SPLIT
                    for r in range(SPLIT):
                        tmp_ref[c, r * part:(r + 1) * part, :] = acc_ref[c, pl.ds(r, part, stride=SPLIT), :]
                    for r in range(dil):
                        start = (r % SPLIT) * part + r // SPLIT
                        o_ref[0, r, :, cols] = \
                            tmp_ref[c, pl.ds(start, tm // dil, stride=SPLIT), :].astype(o_ref.dtype)


def _projection_groups(xb, wb, tm):
    m, k = xb.shape
    tn = B_COLS
    clamp = lambda j, first: jnp.clip(j - first, 0, GROUP_TILES - 1)
    return pl.pallas_call(
        _proj_groups_kernel,
        grid=(m // tm, N_GROUPS * GROUP_TILES),
        in_specs=[pl.BlockSpec((tm, k), lambda i, j: (i, 0)),
                  pl.BlockSpec((k, tn), lambda i, j: (0, j))],
        out_specs=[pl.BlockSpec((1, dil, tm // dil, tn), lambda i, j, g=g: (i, 0, 0, clamp(j, g * GROUP_TILES)))
                   for g, (_, dil) in enumerate(B_PATTERNS)],
        out_shape=[jax.ShapeDtypeStruct((m // tm, dil, tm // dil, GROUP_TILES * tn), BF16)
                   for _, dil in B_PATTERNS],
        scratch_shapes=[pltpu.VMEM((tn // PAIR, tm, PAIR), F32), pltpu.VMEM((tn // PAIR, tm, PAIR), F32)],
        compiler_params=pltpu.CompilerParams(
            dimension_semantics=("parallel", "arbitrary"), vmem_limit_bytes=VMEM_LIMIT),
        name="proj_groups",
    )(xb, wb)


def _diff_attn_kernel(slopes_ref, q_ref, k_ref, v_ref, lq1_ref, lk1_ref, lq2_ref, lk2_ref, subln_ref,
                      o_ref, vt_ref, bias_ref, sa_ref, sb_ref, ma_ref, mb_ref, acc_ref, *, t, lambda_init):
    slope = slopes_ref[pl.program_id(1)] * LOG2E
    nchunks = vt_ref.shape[0]

    def transpose_v(c, carry):
        r0 = pl.multiple_of(c * t, t)
        vt_ref[c, 0:PAIR, :] = v_ref[0, pl.ds(r0, t), :].astype(F32).T.astype(BF16)
        vt_ref[c, PAIR:, :] = jnp.ones((BF16_ROWS, t), BF16)
        return carry
    lax.fori_loop(0, nchunks, transpose_v, 0)
    krow = lax.broadcasted_iota(jnp.int32, (t, t), 0)
    qcol = lax.broadcasted_iota(jnp.int32, (t, t), 1)
    bias = slope * krow.astype(F32)
    bias_ref[0] = bias
    bias_ref[1] = jnp.where(krow <= qcol, bias, -jnp.inf)

    lam = (jnp.exp(jnp.sum(lq1_ref[...] * lk1_ref[...], axis=-1, keepdims=True))
           - jnp.exp(jnp.sum(lq2_ref[...] * lk2_ref[...], axis=-1, keepdims=True))
           + lambda_init)
    dim = lax.broadcasted_iota(jnp.int32, (PAIR, t), 0)
    neg = jnp.full((1, t), -jnp.inf, F32)
    half = t // 2
    s_refs = ((sa_ref, ma_ref), (sb_ref, mb_ref))

    for qi in range(nchunks):
        rows = slice(qi * t, (qi + 1) * t)
        acc = acc_ref.at[qi % 2]
        qt = q_ref[0, rows, :].astype(F32).T * (A_HEAD_DIM ** -0.5 * LOG2E)
        qt_maps = (jnp.where(dim < A_HEAD_DIM, qt, 0.0).astype(BF16),
                   jnp.where(dim < A_HEAD_DIM, 0.0, qt).astype(BF16))
        acc[...] = jnp.zeros(acc.shape, F32)

        def scores(j, buf, qt_maps=qt_maps):
            s_ref, max_ref = buf
            kc = k_ref[0, pl.ds(pl.multiple_of(j * t, t), t), :]
            for mp in range(2):
                s = jnp.dot(kc, qt_maps[mp], preferred_element_type=F32) + bias_ref[0]
                s_ref[mp] = s
                max_ref[mp] = jnp.max(s, axis=0, keepdims=True)

        def update(j, buf, stats, qi=qi, acc=acc):
            s_ref, max_ref = buf
            vt = vt_ref[j]
            shift = slope * ((j - qi) * t).astype(F32)
            out = []
            for mp in range(2):
                m_new = jnp.maximum(stats[mp], max_ref[mp] + shift)
                p = jnp.exp2(s_ref[mp] - (m_new - shift)).astype(BF16)
                alpha = jnp.exp2(stats[mp] - m_new)
                acc[mp] = alpha * acc[mp] + jnp.dot(vt, p, preferred_element_type=F32)
                out.append(m_new)
            return tuple(out)

        def scores_diag(buf, qi=qi, qt_maps=qt_maps):
            s_ref, max_ref = buf
            k_lo = k_ref[0, qi * t:qi * t + half, :]
            k_hi = k_ref[0, qi * t + half:(qi + 1) * t, :]
            for mp in range(2):
                s_lo = jnp.dot(k_lo, qt_maps[mp], preferred_element_type=F32) + bias_ref[1, 0:half, :]
                s_hi = jnp.dot(k_hi, qt_maps[mp][:, half:], preferred_element_type=F32) \
                    + bias_ref[1, half:, half:]
                s_ref[mp, 0:half, :] = s_lo
                s_ref[mp, half:, half:] = s_hi
                max_lo = jnp.max(s_lo, axis=0, keepdims=True)
                max_ref[mp, :, 0:half] = max_lo[:, :half]
                max_ref[mp, :, half:] = jnp.maximum(max_lo[:, half:], jnp.max(s_hi, axis=0, keepdims=True))

        def update_diag(buf, stats, qi=qi, acc=acc):
            s_ref, max_ref = buf
            for mp in range(2):
                m_new = jnp.maximum(stats[mp], max_ref[mp])
                p_lo = jnp.exp2(s_ref[mp, 0:half, :] - m_new).astype(BF16)
                p_hi = jnp.exp2(s_ref[mp, half:, half:] - m_new[:, half:]).astype(BF16)
                alpha = jnp.exp2(stats[mp] - m_new)
                pv = jnp.dot(vt_ref[qi, :, 0:half], p_lo, preferred_element_type=F32)
                pv_hi = jnp.dot(vt_ref[qi, :, half:], p_hi, preferred_element_type=F32)
                acc[mp, :, 0:half] = alpha[:, :half] * acc[mp, :, 0:half] + pv[:, :half]
                acc[mp, :, half:] = alpha[:, half:] * acc[mp, :, half:] + (pv[:, half:] + pv_hi)

        first, second = s_refs

        def pair(i, stats, scores=scores, update=update, first=first, second=second):
            j = 2 * i
            scores(j + 1, second)
            stats = update(j, first, stats)
            scores(j + 2, first)
            return update(j + 1, second, stats)

        idx = jnp.int32
        if qi == 0:
            scores_diag(first)
            update_diag(first, (neg, neg))
        else:
            scores(idx(0), first)
            stats = lax.fori_loop(0, (qi - 1) // 2, pair, (neg, neg))
            if qi % 2 == 1:
                scores_diag(second)
                update_diag(second, update(idx(qi - 1), first, stats))
            else:
                scores(idx(qi - 1), second)
                stats = update(idx(qi - 2), first, stats)
                scores_diag(first)
                update_diag(first, update(idx(qi - 1), second, stats))
        if qi % 2 == 0:
            s_refs = (second, first)

        a1, a2 = acc[0], acc[1]
        ot = a1[:PAIR] / a1[PAIR:PAIR + 1] - lam * (a2[:PAIR] / a2[PAIR:PAIR + 1])
        ot = ot * lax.rsqrt(jnp.mean(ot * ot, axis=0, keepdims=True) + EPS)
        ot = ot * subln_ref[...] * (1.0 - lambda_init)
        o_ref[0, rows, :] = ot.T.astype(o_ref.dtype)


def _diff_attention(p3, slopes, lq1, lk1, lq2, lk2, subln_col, lambda_init, t):
    bsz, seq, _ = p3.shape
    vec = lambda n: pl.BlockSpec((1, n), lambda b, h: (0, 0))
    kernel = functools.partial(_diff_attn_kernel, t=t, lambda_init=lambda_init)
    return pl.pallas_call(
        kernel,
        grid=(bsz, A_HEADS),
        in_specs=[
            pl.BlockSpec(memory_space=pltpu.SMEM),
            pl.BlockSpec((1, seq, PAIR), lambda b, h: (b, 0, h)),
            pl.BlockSpec((1, seq, PAIR), lambda b, h: (b, 0, A_HEADS + h)),
            pl.BlockSpec((1, seq, PAIR), lambda b, h: (b, 0, 2 * A_HEADS + h)),
            vec(A_HEAD_DIM), vec(A_HEAD_DIM), vec(A_HEAD_DIM), vec(A_HEAD_DIM),
            pl.BlockSpec((PAIR, 1), lambda b, h: (0, 0)),
        ],
        out_specs=pl.BlockSpec((1, seq, PAIR), lambda b, h: (b, 0, h)),
        out_shape=jax.ShapeDtypeStruct((bsz, seq, A_COLS), BF16),
        scratch_shapes=[pltpu.VMEM((seq // t, PAIR + BF16_ROWS, t), BF16), pltpu.VMEM((2, t, t), F32),
                        pltpu.VMEM((2, t, t), F32), pltpu.VMEM((2, t, t), F32),
                        pltpu.VMEM((2, 1, t), F32), pltpu.VMEM((2, 1, t), F32),
                        pltpu.VMEM((2, 2, PAIR + BF16_ROWS, t), F32)],
        compiler_params=pltpu.CompilerParams(
            dimension_semantics=("parallel", "parallel"), vmem_limit_bytes=VMEM_LIMIT),
        name="diff_attn",
    )(slopes, p3, p3, p3, lq1, lk1, lq2, lk2, subln_col)


def _dilated_kernel(q_ref, k_ref, v_ref, kprev_ref, vprev_ref, o_ref, lse_ref, bias_ref, o_scr, lse_scr, *,
                    dil, slopes, tiles_per_seq):
    i = pl.program_id(0)
    blocks = q_ref.shape[2] // STEPS

    @pl.when(i == 0)
    def _():
        qi = lax.broadcasted_iota(jnp.int32, (STEPS, 2 * STEPS), 0)
        kj = lax.broadcasted_iota(jnp.int32, (STEPS, 2 * STEPS), 1)
        step = qi + STEPS - kj
        window = (step >= 0) & (step <= STEPS)
        dist = (step * dil).astype(F32)
        for hd in range(B_HEADS):
            alibi = -float(slopes[hd]) * dist
            bias_ref[0, hd] = jnp.where(window & (kj >= STEPS), alibi, -jnp.inf)
            bias_ref[1, hd] = jnp.where(window, alibi, -jnp.inf)

    lane = lax.broadcasted_iota(jnp.int32, (STEPS, PAIR), 1)
    low = lane < B_HEAD_DIM
    ones = jnp.ones((2 * STEPS, PAIR), BF16)

    def unit(u, carry):
        r = u // blocks
        nl = u % blocks
        has_prev = jnp.logical_or(nl > 0, i % tiles_per_seq != 0)
        variant = has_prev.astype(jnp.int32)
        cur = pl.ds(pl.multiple_of(nl * STEPS, STEPS), STEPS)
        before = pl.ds(pl.multiple_of(jnp.maximum(nl - 1, 0) * STEPS, STEPS), STEPS)
        q = q_ref[0, r, cur, :].astype(F32) * (B_HEAD_DIM ** -0.5)
        k_before = jnp.where(nl > 0, k_ref[0, r, before, :], kprev_ref[0, r])
        v_before = jnp.where(nl > 0, v_ref[0, r, before, :], vprev_ref[0, r])
        kk = jnp.concatenate([k_before, k_ref[0, r, cur, :]], axis=0)
        vv = jnp.concatenate([v_before, v_ref[0, r, cur, :]], axis=0)
        start = nl * (STEPS * dil) + r
        rows = pl.ds(start, STEPS, stride=dil) if dil > 1 else pl.ds(pl.multiple_of(start, STEPS), STEPS)
        for pr in range(B_HEADS // 2):
            cols = slice(pr * PAIR, (pr + 1) * PAIR)
            qp, kp = q[:, cols], kk[:, cols]
            v_ones = jnp.concatenate([vv[:, cols], ones], axis=1)
            o_halves, lse_halves = [], []
            for par in range(2):
                qm = jnp.where(low, qp, 0.0) if par == 0 else jnp.where(low, 0.0, qp)
                s = _nt_dot(qm.astype(BF16), kp) + bias_ref[variant, 2 * pr + par]
                m = jnp.max(s, axis=-1, keepdims=True)
                e = jnp.exp(s - m).astype(BF16)
                o_den = jnp.dot(e, v_ones, preferred_element_type=F32)
                den = o_den[:, PAIR:]
                o_halves.append(o_den[:, :PAIR] / den)
                lse_halves.append(m + jnp.log(den))
            o_scr[pr, rows, :] = jnp.where(low, o_halves[0], o_halves[1])
            lse_scr[pr, rows, :] = jnp.where(low, lse_halves[0], lse_halves[1])
        return carry

    lax.fori_loop(0, dil * blocks, unit, 0, unroll=4)
    for pr in range(B_HEADS // 2):
        cols = slice(pr * PAIR, (pr + 1) * PAIR)
        o_ref[:, cols] = o_scr[pr].astype(o_ref.dtype)
        lse_ref[:, cols] = lse_scr[pr]


def _dilated_attention(pb, bsz, seq, dil):
    tiles, _, per_res, _ = pb.shape
    tm = dil * per_res
    tile_blk = (1, dil, per_res, B_COLS)
    prev_blk = (1, dil, STEPS, B_COLS)
    last = per_res // STEPS - 1
    prev = lambda i: jnp.maximum(i - 1, 0)
    kernel = functools.partial(_dilated_kernel, dil=dil, slopes=_alibi_slopes(B_HEADS),
                               tiles_per_seq=tiles // bsz)
    out_block = pl.BlockSpec((tm, B_COLS), lambda i: (i, 0))
    return pl.pallas_call(
        kernel,
        grid=(tiles,),
        in_specs=[
            pl.BlockSpec(tile_blk, lambda i: (i, 0, 0, 0)),
            pl.BlockSpec(tile_blk, lambda i: (i, 0, 0, 1)),
            pl.BlockSpec(tile_blk, lambda i: (i, 0, 0, 2)),
            pl.BlockSpec(prev_blk, lambda i: (prev(i), 0, last, 1)),
            pl.BlockSpec(prev_blk, lambda i: (prev(i), 0, last, 2)),
        ],
        out_specs=[out_block, out_block],
        out_shape=[jax.ShapeDtypeStruct((bsz * seq, B_COLS), BF16),
                   jax.ShapeDtypeStruct((bsz * seq, B_COLS), F32)],
        scratch_shapes=[pltpu.VMEM((2, B_HEADS, STEPS, 2 * STEPS), F32),
                        pltpu.VMEM((B_HEADS // 2, tm, PAIR), F32), pltpu.VMEM((B_HEADS // 2, tm, PAIR), F32)],
        compiler_params=pltpu.CompilerParams(
            dimension_semantics=("arbitrary",), vmem_limit_bytes=VMEM_LIMIT),
        name=f"dilated_attn_d{dil}",
    )(pb, pb, pb, pb, pb)


def _merge_kernel(oa_ref, o0_ref, o1_ref, o2_ref, l0_ref, l1_ref, l2_ref, ga_ref, gb_ref, x_ref,
                  wpa_ref, wpb_ref, wo_ref, g_ref, b_ref, out_ref):
    l0, l1, l2 = l0_ref[...], l1_ref[...], l2_ref[...]
    mx = jnp.maximum(jnp.maximum(l0, l1), l2)
    e0, e1, e2 = jnp.exp(l0 - mx), jnp.exp(l1 - mx), jnp.exp(l2 - mx)
    ob = (e0 * o0_ref[...].astype(F32) + e1 * o1_ref[...].astype(F32) + e2 * o2_ref[...].astype(F32)) \
        / (e0 + e1 + e2)
    ya = jnp.dot(oa_ref[...], wpa_ref[...], preferred_element_type=F32)
    yb = jnp.dot(ob.astype(BF16), wpb_ref[...], preferred_element_type=F32)
    y = ga_ref[...].astype(F32) * ya + gb_ref[...].astype(F32) * yb
    z = ALPHA * x_ref[...] + jnp.dot(y.astype(BF16), wo_ref[...], preferred_element_type=F32)
    out_ref[...] = _layer_norm(z, g_ref[...], b_ref[...])


def _merge(oa, obs, lses, gates, x2, wpa, wpb, wo, g, b, tm):
    m = x2.shape[0]
    row = lambda w: pl.BlockSpec((tm, w), lambda i: (i, 0))
    full = lambda a: pl.BlockSpec(a.shape, lambda i: (0, 0))
    return pl.pallas_call(
        _merge_kernel,
        grid=(m // tm,),
        in_specs=[row(A_COLS), row(B_COLS), row(B_COLS), row(B_COLS), row(B_COLS), row(B_COLS), row(B_COLS),
                  pl.BlockSpec((tm, D_MODEL), lambda i: (i, 0)), pl.BlockSpec((tm, D_MODEL), lambda i: (i, 1)),
                  row(D_MODEL), full(wpa), full(wpb), full(wo), full(g), full(b)],
        out_specs=row(D_MODEL),
        out_shape=jax.ShapeDtypeStruct((m, D_MODEL), F32),
        compiler_params=pltpu.CompilerParams(dimension_semantics=("parallel",), vmem_limit_bytes=VMEM_LIMIT),
        name="merge_ln1",
    )(oa, *obs, *lses, gates, gates, x2, wpa, wpb, wo, g, b)


HALO = BF16_ROWS


def _ffn_kernel(x_ref, halo_ref, wup_ref, wconv_ref, bconv_ref, wd_ref, g_ref, b_ref,
                out_ref, xcat_ref, acc_ref, *, tiles_per_seq):
    i = pl.program_id(0)
    halo = jnp.where(i % tiles_per_seq == 0, 0.0, halo_ref[...])
    xcat_ref[0:HALO, :] = halo.astype(BF16)
    xcat_ref[HALO:, :] = x_ref[...].astype(BF16)
    xcat = xcat_ref[...]

    def conv(col0, width):
        cols = slice(col0, col0 + width)
        hfull = jnp.dot(xcat, wup_ref[:, cols], preferred_element_type=F32)
        out = bconv_ref[:, cols] + wconv_ref[CONV_WIDTH - 1:CONV_WIDTH, cols] * hfull[HALO:, :]
        for back in range(1, CONV_WIDTH):
            shifted = pltpu.roll(hfull, back, axis=0)[HALO:, :]
            out = out + wconv_ref[CONV_WIDTH - 1 - back:CONV_WIDTH - back, cols] * shifted
        return out

    for col0, width in FF_CHUNKS:
        a = conv(col0, width)
        gv = conv(D_FF + col0, width)
        f = 0.5 * a * (1.0 + lax.erf(a * (2.0 ** -0.5))) * gv
        y = jnp.dot(f.astype(BF16), wd_ref[col0:col0 + width, :], preferred_element_type=F32)
        if col0 == 0:
            acc_ref[...] = y
        else:
            acc_ref[...] += y

    z = ALPHA * x_ref[...] + acc_ref[...]
    out_ref[...] = _layer_norm(z, g_ref[...], b_ref[...])


def _ffn(x1, w_up, w_conv, b_conv, w_down, g, b, seq, tm):
    m = x1.shape[0]
    halo_blocks = tm // HALO
    kernel = functools.partial(_ffn_kernel, tiles_per_seq=seq // tm)
    full = lambda a: pl.BlockSpec(a.shape, lambda i: (0, 0))
    return pl.pallas_call(
        kernel,
        grid=(m // tm,),
        in_specs=[
            pl.BlockSpec((tm, D_MODEL), lambda i: (i, 0)),
            pl.BlockSpec((HALO, D_MODEL), lambda i: (jnp.maximum(i * halo_blocks - 1, 0), 0)),
            full(w_up), full(w_conv), full(b_conv), full(w_down), full(g), full(b),
        ],
        out_specs=pl.BlockSpec((tm, D_MODEL), lambda i: (i, 0)),
        out_shape=jax.ShapeDtypeStruct((m, D_MODEL), F32),
        scratch_shapes=[pltpu.VMEM((HALO + tm, D_MODEL), BF16), pltpu.VMEM((tm, D_MODEL), F32)],
        compiler_params=pltpu.CompilerParams(dimension_semantics=("parallel",), vmem_limit_bytes=VMEM_LIMIT),
        name="ffn_ln2",
    )(x1, x1, w_up, w_conv, b_conv, w_down, g, b)


def kernel(x, w_in, b_gate, lambda_q1, lambda_k1, lambda_q2, lambda_k2, subln_w, w_pa, w_pb, w_o, ln1_g, ln1_b,
           w_up, w_conv, b_conv, w_down, ln2_g, ln2_b):
    bsz, seq, d = x.shape
    assert (seq, d) == (4096, D_MODEL) and w_in.shape[0] == DEPTH
    slopes_a = jnp.asarray(_alibi_slopes(A_HEADS))
    for l in range(DEPTH):
        lambda_init = 0.8 - 0.6 * math.exp(-0.3 * l)
        x2 = x.reshape(bsz * seq, d)
        w_wide = jnp.concatenate([w_in[l][:, :3 * A_COLS], w_in[l][:, QKV_COLS:]], axis=1).astype(BF16)
        proj_a, gates, xb = _projection_wide(x2, w_wide, b_gate[l][None, :], tm=PROJ_ROWS)
        group_cols = [3 * A_COLS + (part * N_GROUPS + g) * B_COLS for g in range(N_GROUPS) for part in range(3)]
        w_groups = jnp.concatenate([w_in[l][:, s:s + B_COLS] for s in group_cols], axis=1).astype(BF16)
        proj_groups = _projection_groups(xb, w_groups, tm=PROJ_ROWS)
        oa = _diff_attention(proj_a.reshape(bsz, seq, 3 * A_COLS), slopes_a, lambda_q1[l][None],
                             lambda_k1[l][None], lambda_q2[l][None], lambda_k2[l][None], subln_w[l][:, None],
                             lambda_init, t=512)
        obs, lses = [], []
        for pb, (window, dil) in zip(proj_groups, B_PATTERNS):
            assert window // dil == STEPS
            o, lse = _dilated_attention(pb, bsz, seq, dil)
            obs.append(o)
            lses.append(lse)

        x1 = _merge(oa.reshape(bsz * seq, A_COLS), obs, lses, gates, x2,
                    w_pa[l].astype(BF16), w_pb[l].astype(BF16), w_o[l].astype(BF16),
                    ln1_g[l][None], ln1_b[l][None], tm=512)
        x2 = _ffn(x1, w_up[l].astype(BF16), w_conv[l], b_conv[l][None], w_down[l].astype(BF16),
                  ln2_g[l][None], ln2_b[l][None], seq, tm=512)
        x = x2.reshape(bsz, seq, d)
    return x
```

```python
import functools
import math

import numpy as np
import jax
import jax.numpy as jnp
from jax import lax
from jax.experimental import pallas as pl
from jax.experimental.pallas import tpu as pltpu

BF16 = jnp.bfloat16
F32 = jnp.float32

D_MODEL = 1024
A_HEADS = 8
A_HEAD_DIM = 64
B_PATTERNS = ((128, 1), (512, 4), (2048, 16))
B_HEADS = 8
B_HEAD_DIM = 64
D_FF = 2816
CONV_WIDTH = 3
EPS = 1e-5
DEPTH = 1
ALPHA = (2.0 * DEPTH) ** 0.25

A_COLS = A_HEADS * 2 * A_HEAD_DIM
B_COLS = B_HEADS * B_HEAD_DIM
N_GROUPS = len(B_PATTERNS)
QKV_COLS = 3 * A_COLS + 3 * N_GROUPS * B_COLS
PAIR = 2 * A_HEAD_DIM
STEPS = 128
BF16_ROWS = 16
LOG2E = math.log2(math.e)
BF16_EXACT = 256
SLOPE_TERMS = 3
PROJ_ROWS = 2048
MXU_WIDTH = 256
FF_CHUNKS = ((0, 6 * MXU_WIDTH), (6 * MXU_WIDTH, 5 * MXU_WIDTH))

VMEM_LIMIT = 56 * 1024 * 1024


def _alibi_slopes(n):
    return np.power(np.float32(2.0), -8.0 * (np.arange(n, dtype=np.float32) + 1) / n).astype(np.float32)


def _nt_dot(a, b):
    return lax.dot_general(a, b, (((1,), (1,)), ((), ())), preferred_element_type=F32)


def _layer_norm(z, g, b):
    mu = jnp.mean(z, axis=-1, keepdims=True)
    zc = z - mu
    var = jnp.mean(zc * zc, axis=-1, keepdims=True)
    return zc * lax.rsqrt(var + EPS) * g + b


WIDE_COLS = 1024
A_TILES = 3 * A_COLS // WIDE_COLS
GROUP_TILES = 3
SPLIT = 4


def _proj_wide_kernel(x_ref, w_ref, b_ref, pa_ref, g_ref, xb_ref):
    j = pl.program_id(1)

    @pl.when(j == 0)
    def _():
        xb_ref[...] = x_ref[...].astype(BF16)

    @pl.when(j < A_TILES)
    def _():
        pa_ref[...] = jnp.dot(xb_ref[...], w_ref[...], preferred_element_type=F32).astype(pa_ref.dtype)

    @pl.when(j >= A_TILES)
    def _():
        z = jnp.dot(xb_ref[...], w_ref[...], preferred_element_type=F32) + b_ref[...]
        g_ref[...] = (0.5 * jnp.tanh(0.5 * z) + 0.5).astype(g_ref.dtype)


def _projection_wide(x2, wb, b_gate, tm):
    m, k = x2.shape
    tn = WIDE_COLS
    gate_tiles = b_gate.shape[1] // tn
    clamp = lambda j, first, count: jnp.clip(j - first, 0, count - 1)
    return pl.pallas_call(
        _proj_wide_kernel,
        grid=(m // tm, A_TILES + gate_tiles),
        in_specs=[pl.BlockSpec((tm, k), lambda i, j: (i, 0)),
                  pl.BlockSpec((k, tn), lambda i, j: (0, j)),
                  pl.BlockSpec((1, tn), lambda i, j: (0, clamp(j, A_TILES, gate_tiles)))],
        out_specs=[pl.BlockSpec((tm, tn), lambda i, j: (i, clamp(j, 0, A_TILES))),
                   pl.BlockSpec((tm, tn), lambda i, j: (i, clamp(j, A_TILES, gate_tiles))),
                   pl.BlockSpec((tm, k), lambda i, j: (i, 0))],
        out_shape=[jax.ShapeDtypeStruct((m, A_TILES * tn), BF16),
                   jax.ShapeDtypeStruct((m, gate_tiles * tn), BF16),
                   jax.ShapeDtypeStruct((m, k), BF16)],
        compiler_params=pltpu.CompilerParams(
            dimension_semantics=("parallel", "arbitrary"), vmem_limit_bytes=VMEM_LIMIT),
        name="proj_wide",
    )(x2, wb, b_gate)


def _proj_groups_kernel(x_ref, w_ref, *rest):
    group_refs, (acc_ref, tmp_ref) = rest[:N_GROUPS], rest[N_GROUPS:]
    j = pl.program_id(1)
    lane_tiles, tm, _ = acc_ref.shape
    for g, (o_ref, (_, dil)) in enumerate(zip(group_refs, B_PATTERNS)):
        @pl.when((j >= g * GROUP_TILES) & (j < (g + 1) * GROUP_TILES))
        def _(o_ref=o_ref, dil=dil):
            acc = jnp.dot(x_ref[...], w_ref[...], preferred_element_type=F32)
            if dil == 1:
                o_ref[0, 0] = acc.astype(o_ref.dtype)
                return
            for c in range(lane_tiles):
                acc_ref[c] = acc[:, c * PAIR:(c + 1) * PAIR]
            for c in range(lane_tiles):
                cols = slice(c * PAIR, (c + 1) * PAIR)
                if dil == SPLIT:
                    for r in range(dil):
                        o_ref[0, r, :, cols] = acc_ref[c, pl.ds(r, tm // dil, stride=dil), :].astype(o_ref.dtype)
                else:
                    assert dil == SPLIT * SPLIT
                    part = tm // SPLIT
                    for r in range(SPLIT):
                        tmp_ref[c, r * part:(r + 1) * part, :] = acc_ref[c, pl.ds(r, part, stride=SPLIT), :]
                    for r in range(dil):
                        start = (r % SPLIT) * part + r // SPLIT
                        o_ref[0, r, :, cols] = \
                            tmp_ref[c, pl.ds(start, tm // dil, stride=SPLIT), :].astype(o_ref.dtype)


def _projection_groups(xb, wb, tm):
    m, k = xb.shape
    tn = B_COLS
    clamp = lambda j, first: jnp.clip(j - first, 0, GROUP_TILES - 1)
    return pl.pallas_call(
        _proj_groups_kernel,
        grid=(m // tm, N_GROUPS * GROUP_TILES),
        in_specs=[pl.BlockSpec((tm, k), lambda i, j: (i, 0)),
                  pl.BlockSpec((k, tn), lambda i, j: (0, j))],
        out_specs=[pl.BlockSpec((1, dil, tm // dil, tn), lambda i, j, g=g: (i, 0, 0, clamp(j, g * GROUP_TILES)))
                   for g, (_, dil) in enumerate(B_PATTERNS)],
        out_shape=[jax.ShapeDtypeStruct((m // tm, dil, tm // dil, GROUP_TILES * tn), BF16)
                   for _, dil in B_PATTERNS],
        scratch_shapes=[pltpu.VMEM((tn // PAIR, tm, PAIR), F32), pltpu.VMEM((tn // PAIR, tm, PAIR), F32)],
        compiler_params=pltpu.CompilerParams(
            dimension_semantics=("parallel", "arbitrary"), vmem_limit_bytes=VMEM_LIMIT),
        name="proj_groups",
    )(xb, wb)


def _diff_attn_kernel(slopes_ref, q_ref, k_ref, v_ref, lq1_ref, lk1_ref, lq2_ref, lk2_ref, subln_ref,
                      o_ref, vt_ref, mask_ref, kaug_ref, sa_ref, sb_ref, ma_ref, mb_ref, acc_ref, *,
                      t, lambda_init):
    slope = slopes_ref[pl.program_id(1)] * LOG2E
    nchunks = vt_ref.shape[0]

    def transpose_v(c, carry):
        r0 = pl.multiple_of(c * t, t)
        vt_ref[c, 0:PAIR, :] = v_ref[0, pl.ds(r0, t), :].astype(F32).T.astype(BF16)
        vt_ref[c, PAIR:, :] = jnp.ones((BF16_ROWS, t), BF16)
        return carry
    lax.fori_loop(0, nchunks, transpose_v, 0)
    krow = lax.broadcasted_iota(jnp.int32, (t, t), 0)
    qcol = lax.broadcasted_iota(jnp.int32, (t, t), 1)
    mask_ref[...] = jnp.where(krow <= qcol, 0.0, -jnp.inf)

    r = lax.broadcasted_iota(jnp.int32, (t, PAIR), 0)
    c = lax.broadcasted_iota(jnp.int32, (t, PAIR), 1)
    r_low = (r % BF16_EXACT).astype(F32)
    r_high = (r - r % BF16_EXACT).astype(F32)
    kaug_ref[...] = jnp.where(c < SLOPE_TERMS, r_low, jnp.where(c < 2 * SLOPE_TERMS, r_high, 0.0)).astype(BF16)
    rest = jnp.full((PAIR, t), slope, F32)
    wrow = lax.broadcasted_iota(jnp.int32, (PAIR, t), 0)
    waug = jnp.zeros((PAIR, t), F32)
    for term in range(SLOPE_TERMS):
        part = rest.astype(BF16).astype(F32)
        waug = jnp.where((wrow == term) | (wrow == SLOPE_TERMS + term), part, waug)
        rest = rest - part
    waug = waug.astype(BF16)

    lam = (jnp.exp(jnp.sum(lq1_ref[...] * lk1_ref[...], axis=-1, keepdims=True))
           - jnp.exp(jnp.sum(lq2_ref[...] * lk2_ref[...], axis=-1, keepdims=True))
           + lambda_init)
    dim = lax.broadcasted_iota(jnp.int32, (PAIR, t), 0)
    neg = jnp.full((1, t), -jnp.inf, F32)
    half = t // 2
    s_refs = ((sa_ref, ma_ref), (sb_ref, mb_ref))

    for qi in range(nchunks):
        rows = slice(qi * t, (qi + 1) * t)
        acc = acc_ref.at[qi % 2]
        qt = q_ref[0, rows, :].astype(F32).T * (A_HEAD_DIM ** -0.5 * LOG2E)
        qt_maps = (jnp.concatenate([jnp.where(dim < A_HEAD_DIM, qt, 0.0).astype(BF16), waug], axis=0),
                   jnp.concatenate([jnp.where(dim < A_HEAD_DIM, 0.0, qt).astype(BF16), waug], axis=0))
        acc[...] = jnp.zeros(acc.shape, F32)

        def scores(j, buf, qt_maps=qt_maps):
            s_ref, max_ref = buf
            kc = k_ref[0, pl.ds(pl.multiple_of(j * t, t), t), :]
            kc = jnp.concatenate([kc, kaug_ref[...]], axis=1)
            for mp in range(2):
                s = jnp.dot(kc, qt_maps[mp], preferred_element_type=F32)
                s_ref[mp] = s
                max_ref[mp] = jnp.max(s, axis=0, keepdims=True)

        def update(j, buf, stats, qi=qi, acc=acc):
            s_ref, max_ref = buf
            vt = vt_ref[j]
            shift = slope * ((j - qi) * t).astype(F32)
            out = []
            for mp in range(2):
                m_new = jnp.maximum(stats[mp], max_ref[mp] + shift)
                p = jnp.exp2(s_ref[mp] - (m_new - shift)).astype(BF16)
                alpha = jnp.exp2(stats[mp] - m_new)
                acc[mp] = alpha * acc[mp] + jnp.dot(vt, p, preferred_element_type=F32)
                out.append(m_new)
            return tuple(out)

        def scores_diag(buf, qi=qi, qt_maps=qt_maps):
            s_ref, max_ref = buf
            k_lo = jnp.concatenate([k_ref[0, qi * t:qi * t + half, :], kaug_ref[0:half, :]], axis=1)
            k_hi = jnp.concatenate([k_ref[0, qi * t + half:(qi + 1) * t, :], kaug_ref[half:, :]], axis=1)
            for mp in range(2):
                s_lo = jnp.dot(k_lo, qt_maps[mp], preferred_element_type=F32) + mask_ref[0:half, :]
                s_hi = jnp.dot(k_hi, qt_maps[mp][:, half:], preferred_element_type=F32) \
                    + mask_ref[half:, half:]
                s_ref[mp, 0:half, :] = s_lo
                s_ref[mp, half:, half:] = s_hi
                max_lo = jnp.max(s_lo, axis=0, keepdims=True)
                max_ref[mp, :, 0:half] = max_lo[:, :half]
                max_ref[mp, :, half:] = jnp.maximum(max_lo[:, half:], jnp.max(s_hi, axis=0, keepdims=True))

        def update_diag(buf, stats, qi=qi, acc=acc):
            s_ref, max_ref = buf
            for mp in range(2):
                m_new = jnp.maximum(stats[mp], max_ref[mp])
                p_lo = jnp.exp2(s_ref[mp, 0:half, :] - m_new).astype(BF16)
                p_hi = jnp.exp2(s_ref[mp, half:, half:] - m_new[:, half:]).astype(BF16)
                alpha = jnp.exp2(stats[mp] - m_new)
                pv = jnp.dot(vt_ref[qi, :, 0:half], p_lo, preferred_element_type=F32)
                pv_hi = jnp.dot(vt_ref[qi, :, half:], p_hi, preferred_element_type=F32)
                acc[mp, :, 0:half] = alpha[:, :half] * acc[mp, :, 0:half] + pv[:, :half]
                acc[mp, :, half:] = alpha[:, half:] * acc[mp, :, half:] + (pv[:, half:] + pv_hi)

        first, second = s_refs

        def pair(i, stats, scores=scores, update=update, first=first, second=second):
            j = 2 * i
            scores(j + 1, second)
            stats = update(j, first, stats)
            scores(j + 2, first)
            return update(j + 1, second, stats)

        idx = jnp.int32
        if qi == 0:
            scores_diag(first)
            update_diag(first, (neg, neg))
        else:
            scores(idx(0), first)
            stats = lax.fori_loop(0, (qi - 1) // 2, pair, (neg, neg))
            if qi % 2 == 1:
                scores_diag(second)
                update_diag(second, update(idx(qi - 1), first, stats))
            else:
                scores(idx(qi - 1), second)
                stats = update(idx(qi - 2), first, stats)
                scores_diag(first)
                update_diag(first, update(idx(qi - 1), second, stats))
        if qi % 2 == 0:
            s_refs = (second, first)

        a1, a2 = acc[0], acc[1]
        ot = a1[:PAIR] / a1[PAIR:PAIR + 1] - lam * (a2[:PAIR] / a2[PAIR:PAIR + 1])
        ot = ot * lax.rsqrt(jnp.mean(ot * ot, axis=0, keepdims=True) + EPS)
        ot = ot * subln_ref[...] * (1.0 - lambda_init)
        o_ref[0, rows, :] = ot.T.astype(o_ref.dtype)


def _diff_attention(p3, slopes, lq1, lk1, lq2, lk2, subln_col, lambda_init, t):
    bsz, seq, _ = p3.shape
    vec = lambda n: pl.BlockSpec((1, n), lambda b, h: (0, 0))
    kernel = functools.partial(_diff_attn_kernel, t=t, lambda_init=lambda_init)
    return pl.pallas_call(
        kernel,
        grid=(bsz, A_HEADS),
        in_specs=[
            pl.BlockSpec(memory_space=pltpu.SMEM),
            pl.BlockSpec((1, seq, PAIR), lambda b, h: (b, 0, h)),
            pl.BlockSpec((1, seq, PAIR), lambda b, h: (b, 0, A_HEADS + h)),
            pl.BlockSpec((1, seq, PAIR), lambda b, h: (b, 0, 2 * A_HEADS + h)),
            vec(A_HEAD_DIM), vec(A_HEAD_DIM), vec(A_HEAD_DIM), vec(A_HEAD_DIM),
            pl.BlockSpec((PAIR, 1), lambda b, h: (0, 0)),
        ],
        out_specs=pl.BlockSpec((1, seq, PAIR), lambda b, h: (b, 0, h)),
        out_shape=jax.ShapeDtypeStruct((bsz, seq, A_COLS), BF16),
        scratch_shapes=[pltpu.VMEM((seq // t, PAIR + BF16_ROWS, t), BF16), pltpu.VMEM((t, t), F32),
                        pltpu.VMEM((t, PAIR), BF16),
                        pltpu.VMEM((2, t, t), F32), pltpu.VMEM((2, t, t), F32),
                        pltpu.VMEM((2, 1, t), F32), pltpu.VMEM((2, 1, t), F32),
                        pltpu.VMEM((2, 2, PAIR + BF16_ROWS, t), F32)],
        compiler_params=pltpu.CompilerParams(
            dimension_semantics=("parallel", "parallel"), vmem_limit_bytes=VMEM_LIMIT),
        name="diff_attn",
    )(slopes, p3, p3, p3, lq1, lk1, lq2, lk2, subln_col)


def _dilated_kernel(q_ref, k_ref, v_ref, kprev_ref, vprev_ref, o_ref, lse_ref, bias_ref, o_scr, lse_scr, *,
                    dil, slopes, tiles_per_seq):
    i = pl.program_id(0)
    blocks = q_ref.shape[2] // STEPS

    @pl.when(i == 0)
    def _():
        qi = lax.broadcasted_iota(jnp.int32, (STEPS, 2 * STEPS), 0)
        kj = lax.broadcasted_iota(jnp.int32, (STEPS, 2 * STEPS), 1)
        step = qi + STEPS - kj
        window = (step >= 0) & (step <= STEPS)
        dist = (step * dil).astype(F32)
        for hd in range(B_HEADS):
            alibi = -float(slopes[hd]) * dist
            bias_ref[0, hd] = jnp.where(window & (kj >= STEPS), alibi, -jnp.inf)
            bias_ref[1, hd] = jnp.where(window, alibi, -jnp.inf)

    lane = lax.broadcasted_iota(jnp.int32, (STEPS, PAIR), 1)
    low = lane < B_HEAD_DIM
    ones = jnp.ones((2 * STEPS, PAIR), BF16)

    def unit(u, carry):
        r = u // blocks
        nl = u % blocks
        has_prev = jnp.logical_or(nl > 0, i % tiles_per_seq != 0)
        variant = has_prev.astype(jnp.int32)
        cur = pl.ds(pl.multiple_of(nl * STEPS, STEPS), STEPS)
        before = pl.ds(pl.multiple_of(jnp.maximum(nl - 1, 0) * STEPS, STEPS), STEPS)
        q = q_ref[0, r, cur, :].astype(F32) * (B_HEAD_DIM ** -0.5)
        k_before = jnp.where(nl > 0, k_ref[0, r, before, :], kprev_ref[0, r])
        v_before = jnp.where(nl > 0, v_ref[0, r, before, :], vprev_ref[0, r])
        kk = jnp.concatenate([k_before, k_ref[0, r, cur, :]], axis=0)
        vv = jnp.concatenate([v_before, v_ref[0, r, cur, :]], axis=0)
        start = nl * (STEPS * dil) + r
        rows = pl.ds(start, STEPS, stride=dil) if dil > 1 else pl.ds(pl.multiple_of(start, STEPS), STEPS)
        for pr in range(B_HEADS // 2):
            cols = slice(pr * PAIR, (pr + 1) * PAIR)
            qp, kp = q[:, cols], kk[:, cols]
            v_ones = jnp.concatenate([vv[:, cols], ones], axis=1)
            o_halves, lse_halves = [], []
            for par in range(2):
                qm = jnp.where(low, qp, 0.0) if par == 0 else jnp.where(low, 0.0, qp)
                s = _nt_dot(qm.astype(BF16), kp) + bias_ref[variant, 2 * pr + par]
                m = jnp.max(s, axis=-1, keepdims=True)
                e = jnp.exp(s - m).astype(BF16)
                o_den = jnp.dot(e, v_ones, preferred_element_type=F32)
                den = o_den[:, PAIR:]
                o_halves.append(o_den[:, :PAIR] / den)
                lse_halves.append(m + jnp.log(den))
            o_scr[pr, rows, :] = jnp.where(low, o_halves[0], o_halves[1])
            lse_scr[pr, rows, :] = jnp.where(low, lse_halves[0], lse_halves[1])
        return carry

    lax.fori_loop(0, dil * blocks, unit, 0, unroll=4)
    for pr in range(B_HEADS // 2):
        cols = slice(pr * PAIR, (pr + 1) * PAIR)
        o_ref[:, cols] = o_scr[pr].astype(o_ref.dtype)
        lse_ref[:, cols] = lse_scr[pr]


def _dilated_attention(pb, bsz, seq, dil):
    tiles, _, per_res, _ = pb.shape
    tm = dil * per_res
    tile_blk = (1, dil, per_res, B_COLS)
    prev_blk = (1, dil, STEPS, B_COLS)
    last = per_res // STEPS - 1
    prev = lambda i: jnp.maximum(i - 1, 0)
    kernel = functools.partial(_dilated_kernel, dil=dil, slopes=_alibi_slopes(B_HEADS),
                               tiles_per_seq=tiles // bsz)
    out_block = pl.BlockSpec((tm, B_COLS), lambda i: (i, 0))
    return pl.pallas_call(
        kernel,
        grid=(tiles,),
        in_specs=[
            pl.BlockSpec(tile_blk, lambda i: (i, 0, 0, 0)),
            pl.BlockSpec(tile_blk, lambda i: (i, 0, 0, 1)),
            pl.BlockSpec(tile_blk, lambda i: (i, 0, 0, 2)),
            pl.BlockSpec(prev_blk, lambda i: (prev(i), 0, last, 1)),
            pl.BlockSpec(prev_blk, lambda i: (prev(i), 0, last, 2)),
        ],
        out_specs=[out_block, out_block],
        out_shape=[jax.ShapeDtypeStruct((bsz * seq, B_COLS), BF16),
                   jax.ShapeDtypeStruct((bsz * seq, B_COLS), F32)],
        scratch_shapes=[pltpu.VMEM((2, B_HEADS, STEPS, 2 * STEPS), F32),
                        pltpu.VMEM((B_HEADS // 2, tm, PAIR), F32), pltpu.VMEM((B_HEADS // 2, tm, PAIR), F32)],
        compiler_params=pltpu.CompilerParams(
            dimension_semantics=("arbitrary",), vmem_limit_bytes=VMEM_LIMIT),
        name=f"dilated_attn_d{dil}",
    )(pb, pb, pb, pb, pb)


def _merge_kernel(oa_ref, o0_ref, o1_ref, o2_ref, l0_ref, l1_ref, l2_ref, ga_ref, gb_ref, x_ref,
                  wpa_ref, wpb_ref, wo_ref, g_ref, b_ref, out_ref):
    l0, l1, l2 = l0_ref[...], l1_ref[...], l2_ref[...]
    mx = jnp.maximum(jnp.maximum(l0, l1), l2)
    e0, e1, e2 = jnp.exp(l0 - mx), jnp.exp(l1 - mx), jnp.exp(l2 - mx)
    ob = (e0 * o0_ref[...].astype(F32) + e1 * o1_ref[...].astype(F32) + e2 * o2_ref[...].astype(F32)) \
        / (e0 + e1 + e2)
    ya = jnp.dot(oa_ref[...], wpa_ref[...], preferred_element_type=F32)
    yb = jnp.dot(ob.astype(BF16), wpb_ref[...], preferred_element_type=F32)
    y = ga_ref[...].astype(F32) * ya + gb_ref[...].astype(F32) * yb
    z = ALPHA * x_ref[...] + jnp.dot(y.astype(BF16), wo_ref[...], preferred_element_type=F32)
    out_ref[...] = _layer_norm(z, g_ref[...], b_ref[...])


def _merge(oa, obs, lses, gates, x2, wpa, wpb, wo, g, b, tm):
    m = x2.shape[0]
    row = lambda w: pl.BlockSpec((tm, w), lambda i: (i, 0))
    full = lambda a: pl.BlockSpec(a.shape, lambda i: (0, 0))
    return pl.pallas_call(
        _merge_kernel,
        grid=(m // tm,),
        in_specs=[row(A_COLS), row(B_COLS), row(B_COLS), row(B_COLS), row(B_COLS), row(B_COLS), row(B_COLS),
                  pl.BlockSpec((tm, D_MODEL), lambda i: (i, 0)), pl.BlockSpec((tm, D_MODEL), lambda i: (i, 1)),
                  row(D_MODEL), full(wpa), full(wpb), full(wo), full(g), full(b)],
        out_specs=row(D_MODEL),
        out_shape=jax.ShapeDtypeStruct((m, D_MODEL), F32),
        compiler_params=pltpu.CompilerParams(dimension_semantics=("parallel",), vmem_limit_bytes=VMEM_LIMIT),
        name="merge_ln1",
    )(oa, *obs, *lses, gates, gates, x2, wpa, wpb, wo, g, b)


HALO = BF16_ROWS


def _ffn_kernel(x_ref, halo_ref, wup_ref, wconv_ref, bconv_ref, wd_ref, g_ref, b_ref,
                out_ref, xcat_ref, acc_ref, *, tiles_per_seq):
    i = pl.program_id(0)
    halo = jnp.where(i % tiles_per_seq == 0, 0.0, halo_ref[...])
    xcat_ref[0:HALO, :] = halo.astype(BF16)
    xcat_ref[HALO:, :] = x_ref[...].astype(BF16)
    xcat = xcat_ref[...]

    def conv(col0, width):
        cols = slice(col0, col0 + width)
        hfull = jnp.dot(xcat, wup_ref[:, cols], preferred_element_type=F32)
        out = bconv_ref[:, cols] + wconv_ref[CONV_WIDTH - 1:CONV_WIDTH, cols] * hfull[HALO:, :]
        for back in range(1, CONV_WIDTH):
            shifted = pltpu.roll(hfull, back, axis=0)[HALO:, :]
            out = out + wconv_ref[CONV_WIDTH - 1 - back:CONV_WIDTH - back, cols] * shifted
        return out

    for col0, width in FF_CHUNKS:
        a = conv(col0, width)
        gv = conv(D_FF + col0, width)
        f = 0.5 * a * (1.0 + lax.erf(a * (2.0 ** -0.5))) * gv
        y = jnp.dot(f.astype(BF16), wd_ref[col0:col0 + width, :], preferred_element_type=F32)
        if col0 == 0:
            acc_ref[...] = y
        else:
            acc_ref[...] += y

    z = ALPHA * x_ref[...] + acc_ref[...]
    out_ref[...] = _layer_norm(z, g_ref[...], b_ref[...])


def _ffn(x1, w_up, w_conv, b_conv, w_down, g, b, seq, tm):
    m = x1.shape[0]
    halo_blocks = tm // HALO
    kernel = functools.partial(_ffn_kernel, tiles_per_seq=seq // tm)
    full = lambda a: pl.BlockSpec(a.shape, lambda i: (0, 0))
    return pl.pallas_call(
        kernel,
        grid=(m // tm,),
        in_specs=[
            pl.BlockSpec((tm, D_MODEL), lambda i: (i, 0)),
            pl.BlockSpec((HALO, D_MODEL), lambda i: (jnp.maximum(i * halo_blocks - 1, 0), 0)),
            full(w_up), full(w_conv), full(b_conv), full(w_down), full(g), full(b),
        ],
        out_specs=pl.BlockSpec((tm, D_MODEL), lambda i: (i, 0)),
        out_shape=jax.ShapeDtypeStruct((m, D_MODEL), F32),
        scratch_shapes=[pltpu.VMEM((HALO + tm, D_MODEL), BF16), pltpu.VMEM((tm, D_MODEL), F32)],
        compiler_params=pltpu.CompilerParams(dimension_semantics=("parallel",), vmem_limit_bytes=VMEM_LIMIT),
        name="ffn_ln2",
    )(x1, x1, w_up, w_conv, b_conv, w_down, g, b)


def kernel(x, w_in, b_gate, lambda_q1, lambda_k1, lambda_q2, lambda_k2, subln_w, w_pa, w_pb, w_o, ln1_g, ln1_b,
           w_up, w_conv, b_conv, w_down, ln2_g, ln2_b):
    bsz, seq, d = x.shape
    assert (seq, d) == (4096, D_MODEL) and w_in.shape[0] == DEPTH
    slopes_a = jnp.asarray(_alibi_slopes(A_HEADS))
    for l in range(DEPTH):
        lambda_init = 0.8 - 0.6 * math.exp(-0.3 * l)
        x2 = x.reshape(bsz * seq, d)
        w_wide = jnp.concatenate([w_in[l][:, :3 * A_COLS], w_in[l][:, QKV_COLS:]], axis=1).astype(BF16)
        proj_a, gates, xb = _projection_wide(x2, w_wide, b_gate[l][None, :], tm=PROJ_ROWS)
        group_cols = [3 * A_COLS + (part * N_GROUPS + g) * B_COLS for g in range(N_GROUPS) for part in range(3)]
        w_groups = jnp.concatenate([w_in[l][:, s:s + B_COLS] for s in group_cols], axis=1).astype(BF16)
        proj_groups = _projection_groups(xb, w_groups, tm=PROJ_ROWS)
        oa = _diff_attention(proj_a.reshape(bsz, seq, 3 * A_COLS), slopes_a, lambda_q1[l][None],
                             lambda_k1[l][None], lambda_q2[l][None], lambda_k2[l][None], subln_w[l][:, None],
                             lambda_init, t=512)
        obs, lses = [], []
        for pb, (window, dil) in zip(proj_groups, B_PATTERNS):
            assert window // dil == STEPS
            o, lse = _dilated_attention(pb, bsz, seq, dil)
            obs.append(o)
            lses.append(lse)

        x1 = _merge(oa.reshape(bsz * seq, A_COLS), obs, lses, gates, x2,
                    w_pa[l].astype(BF16), w_pb[l].astype(BF16), w_o[l].astype(BF16),
                    ln1_g[l][None], ln1_b[l][None], tm=512)
        x2 = _ffn(x1, w_up[l].astype(BF16), w_conv[l], b_conv[l][None], w_down[l].astype(BF16),
                  ln2_g[l][None], ln2_b[l][None], seq, tm=512)
        x = x2.reshape(bsz, seq, d)
    return x
```

```python
import functools
import math

import numpy as np
import jax
import jax.numpy as jnp
from jax import lax
from jax.experimental import pallas as pl
from jax.experimental.pallas import tpu as pltpu

BF16 = jnp.bfloat16
F32 = jnp.float32

D_MODEL = 1024
A_HEADS = 8
A_HEAD_DIM = 64
B_PATTERNS = ((128, 1), (512, 4), (2048, 16))
B_HEADS = 8
B_HEAD_DIM = 64
D_FF = 2816
CONV_WIDTH = 3
EPS = 1e-5
DEPTH = 1
ALPHA = (2.0 * DEPTH) ** 0.25

A_COLS = A_HEADS * 2 * A_HEAD_DIM
B_COLS = B_HEADS * B_HEAD_DIM
N_GROUPS = len(B_PATTERNS)
QKV_COLS = 3 * A_COLS + 3 * N_GROUPS * B_COLS
PAIR = 2 * A_HEAD_DIM
STEPS = 128
BF16_ROWS = 16
LOG2E = math.log2(math.e)
BF16_EXACT = 256
SLOPE_TERMS = 3
PROJ_ROWS = 2048
MXU_WIDTH = 256
FF_CHUNKS = ((0, 6 * MXU_WIDTH), (6 * MXU_WIDTH, 5 * MXU_WIDTH))

VMEM_LIMIT = 56 * 1024 * 1024


def _alibi_slopes(n):
    return np.power(np.float32(2.0), -8.0 * (np.arange(n, dtype=np.float32) + 1) / n).astype(np.float32)


def _nt_dot(a, b):
    return lax.dot_general(a, b, (((1,), (1,)), ((), ())), preferred_element_type=F32)


def _layer_norm(z, g, b):
    mu = jnp.mean(z, axis=-1, keepdims=True)
    zc = z - mu
    var = jnp.mean(zc * zc, axis=-1, keepdims=True)
    return zc * lax.rsqrt(var + EPS) * g + b


WIDE_COLS = 1024
A_TILES = 3 * A_COLS // WIDE_COLS
GROUP_TILES = 3
SPLIT = 4


def _proj_wide_kernel(x_ref, w_ref, b_ref, pa_ref, g_ref, xb_ref):
    j = pl.program_id(1)

    @pl.when(j == 0)
    def _():
        xb_ref[...] = x_ref[...].astype(BF16)

    @pl.when(j < A_TILES)
    def _():
        pa_ref[...] = jnp.dot(xb_ref[...], w_ref[...], preferred_element_type=F32).astype(pa_ref.dtype)

    @pl.when(j >= A_TILES)
    def _():
        z = jnp.dot(xb_ref[...], w_ref[...], preferred_element_type=F32) + b_ref[...]
        g_ref[...] = (0.5 * jnp.tanh(0.5 * z) + 0.5).astype(g_ref.dtype)


def _projection_wide(x2, wb, b_gate, tm):
    m, k = x2.shape
    tn = WIDE_COLS
    gate_tiles = b_gate.shape[1] // tn
    clamp = lambda j, first, count: jnp.clip(j - first, 0, count - 1)
    return pl.pallas_call(
        _proj_wide_kernel,
        grid=(m // tm, A_TILES + gate_tiles),
        in_specs=[pl.BlockSpec((tm, k), lambda i, j: (i, 0)),
                  pl.BlockSpec((k, tn), lambda i, j: (0, j)),
                  pl.BlockSpec((1, tn), lambda i, j: (0, clamp(j, A_TILES, gate_tiles)))],
        out_specs=[pl.BlockSpec((tm, tn), lambda i, j: (i, clamp(j, 0, A_TILES))),
                   pl.BlockSpec((tm, tn), lambda i, j: (i, clamp(j, A_TILES, gate_tiles))),
                   pl.BlockSpec((tm, k), lambda i, j: (i, 0))],
        out_shape=[jax.ShapeDtypeStruct((m, A_TILES * tn), BF16),
                   jax.ShapeDtypeStruct((m, gate_tiles * tn), BF16),
                   jax.ShapeDtypeStruct((m, k), BF16)],
        compiler_params=pltpu.CompilerParams(
            dimension_semantics=("parallel", "arbitrary"), vmem_limit_bytes=VMEM_LIMIT),
        name="proj_wide",
    )(x2, wb, b_gate)


def _proj_groups_kernel(x_ref, w_ref, *rest):
    group_refs, (acc_ref, tmp_ref) = rest[:N_GROUPS], rest[N_GROUPS:]
    j = pl.program_id(1)
    lane_tiles, tm, _ = acc_ref.shape
    for g, (o_ref, (_, dil)) in enumerate(zip(group_refs, B_PATTERNS)):
        @pl.when((j >= g * GROUP_TILES) & (j < (g + 1) * GROUP_TILES))
        def _(o_ref=o_ref, dil=dil):
            acc = jnp.dot(x_ref[...], w_ref[...], preferred_element_type=F32)
            if dil == 1:
                o_ref[0, 0] = acc.astype(o_ref.dtype)
                return
            for c in range(lane_tiles):
                acc_ref[c] = acc[:, c * PAIR:(c + 1) * PAIR]
            for c in range(lane_tiles):
                cols = slice(c * PAIR, (c + 1) * PAIR)
                if dil == SPLIT:
                    for r in range(dil):
                        o_ref[0, r, :, cols] = acc_ref[c, pl.ds(r, tm // dil, stride=dil), :].astype(o_ref.dtype)
                else:
                    assert dil == SPLIT * SPLIT
                    part = tm // SPLIT
                    for r in range(SPLIT):
                        tmp_ref[c, r * part:(r + 1) * part, :] = acc_ref[c, pl.ds(r, part, stride=SPLIT), :]
                    for r in range(dil):
                        start = (r % SPLIT) * part + r // SPLIT
                        o_ref[0, r, :, cols] = \
                            tmp_ref[c, pl.ds(start, tm // dil, stride=SPLIT), :].astype(o_ref.dtype)


def _projection_groups(xb, wb, tm):
    m, k = xb.shape
    tn = B_COLS
    clamp = lambda j, first: jnp.clip(j - first, 0, GROUP_TILES - 1)
    return pl.pallas_call(
        _proj_groups_kernel,
        grid=(m // tm, N_GROUPS * GROUP_TILES),
        in_specs=[pl.BlockSpec((tm, k), lambda i, j: (i, 0)),
                  pl.BlockSpec((k, tn), lambda i, j: (0, j))],
        out_specs=[pl.BlockSpec((1, dil, tm // dil, tn), lambda i, j, g=g: (i, 0, 0, clamp(j, g * GROUP_TILES)))
                   for g, (_, dil) in enumerate(B_PATTERNS)],
        out_shape=[jax.ShapeDtypeStruct((m // tm, dil, tm // dil, GROUP_TILES * tn), BF16)
                   for _, dil in B_PATTERNS],
        scratch_shapes=[pltpu.VMEM((tn // PAIR, tm, PAIR), F32), pltpu.VMEM((tn // PAIR, tm, PAIR), F32)],
        compiler_params=pltpu.CompilerParams(
            dimension_semantics=("parallel", "arbitrary"), vmem_limit_bytes=VMEM_LIMIT),
        name="proj_groups",
    )(xb, wb)


def _diff_attn_kernel(slopes_ref, q_ref, k_ref, v_ref, lq1_ref, lk1_ref, lq2_ref, lk2_ref, subln_ref,
                      o_ref, vt_ref, mask_ref, kaug_ref, sa_ref, sb_ref, ma_ref, mb_ref, acc_ref, *,
                      t, lambda_init):
    slope = slopes_ref[pl.program_id(1)] * LOG2E
    nchunks = vt_ref.shape[0]

    def transpose_v(c, carry):
        r0 = pl.multiple_of(c * t, t)
        vt_ref[c, 0:PAIR, :] = v_ref[0, pl.ds(r0, t), :].astype(F32).T.astype(BF16)
        vt_ref[c, PAIR:, :] = jnp.ones((BF16_ROWS, t), BF16)
        return carry
    lax.fori_loop(0, nchunks, transpose_v, 0)
    krow = lax.broadcasted_iota(jnp.int32, (t, t), 0)
    qcol = lax.broadcasted_iota(jnp.int32, (t, t), 1)
    mask_ref[...] = jnp.where(krow <= qcol, 0.0, -jnp.inf)

    r = lax.broadcasted_iota(jnp.int32, (t, PAIR), 0)
    c = lax.broadcasted_iota(jnp.int32, (t, PAIR), 1)
    r_low = (r % BF16_EXACT).astype(F32)
    r_high = (r - r % BF16_EXACT).astype(F32)
    kaug_ref[...] = jnp.where(c < SLOPE_TERMS, r_low, jnp.where(c < 2 * SLOPE_TERMS, r_high, 0.0)).astype(BF16)
    rest = jnp.full((PAIR, t), slope, F32)
    wrow = lax.broadcasted_iota(jnp.int32, (PAIR, t), 0)
    waug = jnp.zeros((PAIR, t), F32)
    for term in range(SLOPE_TERMS):
        part = rest.astype(BF16).astype(F32)
        waug = jnp.where((wrow == term) | (wrow == SLOPE_TERMS + term), part, waug)
        rest = rest - part
    waug = waug.astype(BF16)

    lam = (jnp.exp(jnp.sum(lq1_ref[...] * lk1_ref[...], axis=-1, keepdims=True))
           - jnp.exp(jnp.sum(lq2_ref[...] * lk2_ref[...], axis=-1, keepdims=True))
           + lambda_init)
    dim = lax.broadcasted_iota(jnp.int32, (PAIR, t), 0)
    neg = jnp.full((1, t), -jnp.inf, F32)
    half = t // 2
    s_refs = ((sa_ref, ma_ref), (sb_ref, mb_ref))

    for qi in range(nchunks):
        rows = slice(qi * t, (qi + 1) * t)
        acc = acc_ref.at[qi % 2]
        qt = q_ref[0, rows, :].astype(F32).T * (A_HEAD_DIM ** -0.5 * LOG2E)
        qt_maps = (jnp.concatenate([jnp.where(dim < A_HEAD_DIM, qt, 0.0).astype(BF16), waug], axis=0),
                   jnp.concatenate([jnp.where(dim < A_HEAD_DIM, 0.0, qt).astype(BF16), waug], axis=0))
        acc[...] = jnp.zeros(acc.shape, F32)

        def scores(j, buf, qt_maps=qt_maps):
            s_ref, max_ref = buf
            kc = k_ref[0, pl.ds(pl.multiple_of(j * t, t), t), :]
            kc = jnp.concatenate([kc, kaug_ref[...]], axis=1)
            for mp in range(2):
                s = jnp.dot(kc, qt_maps[mp], preferred_element_type=F32)
                s_ref[mp] = s
                max_ref[mp] = jnp.max(s, axis=0, keepdims=True)

        def update(j, buf, stats, qi=qi, acc=acc):
            s_ref, max_ref = buf
            vt = vt_ref[j]
            shift = slope * ((j - qi) * t).astype(F32)
            out = []
            for mp in range(2):
                m_new = jnp.maximum(stats[mp], max_ref[mp] + shift)
                p = jnp.exp2(s_ref[mp] - (m_new - shift)).astype(BF16)
                alpha = jnp.exp2(stats[mp] - m_new)
                acc[mp] = alpha * acc[mp] + jnp.dot(vt, p, preferred_element_type=F32)
                out.append(m_new)
            return tuple(out)

        def scores_diag(buf, qi=qi, qt_maps=qt_maps):
            s_ref, max_ref = buf
            k_lo = jnp.concatenate([k_ref[0, qi * t:qi * t + half, :], kaug_ref[0:half, :]], axis=1)
            k_hi = jnp.concatenate([k_ref[0, qi * t + half:(qi + 1) * t, :], kaug_ref[half:, :]], axis=1)
            for mp in range(2):
                s_lo = jnp.dot(k_lo, qt_maps[mp], preferred_element_type=F32) + mask_ref[0:half, :]
                s_hi = jnp.dot(k_hi, qt_maps[mp][:, half:], preferred_element_type=F32) \
                    + mask_ref[half:, half:]
                s_ref[mp, 0:half, :] = s_lo
                s_ref[mp, half:, half:] = s_hi
                max_lo = jnp.max(s_lo, axis=0, keepdims=True)
                max_ref[mp, :, 0:half] = max_lo[:, :half]
                max_ref[mp, :, half:] = jnp.maximum(max_lo[:, half:], jnp.max(s_hi, axis=0, keepdims=True))

        def update_diag(buf, stats, qi=qi, acc=acc):
            s_ref, max_ref = buf
            for mp in range(2):
                m_new = jnp.maximum(stats[mp], max_ref[mp])
                p_lo = jnp.exp2(s_ref[mp, 0:half, :] - m_new).astype(BF16)
                p_hi = jnp.exp2(s_ref[mp, half:, half:] - m_new[:, half:]).astype(BF16)
                alpha = jnp.exp2(stats[mp] - m_new)
                pv = jnp.dot(vt_ref[qi, :, 0:half], p_lo, preferred_element_type=F32)
                pv_hi = jnp.dot(vt_ref[qi, :, half:], p_hi, preferred_element_type=F32)
                acc[mp, :, 0:half] = alpha[:, :half] * acc[mp, :, 0:half] + pv[:, :half]
                acc[mp, :, half:] = alpha[:, half:] * acc[mp, :, half:] + (pv[:, half:] + pv_hi)

        first, second = s_refs

        def pair(i, stats, scores=scores, update=update, first=first, second=second):
            j = 2 * i
            scores(j + 1, second)
            stats = update(j, first, stats)
            scores(j + 2, first)
            return update(j + 1, second, stats)

        idx = jnp.int32
        if qi == 0:
            scores_diag(first)
            update_diag(first, (neg, neg))
        else:
            scores(idx(0), first)
            stats = lax.fori_loop(0, (qi - 1) // 2, pair, (neg, neg))
            if qi % 2 == 1:
                scores_diag(second)
                update_diag(second, update(idx(qi - 1), first, stats))
            else:
                scores(idx(qi - 1), second)
                stats = update(idx(qi - 2), first, stats)
                scores_diag(first)
                update_diag(first, update(idx(qi - 1), second, stats))
        if qi % 2 == 0:
            s_refs = (second, first)

        a1, a2 = acc[0], acc[1]
        ot = a1[:PAIR] / a1[PAIR:PAIR + 1] - lam * (a2[:PAIR] / a2[PAIR:PAIR + 1])
        ot = ot * lax.rsqrt(jnp.mean(ot * ot, axis=0, keepdims=True) + EPS)
        ot = ot * subln_ref[...] * (1.0 - lambda_init)
        o_ref[0, rows, :] = ot.T.astype(o_ref.dtype)


def _diff_attention(p3, slopes, lq1, lk1, lq2, lk2, subln_col, lambda_init, t):
    bsz, seq, _ = p3.shape
    vec = lambda n: pl.BlockSpec((1, n), lambda b, h: (0, 0))
    kernel = functools.partial(_diff_attn_kernel, t=t, lambda_init=lambda_init)
    return pl.pallas_call(
        kernel,
        grid=(bsz, A_HEADS),
        in_specs=[
            pl.BlockSpec(memory_space=pltpu.SMEM),
            pl.BlockSpec((1, seq, PAIR), lambda b, h: (b, 0, h)),
            pl.BlockSpec((1, seq, PAIR), lambda b, h: (b, 0, A_HEADS + h)),
            pl.BlockSpec((1, seq, PAIR), lambda b, h: (b, 0, 2 * A_HEADS + h)),
            vec(A_HEAD_DIM), vec(A_HEAD_DIM), vec(A_HEAD_DIM), vec(A_HEAD_DIM),
            pl.BlockSpec((PAIR, 1), lambda b, h: (0, 0)),
        ],
        out_specs=pl.BlockSpec((1, seq, PAIR), lambda b, h: (b, 0, h)),
        out_shape=jax.ShapeDtypeStruct((bsz, seq, A_COLS), BF16),
        scratch_shapes=[pltpu.VMEM((seq // t, PAIR + BF16_ROWS, t), BF16), pltpu.VMEM((t, t), F32),
                        pltpu.VMEM((t, PAIR), BF16),
                        pltpu.VMEM((2, t, t), F32), pltpu.VMEM((2, t, t), F32),
                        pltpu.VMEM((2, 1, t), F32), pltpu.VMEM((2, 1, t), F32),
                        pltpu.VMEM((2, 2, PAIR + BF16_ROWS, t), F32)],
        compiler_params=pltpu.CompilerParams(
            dimension_semantics=("parallel", "parallel"), vmem_limit_bytes=VMEM_LIMIT),
        name="diff_attn",
    )(slopes, p3, p3, p3, lq1, lk1, lq2, lk2, subln_col)


def _dilated_kernel(q_ref, k_ref, v_ref, kprev_ref, vprev_ref, o_ref, lse_ref, bias_ref, o_scr, lse_scr, *,
                    dil, slopes, tiles_per_seq):
    i = pl.program_id(0)
    blocks = q_ref.shape[2] // STEPS

    @pl.when(i == 0)
    def _():
        qi = lax.broadcasted_iota(jnp.int32, (STEPS, 2 * STEPS), 0)
        kj = lax.broadcasted_iota(jnp.int32, (STEPS, 2 * STEPS), 1)
        step = qi + STEPS - kj
        window = (step >= 0) & (step <= STEPS)
        dist = (step * dil).astype(F32)
        for hd in range(B_HEADS):
            alibi = -float(slopes[hd]) * dist
            bias_ref[0, hd] = jnp.where(window & (kj >= STEPS), alibi, -jnp.inf)
            bias_ref[1, hd] = jnp.where(window, alibi, -jnp.inf)

    lane = lax.broadcasted_iota(jnp.int32, (STEPS, PAIR), 1)
    low = lane < B_HEAD_DIM
    low_keys = lax.broadcasted_iota(jnp.int32, (2 * STEPS, PAIR), 1) < B_HEAD_DIM
    ones_even = jnp.where(low_keys, 1.0, 0.0).astype(BF16)
    ones_odd = jnp.where(low_keys, 0.0, 1.0).astype(BF16)

    def unit(u, carry):
        r = u // blocks
        nl = u % blocks
        has_prev = jnp.logical_or(nl > 0, i % tiles_per_seq != 0)
        variant = has_prev.astype(jnp.int32)
        cur = pl.ds(pl.multiple_of(nl * STEPS, STEPS), STEPS)
        before = pl.ds(pl.multiple_of(jnp.maximum(nl - 1, 0) * STEPS, STEPS), STEPS)
        q = q_ref[0, r, cur, :].astype(F32) * (B_HEAD_DIM ** -0.5)
        k_before = jnp.where(nl > 0, k_ref[0, r, before, :], kprev_ref[0, r])
        v_before = jnp.where(nl > 0, v_ref[0, r, before, :], vprev_ref[0, r])
        kk = jnp.concatenate([k_before, k_ref[0, r, cur, :]], axis=0)
        vv = jnp.concatenate([v_before, v_ref[0, r, cur, :]], axis=0)
        start = nl * (STEPS * dil) + r
        rows = pl.ds(start, STEPS, stride=dil) if dil > 1 else pl.ds(pl.multiple_of(start, STEPS), STEPS)
        for pr in range(B_HEADS // 2):
            cols = slice(pr * PAIR, (pr + 1) * PAIR)
            qp, kp = q[:, cols], kk[:, cols]
            vp = vv[:, cols].astype(F32)
            v_blocks = jnp.concatenate([
                jnp.concatenate([jnp.where(low_keys, vp, 0.0).astype(BF16), ones_even], axis=1),
                jnp.concatenate([jnp.where(low_keys, 0.0, vp).astype(BF16), ones_odd], axis=1)], axis=0)
            probs, maxes = [], []
            for par in range(2):
                qm = jnp.where(low, qp, 0.0) if par == 0 else jnp.where(low, 0.0, qp)
                s = _nt_dot(qm.astype(BF16), kp) + bias_ref[variant, 2 * pr + par]
                m = jnp.max(s, axis=-1, keepdims=True)
                probs.append(jnp.exp(s - m).astype(BF16))
                maxes.append(m)
            o_den = jnp.dot(jnp.concatenate(probs, axis=1), v_blocks, preferred_element_type=F32)
            den = o_den[:, PAIR:]
            o_scr[pr, rows, :] = o_den[:, :PAIR] / den
            lse_scr[pr, rows, :] = jnp.where(low, maxes[0], maxes[1]) + jnp.log(den)
        return carry

    lax.fori_loop(0, dil * blocks, unit, 0, unroll=4)
    for pr in range(B_HEADS // 2):
        cols = slice(pr * PAIR, (pr + 1) * PAIR)
        o_ref[:, cols] = o_scr[pr].astype(o_ref.dtype)
        lse_ref[:, cols] = lse_scr[pr]


def _dilated_attention(pb, bsz, seq, dil):
    tiles, _, per_res, _ = pb.shape
    tm = dil * per_res
    tile_blk = (1, dil, per_res, B_COLS)
    prev_blk = (1, dil, STEPS, B_COLS)
    last = per_res // STEPS - 1
    prev = lambda i: jnp.maximum(i - 1, 0)
    kernel = functools.partial(_dilated_kernel, dil=dil, slopes=_alibi_slopes(B_HEADS),
                               tiles_per_seq=tiles // bsz)
    out_block = pl.BlockSpec((tm, B_COLS), lambda i: (i, 0))
    return pl.pallas_call(
        kernel,
        grid=(tiles,),
        in_specs=[
            pl.BlockSpec(tile_blk, lambda i: (i, 0, 0, 0)),
            pl.BlockSpec(tile_blk, lambda i: (i, 0, 0, 1)),
            pl.BlockSpec(tile_blk, lambda i: (i, 0, 0, 2)),
            pl.BlockSpec(prev_blk, lambda i: (prev(i), 0, last, 1)),
            pl.BlockSpec(prev_blk, lambda i: (prev(i), 0, last, 2)),
        ],
        out_specs=[out_block, out_block],
        out_shape=[jax.ShapeDtypeStruct((bsz * seq, B_COLS), BF16),
                   jax.ShapeDtypeStruct((bsz * seq, B_COLS), F32)],
        scratch_shapes=[pltpu.VMEM((2, B_HEADS, STEPS, 2 * STEPS), F32),
                        pltpu.VMEM((B_HEADS // 2, tm, PAIR), F32), pltpu.VMEM((B_HEADS // 2, tm, PAIR), F32)],
        compiler_params=pltpu.CompilerParams(
            dimension_semantics=("arbitrary",), vmem_limit_bytes=VMEM_LIMIT),
        name=f"dilated_attn_d{dil}",
    )(pb, pb, pb, pb, pb)


MERGE_SLABS = 2


def _merge_kernel(oa_ref, o0_ref, o1_ref, o2_ref, l0_ref, l1_ref, l2_ref, ga_ref, gb_ref, x_ref,
                  wpa_ref, wpb_ref, wo_ref, g_ref, b_ref, out_ref):
    slab = out_ref.shape[0] // MERGE_SLABS
    for part in range(MERGE_SLABS):
        rows = slice(part * slab, (part + 1) * slab)
        l0, l1, l2 = l0_ref[rows, :], l1_ref[rows, :], l2_ref[rows, :]
        mx = jnp.maximum(jnp.maximum(l0, l1), l2)
        e0, e1, e2 = jnp.exp(l0 - mx), jnp.exp(l1 - mx), jnp.exp(l2 - mx)
        ob = (e0 * o0_ref[rows, :].astype(F32) + e1 * o1_ref[rows, :].astype(F32)
              + e2 * o2_ref[rows, :].astype(F32)) / (e0 + e1 + e2)
        ya = jnp.dot(oa_ref[rows, :], wpa_ref[...], preferred_element_type=F32)
        yb = jnp.dot(ob.astype(BF16), wpb_ref[...], preferred_element_type=F32)
        y = ga_ref[rows, :].astype(F32) * ya + gb_ref[rows, :].astype(F32) * yb
        z = ALPHA * x_ref[rows, :] + jnp.dot(y.astype(BF16), wo_ref[...], preferred_element_type=F32)
        out_ref[rows, :] = _layer_norm(z, g_ref[...], b_ref[...])


def _merge(oa, obs, lses, gates, x2, wpa, wpb, wo, g, b, tm):
    m = x2.shape[0]
    row = lambda w: pl.BlockSpec((tm, w), lambda i: (i, 0))
    full = lambda a: pl.BlockSpec(a.shape, lambda i: (0, 0))
    return pl.pallas_call(
        _merge_kernel,
        grid=(m // tm,),
        in_specs=[row(A_COLS), row(B_COLS), row(B_COLS), row(B_COLS), row(B_COLS), row(B_COLS), row(B_COLS),
                  pl.BlockSpec((tm, D_MODEL), lambda i: (i, 0)), pl.BlockSpec((tm, D_MODEL), lambda i: (i, 1)),
                  row(D_MODEL), full(wpa), full(wpb), full(wo), full(g), full(b)],
        out_specs=row(D_MODEL),
        out_shape=jax.ShapeDtypeStruct((m, D_MODEL), F32),
        compiler_params=pltpu.CompilerParams(dimension_semantics=("parallel",), vmem_limit_bytes=VMEM_LIMIT),
        name="merge_ln1",
    )(oa, *obs, *lses, gates, gates, x2, wpa, wpb, wo, g, b)


HALO = BF16_ROWS


def _ffn_kernel(x_ref, halo_ref, wup_ref, wconv_ref, bconv_ref, wd_ref, g_ref, b_ref,
                out_ref, xcat_ref, acc_ref, *, tiles_per_seq):
    i = pl.program_id(0)
    halo = jnp.where(i % tiles_per_seq == 0, 0.0, halo_ref[...])
    xcat_ref[0:HALO, :] = halo.astype(BF16)
    xcat_ref[HALO:, :] = x_ref[...].astype(BF16)
    xcat = xcat_ref[...]

    def conv(col0, width):
        cols = slice(col0, col0 + width)
        hfull = jnp.dot(xcat, wup_ref[:, cols], preferred_element_type=F32)
        out = bconv_ref[:, cols] + wconv_ref[CONV_WIDTH - 1:CONV_WIDTH, cols] * hfull[HALO:, :]
        for back in range(1, CONV_WIDTH):
            shifted = pltpu.roll(hfull, back, axis=0)[HALO:, :]
            out = out + wconv_ref[CONV_WIDTH - 1 - back:CONV_WIDTH - back, cols] * shifted
        return out

    for col0, width in FF_CHUNKS:
        a = conv(col0, width)
        gv = conv(D_FF + col0, width)
        f = 0.5 * a * (1.0 + lax.erf(a * (2.0 ** -0.5))) * gv
        y = jnp.dot(f.astype(BF16), wd_ref[col0:col0 + width, :], preferred_element_type=F32)
        if col0 == 0:
            acc_ref[...] = y
        else:
            acc_ref[...] += y

    z = ALPHA * x_ref[...] + acc_ref[...]
    out_ref[...] = _layer_norm(z, g_ref[...], b_ref[...])


def _ffn(x1, w_up, w_conv, b_conv, w_down, g, b, seq, tm):
    m = x1.shape[0]
    halo_blocks = tm // HALO
    kernel = functools.partial(_ffn_kernel, tiles_per_seq=seq // tm)
    full = lambda a: pl.BlockSpec(a.shape, lambda i: (0, 0))
    return pl.pallas_call(
        kernel,
        grid=(m // tm,),
        in_specs=[
            pl.BlockSpec((tm, D_MODEL), lambda i: (i, 0)),
            pl.BlockSpec((HALO, D_MODEL), lambda i: (jnp.maximum(i * halo_blocks - 1, 0), 0)),
            full(w_up), full(w_conv), full(b_conv), full(w_down), full(g), full(b),
        ],
        out_specs=pl.BlockSpec((tm, D_MODEL), lambda i: (i, 0)),
        out_shape=jax.ShapeDtypeStruct((m, D_MODEL), F32),
        scratch_shapes=[pltpu.VMEM((HALO + tm, D_MODEL), BF16), pltpu.VMEM((tm, D_MODEL), F32)],
        compiler_params=pltpu.CompilerParams(dimension_semantics=("parallel",), vmem_limit_bytes=VMEM_LIMIT),
        name="ffn_ln2",
    )(x1, x1, w_up, w_conv, b_conv, w_down, g, b)


def kernel(x, w_in, b_gate, lambda_q1, lambda_k1, lambda_q2, lambda_k2, subln_w, w_pa, w_pb, w_o, ln1_g, ln1_b,
           w_up, w_conv, b_conv, w_down, ln2_g, ln2_b):
    bsz, seq, d = x.shape
    assert (seq, d) == (4096, D_MODEL) and w_in.shape[0] == DEPTH
    slopes_a = jnp.asarray(_alibi_slopes(A_HEADS))
    for l in range(DEPTH):
        lambda_init = 0.8 - 0.6 * math.exp(-0.3 * l)
        x2 = x.reshape(bsz * seq, d)
        w_wide = jnp.concatenate([w_in[l][:, :3 * A_COLS], w_in[l][:, QKV_COLS:]], axis=1).astype(BF16)
        proj_a, gates, xb = _projection_wide(x2, w_wide, b_gate[l][None, :], tm=PROJ_ROWS)
        group_cols = [3 * A_COLS + (part * N_GROUPS + g) * B_COLS for g in range(N_GROUPS) for part in range(3)]
        w_groups = jnp.concatenate([w_in[l][:, s:s + B_COLS] for s in group_cols], axis=1).astype(BF16)
        proj_groups = _projection_groups(xb, w_groups, tm=PROJ_ROWS)
        oa = _diff_attention(proj_a.reshape(bsz, seq, 3 * A_COLS), slopes_a, lambda_q1[l][None],
                             lambda_k1[l][None], lambda_q2[l][None], lambda_k2[l][None], subln_w[l][:, None],
                             lambda_init, t=512)
        obs, lses = [], []
        for pb, (window, dil) in zip(proj_groups, B_PATTERNS):
            assert window // dil == STEPS
            o, lse = _dilated_attention(pb, bsz, seq, dil)
            obs.append(o)
            lses.append(lse)

        x1 = _merge(oa.reshape(bsz * seq, A_COLS), obs, lses, gates, x2,
                    w_pa[l].astype(BF16), w_pb[l].astype(BF16), w_o[l].astype(BF16),
                    ln1_g[l][None], ln1_b[l][None], tm=512)
        x2 = _ffn(x1, w_up[l].astype(BF16), w_conv[l], b_conv[l][None], w_down[l].astype(BF16),
                  ln2_g[l][None], ln2_b[l][None], seq, tm=512)
        x = x2.reshape(bsz, seq, d)
    return x
```

```python
import functools
import math

import numpy as np
import jax
import jax.numpy as jnp
from jax import lax
from jax.experimental import pallas as pl
from jax.experimental.pallas import tpu as pltpu

BF16 = jnp.bfloat16
F32 = jnp.float32

D_MODEL = 1024
A_HEADS = 8
A_HEAD_DIM = 64
B_PATTERNS = ((128, 1), (512, 4), (2048, 16))
B_HEADS = 8
B_HEAD_DIM = 64
D_FF = 2816
CONV_WIDTH = 3
EPS = 1e-5
DEPTH = 1
ALPHA = (2.0 * DEPTH) ** 0.25

A_COLS = A_HEADS * 2 * A_HEAD_DIM
B_COLS = B_HEADS * B_HEAD_DIM
N_GROUPS = len(B_PATTERNS)
QKV_COLS = 3 * A_COLS + 3 * N_GROUPS * B_COLS
PAIR = 2 * A_HEAD_DIM
STEPS = 128
BF16_ROWS = 16
LOG2E = math.log2(math.e)
BF16_EXACT = 256
SLOPE_TERMS = 3
PROJ_ROWS = 2048
MXU_WIDTH = 256
FF_CHUNKS = ((0, 6 * MXU_WIDTH), (6 * MXU_WIDTH, 5 * MXU_WIDTH))

VMEM_LIMIT = 56 * 1024 * 1024


def _alibi_slopes(n):
    return np.power(np.float32(2.0), -8.0 * (np.arange(n, dtype=np.float32) + 1) / n).astype(np.float32)


def _nt_dot(a, b):
    return lax.dot_general(a, b, (((1,), (1,)), ((), ())), preferred_element_type=F32)


def _layer_norm(z, g, b):
    mu = jnp.mean(z, axis=-1, keepdims=True)
    zc = z - mu
    var = jnp.mean(zc * zc, axis=-1, keepdims=True)
    return zc * lax.rsqrt(var + EPS) * g + b


WIDE_COLS = 1024
A_TILES = 3 * A_COLS // WIDE_COLS
GROUP_TILES = 2
SPLIT = 4


def _proj_wide_kernel(x_ref, w_ref, b_ref, pa_ref, g_ref, xb_ref):
    j = pl.program_id(1)

    @pl.when(j == 0)
    def _():
        xb_ref[...] = x_ref[...].astype(BF16)

    @pl.when(j < A_TILES)
    def _():
        pa_ref[...] = jnp.dot(xb_ref[...], w_ref[...], preferred_element_type=F32).astype(pa_ref.dtype)

    @pl.when(j >= A_TILES)
    def _():
        z = jnp.dot(xb_ref[...], w_ref[...], preferred_element_type=F32) + b_ref[...]
        g_ref[...] = (0.5 * jnp.tanh(0.5 * z) + 0.5).astype(g_ref.dtype)


def _projection_wide(x2, wb, b_gate, tm):
    m, k = x2.shape
    tn = WIDE_COLS
    gate_tiles = b_gate.shape[1] // tn
    clamp = lambda j, first, count: jnp.clip(j - first, 0, count - 1)
    return pl.pallas_call(
        _proj_wide_kernel,
        grid=(m // tm, A_TILES + gate_tiles),
        in_specs=[pl.BlockSpec((tm, k), lambda i, j: (i, 0)),
                  pl.BlockSpec((k, tn), lambda i, j: (0, j)),
                  pl.BlockSpec((1, tn), lambda i, j: (0, clamp(j, A_TILES, gate_tiles)))],
        out_specs=[pl.BlockSpec((tm, tn), lambda i, j: (i, clamp(j, 0, A_TILES))),
                   pl.BlockSpec((tm, tn), lambda i, j: (i, clamp(j, A_TILES, gate_tiles))),
                   pl.BlockSpec((tm, k), lambda i, j: (i, 0))],
        out_shape=[jax.ShapeDtypeStruct((m, A_TILES * tn), BF16),
                   jax.ShapeDtypeStruct((m, gate_tiles * tn), BF16),
                   jax.ShapeDtypeStruct((m, k), BF16)],
        compiler_params=pltpu.CompilerParams(
            dimension_semantics=("parallel", "arbitrary"), vmem_limit_bytes=VMEM_LIMIT),
        name="proj_wide",
    )(x2, wb, b_gate)


def _proj_groups_kernel(x_ref, w_ref, *rest):
    group_refs, (acc_ref, tmp_ref) = rest[:N_GROUPS], rest[N_GROUPS:]
    j = pl.program_id(1)
    lane_tiles, tm, _ = acc_ref.shape
    for g, (o_ref, (_, dil)) in enumerate(zip(group_refs, B_PATTERNS)):
        @pl.when((j >= g * GROUP_TILES) & (j < (g + 1) * GROUP_TILES))
        def _(o_ref=o_ref, dil=dil):
            acc = jnp.dot(x_ref[...], w_ref[...], preferred_element_type=F32)
            if dil == 1:
                o_ref[0, 0] = acc.astype(o_ref.dtype)
                return
            for c in range(lane_tiles):
                acc_ref[c] = acc[:, c * PAIR:(c + 1) * PAIR]
            for c in range(lane_tiles):
                cols = slice(c * PAIR, (c + 1) * PAIR)
                if dil == SPLIT:
                    for r in range(dil):
                        o_ref[0, r, :, cols] = acc_ref[c, pl.ds(r, tm // dil, stride=dil), :].astype(o_ref.dtype)
                else:
                    assert dil == SPLIT * SPLIT
                    part = tm // SPLIT
                    for r in range(SPLIT):
                        tmp_ref[c, r * part:(r + 1) * part, :] = acc_ref[c, pl.ds(r, part, stride=SPLIT), :]
                    for r in range(dil):
                        start = (r % SPLIT) * part + r // SPLIT
                        o_ref[0, r, :, cols] = \
                            tmp_ref[c, pl.ds(start, tm // dil, stride=SPLIT), :].astype(o_ref.dtype)


def _projection_groups(xb, wb, tm):
    m, k = xb.shape
    tn = 3 * B_COLS // GROUP_TILES
    clamp = lambda j, first: jnp.clip(j - first, 0, GROUP_TILES - 1)
    return pl.pallas_call(
        _proj_groups_kernel,
        grid=(m // tm, N_GROUPS * GROUP_TILES),
        in_specs=[pl.BlockSpec((tm, k), lambda i, j: (i, 0)),
                  pl.BlockSpec((k, tn), lambda i, j: (0, j))],
        out_specs=[pl.BlockSpec((1, dil, tm // dil, tn), lambda i, j, g=g: (i, 0, 0, clamp(j, g * GROUP_TILES)))
                   for g, (_, dil) in enumerate(B_PATTERNS)],
        out_shape=[jax.ShapeDtypeStruct((m // tm, dil, tm // dil, GROUP_TILES * tn), BF16)
                   for _, dil in B_PATTERNS],
        scratch_shapes=[pltpu.VMEM((tn // PAIR, tm, PAIR), F32), pltpu.VMEM((tn // PAIR, tm, PAIR), F32)],
        compiler_params=pltpu.CompilerParams(
            dimension_semantics=("parallel", "arbitrary"), vmem_limit_bytes=VMEM_LIMIT),
        name="proj_groups",
    )(xb, wb)


def _diff_attn_kernel(slopes_ref, q_ref, k_ref, v_ref, lq1_ref, lk1_ref, lq2_ref, lk2_ref, subln_ref,
                      o_ref, vt_ref, mask_ref, kaug_ref, sa_ref, sb_ref, ma_ref, mb_ref, acc_ref, *,
                      t, lambda_init):
    slope = slopes_ref[pl.program_id(1)] * LOG2E
    nchunks = vt_ref.shape[0]

    def transpose_v(c, carry):
        r0 = pl.multiple_of(c * t, t)
        vt_ref[c, 0:PAIR, :] = v_ref[0, pl.ds(r0, t), :].astype(F32).T.astype(BF16)
        vt_ref[c, PAIR:, :] = jnp.ones((BF16_ROWS, t), BF16)
        return carry
    lax.fori_loop(0, nchunks, transpose_v, 0)
    krow = lax.broadcasted_iota(jnp.int32, (t, t), 0)
    qcol = lax.broadcasted_iota(jnp.int32, (t, t), 1)
    mask_ref[...] = jnp.where(krow <= qcol, 0.0, -jnp.inf)

    r = lax.broadcasted_iota(jnp.int32, (t, PAIR), 0)
    c = lax.broadcasted_iota(jnp.int32, (t, PAIR), 1)
    r_low = (r % BF16_EXACT).astype(F32)
    r_high = (r - r % BF16_EXACT).astype(F32)
    kaug_ref[...] = jnp.where(c < SLOPE_TERMS, r_low, jnp.where(c < 2 * SLOPE_TERMS, r_high, 0.0)).astype(BF16)
    rest = jnp.full((PAIR, t), slope, F32)
    wrow = lax.broadcasted_iota(jnp.int32, (PAIR, t), 0)
    waug = jnp.zeros((PAIR, t), F32)
    for term in range(SLOPE_TERMS):
        part = rest.astype(BF16).astype(F32)
        waug = jnp.where((wrow == term) | (wrow == SLOPE_TERMS + term), part, waug)
        rest = rest - part
    waug = waug.astype(BF16)
    dim = lax.broadcasted_iota(jnp.int32, (PAIR, t), 0)

    lam = (jnp.exp(jnp.sum(lq1_ref[...] * lk1_ref[...], axis=-1, keepdims=True))
           - jnp.exp(jnp.sum(lq2_ref[...] * lk2_ref[...], axis=-1, keepdims=True))
           + lambda_init)
    neg = jnp.full((1, t), -jnp.inf, F32)
    half = t // 2
    s_refs = ((sa_ref, ma_ref), (sb_ref, mb_ref))

    for qi in range(nchunks):
        rows = slice(qi * t, (qi + 1) * t)
        acc = acc_ref.at[qi % 2]
        qt = q_ref[0, rows, :].astype(F32).T * (A_HEAD_DIM ** -0.5 * LOG2E)
        qt_maps = (jnp.concatenate([jnp.where(dim < A_HEAD_DIM, qt, 0.0).astype(BF16), waug], axis=0),
                   jnp.concatenate([jnp.where(dim < A_HEAD_DIM, 0.0, qt).astype(BF16), waug], axis=0))
        acc[...] = jnp.zeros(acc.shape, F32)

        def scores(j, buf, qt_maps=qt_maps):
            s_ref, max_ref = buf
            kc = k_ref[0, pl.ds(pl.multiple_of(j * t, t), t), :]
            kc = jnp.concatenate([kc, kaug_ref[...]], axis=1)
            for mp in range(2):
                s = jnp.dot(kc, qt_maps[mp], preferred_element_type=F32)
                s_ref[mp] = s
                max_ref[mp] = jnp.max(s, axis=0, keepdims=True)

        def update(j, buf, stats, qi=qi, acc=acc):
            s_ref, max_ref = buf
            vt = vt_ref[j]
            shift = slope * jnp.asarray((j - qi) * t, F32)
            out = []
            for mp in range(2):
                m_new = jnp.maximum(stats[mp], max_ref[mp] + shift)
                p = jnp.exp2(s_ref[mp] - (m_new - shift)).astype(BF16)
                alpha = jnp.exp2(stats[mp] - m_new)
                acc[mp] = alpha * acc[mp] + jnp.dot(vt, p, preferred_element_type=F32)
                out.append(m_new)
            return tuple(out)

        def scores_diag(buf, qi=qi, qt_maps=qt_maps):
            s_ref, max_ref = buf
            k_lo = jnp.concatenate([k_ref[0, qi * t:qi * t + half, :], kaug_ref[0:half, :]], axis=1)
            k_hi = jnp.concatenate([k_ref[0, qi * t + half:(qi + 1) * t, :], kaug_ref[half:, :]], axis=1)
            for mp in range(2):
                s_lo = jnp.dot(k_lo, qt_maps[mp], preferred_element_type=F32) + mask_ref[0:half, :]
                s_hi = jnp.dot(k_hi, qt_maps[mp][:, half:], preferred_element_type=F32) \
                    + mask_ref[half:, half:]
                s_ref[mp, 0:half, :] = s_lo
                s_ref[mp, half:, half:] = s_hi
                max_lo = jnp.max(s_lo, axis=0, keepdims=True)
                max_ref[mp, :, 0:half] = max_lo[:, :half]
                max_ref[mp, :, half:] = jnp.maximum(max_lo[:, half:], jnp.max(s_hi, axis=0, keepdims=True))

        def update_diag(buf, stats, qi=qi, acc=acc):
            s_ref, max_ref = buf
            for mp in range(2):
                m_new = jnp.maximum(stats[mp], max_ref[mp])
                p_lo = jnp.exp2(s_ref[mp, 0:half, :] - m_new).astype(BF16)
                p_hi = jnp.exp2(s_ref[mp, half:, half:] - m_new[:, half:]).astype(BF16)
                alpha = jnp.exp2(stats[mp] - m_new)
                pv = jnp.dot(vt_ref[qi, :, 0:half], p_lo, preferred_element_type=F32)
                pv_hi = jnp.dot(vt_ref[qi, :, half:], p_hi, preferred_element_type=F32)
                acc[mp, :, 0:half] = alpha[:, :half] * acc[mp, :, 0:half] + pv[:, :half]
                acc[mp, :, half:] = alpha[:, half:] * acc[mp, :, half:] + (pv[:, half:] + pv_hi)

        first, second = s_refs

        def pair(i, stats, scores=scores, update=update, first=first, second=second):
            j = 2 * i
            scores(j + 1, second)
            stats = update(j, first, stats)
            scores(j + 2, first)
            return update(j + 1, second, stats)

        idx = jnp.int32
        if qi == 0:
            scores_diag(first)
            update_diag(first, (neg, neg))
        else:
            scores(idx(0), first)
            stats = lax.fori_loop(0, (qi - 1) // 2, pair, (neg, neg))
            if qi % 2 == 1:
                scores_diag(second)
                update_diag(second, update(idx(qi - 1), first, stats))
            else:
                scores(idx(qi - 1), second)
                stats = update(idx(qi - 2), first, stats)
                scores_diag(first)
                update_diag(first, update(idx(qi - 1), second, stats))
        if qi % 2 == 0:
            s_refs = (second, first)

        a1, a2 = acc[0], acc[1]
        ot = a1[:PAIR] / a1[PAIR:PAIR + 1] - lam * (a2[:PAIR] / a2[PAIR:PAIR + 1])
        ot = ot * lax.rsqrt(jnp.mean(ot * ot, axis=0, keepdims=True) + EPS)
        ot = ot * subln_ref[...] * (1.0 - lambda_init)
        o_ref[0, rows, :] = ot.T.astype(o_ref.dtype)


def _diff_attention(p3, slopes, lq1, lk1, lq2, lk2, subln_col, lambda_init, t):
    bsz, seq, _ = p3.shape
    vec = lambda n: pl.BlockSpec((1, n), lambda b, h: (0, 0))
    kernel = functools.partial(_diff_attn_kernel, t=t, lambda_init=lambda_init)
    return pl.pallas_call(
        kernel,
        grid=(bsz, A_HEADS),
        in_specs=[
            pl.BlockSpec(memory_space=pltpu.SMEM),
            pl.BlockSpec((1, seq, PAIR), lambda b, h: (b, 0, h)),
            pl.BlockSpec((1, seq, PAIR), lambda b, h: (b, 0, A_HEADS + h)),
            pl.BlockSpec((1, seq, PAIR), lambda b, h: (b, 0, 2 * A_HEADS + h)),
            vec(A_HEAD_DIM), vec(A_HEAD_DIM), vec(A_HEAD_DIM), vec(A_HEAD_DIM),
            pl.BlockSpec((PAIR, 1), lambda b, h: (0, 0)),
        ],
        out_specs=pl.BlockSpec((1, seq, PAIR), lambda b, h: (b, 0, h)),
        out_shape=jax.ShapeDtypeStruct((bsz, seq, A_COLS), BF16),
        scratch_shapes=[pltpu.VMEM((seq // t, PAIR + BF16_ROWS, t), BF16), pltpu.VMEM((t, t), F32),
                        pltpu.VMEM((t, PAIR), BF16),
                        pltpu.VMEM((2, t, t), F32), pltpu.VMEM((2, t, t), F32),
                        pltpu.VMEM((2, 1, t), F32), pltpu.VMEM((2, 1, t), F32),
                        pltpu.VMEM((2, 2, PAIR + BF16_ROWS, t), F32)],
        compiler_params=pltpu.CompilerParams(
            dimension_semantics=("parallel", "parallel"), vmem_limit_bytes=VMEM_LIMIT),
        name="diff_attn",
    )(slopes, p3, p3, p3, lq1, lk1, lq2, lk2, subln_col)


def _dilated_kernel(q_ref, k_ref, v_ref, kprev_ref, vprev_ref, o_ref, lse_ref, bias_ref, o_scr, lse_scr, *,
                    dil, slopes, tiles_per_seq):
    i = pl.program_id(0)
    blocks = q_ref.shape[2] // STEPS

    @pl.when(i == 0)
    def _():
        qi = lax.broadcasted_iota(jnp.int32, (STEPS, 2 * STEPS), 0)
        kj = lax.broadcasted_iota(jnp.int32, (STEPS, 2 * STEPS), 1)
        step = qi + STEPS - kj
        window = (step >= 0) & (step <= STEPS)
        dist = (step * dil).astype(F32)
        for hd in range(B_HEADS):
            alibi = -float(slopes[hd]) * dist
            bias_ref[0, hd] = jnp.where(window & (kj >= STEPS), alibi, -jnp.inf)
            bias_ref[1, hd] = jnp.where(window, alibi, -jnp.inf)

    lane = lax.broadcasted_iota(jnp.int32, (STEPS, PAIR), 1)
    low = lane < B_HEAD_DIM
    low_keys = lax.broadcasted_iota(jnp.int32, (2 * STEPS, PAIR), 1) < B_HEAD_DIM
    ones_even = jnp.where(low_keys, 1.0, 0.0).astype(BF16)
    ones_odd = jnp.where(low_keys, 0.0, 1.0).astype(BF16)

    def unit(u, carry):
        r = u // blocks
        nl = u % blocks
        has_prev = jnp.logical_or(nl > 0, i % tiles_per_seq != 0)
        variant = has_prev.astype(jnp.int32)
        cur = pl.ds(pl.multiple_of(nl * STEPS, STEPS), STEPS)
        before = pl.ds(pl.multiple_of(jnp.maximum(nl - 1, 0) * STEPS, STEPS), STEPS)
        q = q_ref[0, r, cur, :].astype(F32) * (B_HEAD_DIM ** -0.5)
        k_before = jnp.where(nl > 0, k_ref[0, r, before, :], kprev_ref[0, r])
        v_before = jnp.where(nl > 0, v_ref[0, r, before, :], vprev_ref[0, r])
        kk = jnp.concatenate([k_before, k_ref[0, r, cur, :]], axis=0)
        vv = jnp.concatenate([v_before, v_ref[0, r, cur, :]], axis=0)
        start = nl * (STEPS * dil) + r
        rows = pl.ds(start, STEPS, stride=dil) if dil > 1 else pl.ds(pl.multiple_of(start, STEPS), STEPS)
        for pr in range(B_HEADS // 2):
            cols = slice(pr * PAIR, (pr + 1) * PAIR)
            qp, kp = q[:, cols], kk[:, cols]
            vp = vv[:, cols].astype(F32)
            v_blocks = jnp.concatenate([
                jnp.concatenate([jnp.where(low_keys, vp, 0.0).astype(BF16), ones_even], axis=1),
                jnp.concatenate([jnp.where(low_keys, 0.0, vp).astype(BF16), ones_odd], axis=1)], axis=0)
            probs, maxes = [], []
            for par in range(2):
                qm = jnp.where(low, qp, 0.0) if par == 0 else jnp.where(low, 0.0, qp)
                s = _nt_dot(qm.astype(BF16), kp) + bias_ref[variant, 2 * pr + par]
                m = jnp.max(s, axis=-1, keepdims=True)
                probs.append(jnp.exp(s - m).astype(BF16))
                maxes.append(m)
            o_den = jnp.dot(jnp.concatenate(probs, axis=1), v_blocks, preferred_element_type=F32)
            den = o_den[:, PAIR:]
            o_scr[pr, rows, :] = o_den[:, :PAIR] / den
            lse_scr[pr, rows, :] = jnp.where(low, maxes[0], maxes[1]) + jnp.log(den)
        return carry

    lax.fori_loop(0, dil * blocks, unit, 0, unroll=4)
    for pr in range(B_HEADS // 2):
        cols = slice(pr * PAIR, (pr + 1) * PAIR)
        o_ref[:, cols] = o_scr[pr].astype(o_ref.dtype)
        lse_ref[:, cols] = lse_scr[pr]


def _dilated_attention(pb, bsz, seq, dil):
    tiles, _, per_res, _ = pb.shape
    tm = dil * per_res
    tile_blk = (1, dil, per_res, B_COLS)
    prev_blk = (1, dil, STEPS, B_COLS)
    last = per_res // STEPS - 1
    prev = lambda i: jnp.maximum(i - 1, 0)
    kernel = functools.partial(_dilated_kernel, dil=dil, slopes=_alibi_slopes(B_HEADS),
                               tiles_per_seq=tiles // bsz)
    out_block = pl.BlockSpec((tm, B_COLS), lambda i: (i, 0))
    return pl.pallas_call(
        kernel,
        grid=(tiles,),
        in_specs=[
            pl.BlockSpec(tile_blk, lambda i: (i, 0, 0, 0)),
            pl.BlockSpec(tile_blk, lambda i: (i, 0, 0, 1)),
            pl.BlockSpec(tile_blk, lambda i: (i, 0, 0, 2)),
            pl.BlockSpec(prev_blk, lambda i: (prev(i), 0, last, 1)),
            pl.BlockSpec(prev_blk, lambda i: (prev(i), 0, last, 2)),
        ],
        out_specs=[out_block, out_block],
        out_shape=[jax.ShapeDtypeStruct((bsz * seq, B_COLS), BF16),
                   jax.ShapeDtypeStruct((bsz * seq, B_COLS), F32)],
        scratch_shapes=[pltpu.VMEM((2, B_HEADS, STEPS, 2 * STEPS), F32),
                        pltpu.VMEM((B_HEADS // 2, tm, PAIR), F32), pltpu.VMEM((B_HEADS // 2, tm, PAIR), F32)],
        compiler_params=pltpu.CompilerParams(
            dimension_semantics=("arbitrary",), vmem_limit_bytes=VMEM_LIMIT),
        name=f"dilated_attn_d{dil}",
    )(pb, pb, pb, pb, pb)


MERGE_SLABS = 2


def _merge_kernel(oa_ref, o0_ref, o1_ref, o2_ref, l0_ref, l1_ref, l2_ref, ga_ref, gb_ref, x_ref,
                  wpa_ref, wpb_ref, wo_ref, g_ref, b_ref, out_ref):
    slab = out_ref.shape[0] // MERGE_SLABS
    for part in range(MERGE_SLABS):
        rows = slice(part * slab, (part + 1) * slab)
        l0, l1, l2 = l0_ref[rows, :], l1_ref[rows, :], l2_ref[rows, :]
        mx = jnp.maximum(jnp.maximum(l0, l1), l2)
        e0, e1, e2 = jnp.exp(l0 - mx), jnp.exp(l1 - mx), jnp.exp(l2 - mx)
        ob = (e0 * o0_ref[rows, :].astype(F32) + e1 * o1_ref[rows, :].astype(F32)
              + e2 * o2_ref[rows, :].astype(F32)) / (e0 + e1 + e2)
        ya = jnp.dot(oa_ref[rows, :], wpa_ref[...], preferred_element_type=F32)
        yb = jnp.dot(ob.astype(BF16), wpb_ref[...], preferred_element_type=F32)
        y = ga_ref[rows, :].astype(F32) * ya + gb_ref[rows, :].astype(F32) * yb
        z = ALPHA * x_ref[rows, :] + jnp.dot(y.astype(BF16), wo_ref[...], preferred_element_type=F32)
        out_ref[rows, :] = _layer_norm(z, g_ref[...], b_ref[...])


def _merge(oa, obs, lses, gates, x2, wpa, wpb, wo, g, b, tm):
    m = x2.shape[0]
    row = lambda w: pl.BlockSpec((tm, w), lambda i: (i, 0))
    full = lambda a: pl.BlockSpec(a.shape, lambda i: (0, 0))
    return pl.pallas_call(
        _merge_kernel,
        grid=(m // tm,),
        in_specs=[row(A_COLS), row(B_COLS), row(B_COLS), row(B_COLS), row(B_COLS), row(B_COLS), row(B_COLS),
                  pl.BlockSpec((tm, D_MODEL), lambda i: (i, 0)), pl.BlockSpec((tm, D_MODEL), lambda i: (i, 1)),
                  row(D_MODEL), full(wpa), full(wpb), full(wo), full(g), full(b)],
        out_specs=row(D_MODEL),
        out_shape=jax.ShapeDtypeStruct((m, D_MODEL), F32),
        compiler_params=pltpu.CompilerParams(dimension_semantics=("parallel",), vmem_limit_bytes=VMEM_LIMIT),
        name="merge_ln1",
    )(oa, *obs, *lses, gates, gates, x2, wpa, wpb, wo, g, b)


HALO = BF16_ROWS


def _ffn_kernel(x_ref, halo_ref, wup_ref, wconv_ref, bconv_ref, wd_ref, g_ref, b_ref,
                out_ref, xcat_ref, acc_ref, *, tiles_per_seq):
    i = pl.program_id(0)
    halo = jnp.where(i % tiles_per_seq == 0, 0.0, halo_ref[...])
    xcat_ref[0:HALO, :] = halo.astype(BF16)
    xcat_ref[HALO:, :] = x_ref[...].astype(BF16)
    xcat = xcat_ref[...]

    def conv(col0, width):
        cols = slice(col0, col0 + width)
        hfull = jnp.dot(xcat, wup_ref[:, cols], preferred_element_type=F32)
        out = bconv_ref[:, cols] + wconv_ref[CONV_WIDTH - 1:CONV_WIDTH, cols] * hfull[HALO:, :]
        for back in range(1, CONV_WIDTH):
            shifted = pltpu.roll(hfull, back, axis=0)[HALO:, :]
            out = out + wconv_ref[CONV_WIDTH - 1 - back:CONV_WIDTH - back, cols] * shifted
        return out

    for col0, width in FF_CHUNKS:
        a = conv(col0, width)
        gv = conv(D_FF + col0, width)
        f = 0.5 * a * (1.0 + lax.erf(a * (2.0 ** -0.5))) * gv
        y = jnp.dot(f.astype(BF16), wd_ref[col0:col0 + width, :], preferred_element_type=F32)
        if col0 == 0:
            acc_ref[...] = y
        else:
            acc_ref[...] += y

    z = ALPHA * x_ref[...] + acc_ref[...]
    out_ref[...] = _layer_norm(z, g_ref[...], b_ref[...])


def _ffn(x1, w_up, w_conv, b_conv, w_down, g, b, seq, tm):
    m = x1.shape[0]
    halo_blocks = tm // HALO
    kernel = functools.partial(_ffn_kernel, tiles_per_seq=seq // tm)
    full = lambda a: pl.BlockSpec(a.shape, lambda i: (0, 0))
    return pl.pallas_call(
        kernel,
        grid=(m // tm,),
        in_specs=[
            pl.BlockSpec((tm, D_MODEL), lambda i: (i, 0)),
            pl.BlockSpec((HALO, D_MODEL), lambda i: (jnp.maximum(i * halo_blocks - 1, 0), 0)),
            full(w_up), full(w_conv), full(b_conv), full(w_down), full(g), full(b),
        ],
        out_specs=pl.BlockSpec((tm, D_MODEL), lambda i: (i, 0)),
        out_shape=jax.ShapeDtypeStruct((m, D_MODEL), F32),
        scratch_shapes=[pltpu.VMEM((HALO + tm, D_MODEL), BF16), pltpu.VMEM((tm, D_MODEL), F32)],
        compiler_params=pltpu.CompilerParams(dimension_semantics=("parallel",), vmem_limit_bytes=VMEM_LIMIT),
        name="ffn_ln2",
    )(x1, x1, w_up, w_conv, b_conv, w_down, g, b)


def kernel(x, w_in, b_gate, lambda_q1, lambda_k1, lambda_q2, lambda_k2, subln_w, w_pa, w_pb, w_o, ln1_g, ln1_b,
           w_up, w_conv, b_conv, w_down, ln2_g, ln2_b):
    bsz, seq, d = x.shape
    assert (seq, d) == (4096, D_MODEL) and w_in.shape[0] == DEPTH
    slopes_a = jnp.asarray(_alibi_slopes(A_HEADS))
    for l in range(DEPTH):
        lambda_init = 0.8 - 0.6 * math.exp(-0.3 * l)
        x2 = x.reshape(bsz * seq, d)
        w_wide = jnp.concatenate([w_in[l][:, :3 * A_COLS], w_in[l][:, QKV_COLS:]], axis=1).astype(BF16)
        proj_a, gates, xb = _projection_wide(x2, w_wide, b_gate[l][None, :], tm=PROJ_ROWS)
        w_groups = w_in[l][:, 3 * A_COLS:QKV_COLS].reshape(d, 3, N_GROUPS, B_COLS).transpose(0, 2, 1, 3)
        w_groups = w_groups.reshape(d, 3 * N_GROUPS * B_COLS).astype(BF16)
        proj_groups = _projection_groups(xb, w_groups, tm=PROJ_ROWS)
        oa = _diff_attention(proj_a.reshape(bsz, seq, 3 * A_COLS), slopes_a, lambda_q1[l][None],
                             lambda_k1[l][None], lambda_q2[l][None], lambda_k2[l][None], subln_w[l][:, None],
                             lambda_init, t=512)
        obs, lses = [], []
        for pb, (window, dil) in zip(proj_groups, B_PATTERNS):
            assert window // dil == STEPS
            o, lse = _dilated_attention(pb, bsz, seq, dil)
            obs.append(o)
            lses.append(lse)

        x1 = _merge(oa.reshape(bsz * seq, A_COLS), obs, lses, gates, x2,
                    w_pa[l].astype(BF16), w_pb[l].astype(BF16), w_o[l].astype(BF16),
                    ln1_g[l][None], ln1_b[l][None], tm=512)
        x2 = _ffn(x1, w_up[l].astype(BF16), w_conv[l], b_conv[l][None], w_down[l].astype(BF16),
                  ln2_g[l][None], ln2_b[l][None], seq, tm=512)
        x = x2.reshape(bsz, seq, d)
    return x
```

```python
import functools
import math

import numpy as np
import jax
import jax.numpy as jnp
from jax import lax
from jax.experimental import pallas as pl
from jax.experimental.pallas import tpu as pltpu

BF16 = jnp.bfloat16
F32 = jnp.float32

D_MODEL = 1024
A_HEADS = 8
A_HEAD_DIM = 64
B_PATTERNS = ((128, 1), (512, 4), (2048, 16))
B_HEADS = 8
B_HEAD_DIM = 64
D_FF = 2816
CONV_WIDTH = 3
EPS = 1e-5
DEPTH = 1
ALPHA = (2.0 * DEPTH) ** 0.25

A_COLS = A_HEADS * 2 * A_HEAD_DIM
B_COLS = B_HEADS * B_HEAD_DIM
N_GROUPS = len(B_PATTERNS)
QKV_COLS = 3 * A_COLS + 3 * N_GROUPS * B_COLS
PAIR = 2 * A_HEAD_DIM
STEPS = 128
BF16_ROWS = 16
LOG2E = math.log2(math.e)
BF16_EXACT = 256
SLOPE_TERMS = 3
PROJ_ROWS = 2048
MXU_WIDTH = 256
FF_CHUNKS = ((0, 6 * MXU_WIDTH), (6 * MXU_WIDTH, 5 * MXU_WIDTH))

VMEM_LIMIT = 56 * 1024 * 1024


def _alibi_slopes(n):
    return np.power(np.float32(2.0), -8.0 * (np.arange(n, dtype=np.float32) + 1) / n).astype(np.float32)


def _nt_dot(a, b):
    return lax.dot_general(a, b, (((1,), (1,)), ((), ())), preferred_element_type=F32)


def _layer_norm(z, g, b):
    mu = jnp.mean(z, axis=-1, keepdims=True)
    zc = z - mu
    var = jnp.mean(zc * zc, axis=-1, keepdims=True)
    return zc * lax.rsqrt(var + EPS) * g + b


WIDE_COLS = 1024
A_TILES = 3 * A_COLS // WIDE_COLS
GROUP_TILES = 2
SPLIT = 4


def _proj_wide_kernel(x_ref, w_ref, b_ref, pa_ref, g_ref, xb_ref):
    j = pl.program_id(1)

    @pl.when(j == 0)
    def _():
        xb_ref[...] = x_ref[...].astype(BF16)

    @pl.when(j < A_TILES)
    def _():
        pa_ref[...] = jnp.dot(xb_ref[...], w_ref[...], preferred_element_type=F32).astype(pa_ref.dtype)

    @pl.when(j >= A_TILES)
    def _():
        z = jnp.dot(xb_ref[...], w_ref[...], preferred_element_type=F32) + b_ref[...]
        g_ref[...] = (0.5 * jnp.tanh(0.5 * z) + 0.5).astype(g_ref.dtype)


def _projection_wide(x2, wb, b_gate, tm):
    m, k = x2.shape
    tn = WIDE_COLS
    gate_tiles = b_gate.shape[1] // tn
    clamp = lambda j, first, count: jnp.clip(j - first, 0, count - 1)
    return pl.pallas_call(
        _proj_wide_kernel,
        grid=(m // tm, A_TILES + gate_tiles),
        in_specs=[pl.BlockSpec((tm, k), lambda i, j: (i, 0)),
                  pl.BlockSpec((k, tn), lambda i, j: (0, j)),
                  pl.BlockSpec((1, tn), lambda i, j: (0, clamp(j, A_TILES, gate_tiles)))],
        out_specs=[pl.BlockSpec((tm, tn), lambda i, j: (i, clamp(j, 0, A_TILES))),
                   pl.BlockSpec((tm, tn), lambda i, j: (i, clamp(j, A_TILES, gate_tiles))),
                   pl.BlockSpec((tm, k), lambda i, j: (i, 0))],
        out_shape=[jax.ShapeDtypeStruct((m, A_TILES * tn), BF16),
                   jax.ShapeDtypeStruct((m, gate_tiles * tn), BF16),
                   jax.ShapeDtypeStruct((m, k), BF16)],
        compiler_params=pltpu.CompilerParams(
            dimension_semantics=("parallel", "arbitrary"), vmem_limit_bytes=VMEM_LIMIT),
        name="proj_wide",
    )(x2, wb, b_gate)


def _proj_groups_kernel(x_ref, w_ref, *rest):
    group_refs, (acc_ref, tmp_ref) = rest[:N_GROUPS], rest[N_GROUPS:]
    j = pl.program_id(1)
    lane_tiles, tm, _ = acc_ref.shape
    for g, (o_ref, (_, dil)) in enumerate(zip(group_refs, B_PATTERNS)):
        @pl.when((j >= g * GROUP_TILES) & (j < (g + 1) * GROUP_TILES))
        def _(o_ref=o_ref, dil=dil):
            acc = jnp.dot(x_ref[...], w_ref[...], preferred_element_type=F32)
            if dil == 1:
                o_ref[0, 0] = acc.astype(o_ref.dtype)
                return
            for c in range(lane_tiles):
                acc_ref[c] = acc[:, c * PAIR:(c + 1) * PAIR]
            for c in range(lane_tiles):
                cols = slice(c * PAIR, (c + 1) * PAIR)
                if dil == SPLIT:
                    for r in range(dil):
                        o_ref[0, r, :, cols] = acc_ref[c, pl.ds(r, tm // dil, stride=dil), :].astype(o_ref.dtype)
                else:
                    assert dil == SPLIT * SPLIT
                    part = tm // SPLIT
                    for r in range(SPLIT):
                        tmp_ref[c, r * part:(r + 1) * part, :] = acc_ref[c, pl.ds(r, part, stride=SPLIT), :]
                    for r in range(dil):
                        start = (r % SPLIT) * part + r // SPLIT
                        o_ref[0, r, :, cols] = \
                            tmp_ref[c, pl.ds(start, tm // dil, stride=SPLIT), :].astype(o_ref.dtype)


def _projection_groups(xb, wb, tm):
    m, k = xb.shape
    tn = 3 * B_COLS // GROUP_TILES
    clamp = lambda j, first: jnp.clip(j - first, 0, GROUP_TILES - 1)
    return pl.pallas_call(
        _proj_groups_kernel,
        grid=(m // tm, N_GROUPS * GROUP_TILES),
        in_specs=[pl.BlockSpec((tm, k), lambda i, j: (i, 0)),
                  pl.BlockSpec((k, tn), lambda i, j: (0, j))],
        out_specs=[pl.BlockSpec((1, dil, tm // dil, tn), lambda i, j, g=g: (i, 0, 0, clamp(j, g * GROUP_TILES)))
                   for g, (_, dil) in enumerate(B_PATTERNS)],
        out_shape=[jax.ShapeDtypeStruct((m // tm, dil, tm // dil, GROUP_TILES * tn), BF16)
                   for _, dil in B_PATTERNS],
        scratch_shapes=[pltpu.VMEM((tn // PAIR, tm, PAIR), F32), pltpu.VMEM((tn // PAIR, tm, PAIR), F32)],
        compiler_params=pltpu.CompilerParams(
            dimension_semantics=("parallel", "arbitrary"), vmem_limit_bytes=VMEM_LIMIT),
        name="proj_groups",
    )(xb, wb)


def _diff_attn_kernel(slopes_ref, q_ref, k_ref, v_ref, lq1_ref, lk1_ref, lq2_ref, lk2_ref, subln_ref,
                      o_ref, vt_ref, mask_ref, kaug_ref, sa_ref, sb_ref, ma_ref, mb_ref, acc_ref, *,
                      t, lambda_init):
    slope = slopes_ref[pl.program_id(1)] * LOG2E
    nchunks = vt_ref.shape[0]

    def transpose_v(c, carry):
        r0 = pl.multiple_of(c * t, t)
        vt_ref[c, 0:PAIR, :] = v_ref[0, pl.ds(r0, t), :].astype(F32).T.astype(BF16)
        vt_ref[c, PAIR:, :] = jnp.ones((BF16_ROWS, t), BF16)
        return carry
    lax.fori_loop(0, nchunks, transpose_v, 0)
    krow = lax.broadcasted_iota(jnp.int32, (t, t), 0)
    qcol = lax.broadcasted_iota(jnp.int32, (t, t), 1)
    mask_ref[...] = jnp.where(krow <= qcol, 0.0, -jnp.inf)

    r = lax.broadcasted_iota(jnp.int32, (t, PAIR), 0)
    c = lax.broadcasted_iota(jnp.int32, (t, PAIR), 1)
    r_low = (r % BF16_EXACT).astype(F32)
    r_high = (r - r % BF16_EXACT).astype(F32)
    kaug_ref[...] = jnp.where(c < SLOPE_TERMS, r_low, jnp.where(c < 2 * SLOPE_TERMS, r_high, 0.0)).astype(BF16)
    rest = jnp.full((PAIR, t), slope, F32)
    wrow = lax.broadcasted_iota(jnp.int32, (PAIR, t), 0)
    waug = jnp.zeros((PAIR, t), F32)
    for term in range(SLOPE_TERMS):
        part = rest.astype(BF16).astype(F32)
        waug = jnp.where((wrow == term) | (wrow == SLOPE_TERMS + term), part, waug)
        rest = rest - part
    waug = waug.astype(BF16)
    dim = lax.broadcasted_iota(jnp.int32, (PAIR, t), 0)

    lam = (jnp.exp(jnp.sum(lq1_ref[...] * lk1_ref[...], axis=-1, keepdims=True))
           - jnp.exp(jnp.sum(lq2_ref[...] * lk2_ref[...], axis=-1, keepdims=True))
           + lambda_init)
    neg = jnp.full((1, t), -jnp.inf, F32)
    half = t // 2
    s_refs = ((sa_ref, ma_ref), (sb_ref, mb_ref))

    for qi in range(nchunks):
        rows = slice(qi * t, (qi + 1) * t)
        acc = acc_ref.at[qi % 2]
        qt = q_ref[0, rows, :].astype(F32).T * (A_HEAD_DIM ** -0.5 * LOG2E)
        qt_maps = (jnp.concatenate([jnp.where(dim < A_HEAD_DIM, qt, 0.0).astype(BF16), waug], axis=0),
                   jnp.concatenate([jnp.where(dim < A_HEAD_DIM, 0.0, qt).astype(BF16), waug], axis=0))
        acc[...] = jnp.zeros(acc.shape, F32)

        def scores(j, buf, qt_maps=qt_maps):
            s_ref, max_ref = buf
            kc = k_ref[0, pl.ds(pl.multiple_of(j * t, t), t), :]
            kc = jnp.concatenate([kc, kaug_ref[...]], axis=1)
            for mp in range(2):
                s = jnp.dot(kc, qt_maps[mp], preferred_element_type=F32)
                s_ref[mp] = s
                max_ref[mp] = jnp.max(s, axis=0, keepdims=True)

        def update(j, buf, stats, qi=qi, acc=acc):
            s_ref, max_ref = buf
            vt = vt_ref[j]
            shift = slope * jnp.asarray((j - qi) * t, F32)
            out = []
            for mp in range(2):
                m_new = jnp.maximum(stats[mp], max_ref[mp] + shift)
                p = jnp.exp2(s_ref[mp] - (m_new - shift)).astype(BF16)
                alpha = jnp.exp2(stats[mp] - m_new)
                acc[mp] = alpha * acc[mp] + jnp.dot(vt, p, preferred_element_type=F32)
                out.append(m_new)
            return tuple(out)

        def scores_diag(buf, qi=qi, qt_maps=qt_maps):
            s_ref, max_ref = buf
            k_lo = jnp.concatenate([k_ref[0, qi * t:qi * t + half, :], kaug_ref[0:half, :]], axis=1)
            k_hi = jnp.concatenate([k_ref[0, qi * t + half:(qi + 1) * t, :], kaug_ref[half:, :]], axis=1)
            for mp in range(2):
                s_lo = jnp.dot(k_lo, qt_maps[mp], preferred_element_type=F32) + mask_ref[0:half, :]
                s_hi = jnp.dot(k_hi, qt_maps[mp][:, half:], preferred_element_type=F32) \
                    + mask_ref[half:, half:]
                s_ref[mp, 0:half, :] = s_lo
                s_ref[mp, half:, half:] = s_hi
                max_lo = jnp.max(s_lo, axis=0, keepdims=True)
                max_ref[mp, :, 0:half] = max_lo[:, :half]
                max_ref[mp, :, half:] = jnp.maximum(max_lo[:, half:], jnp.max(s_hi, axis=0, keepdims=True))

        def update_diag(buf, stats, qi=qi, acc=acc):
            s_ref, max_ref = buf
            for mp in range(2):
                m_new = jnp.maximum(stats[mp], max_ref[mp])
                p_lo = jnp.exp2(s_ref[mp, 0:half, :] - m_new).astype(BF16)
                p_hi = jnp.exp2(s_ref[mp, half:, half:] - m_new[:, half:]).astype(BF16)
                alpha = jnp.exp2(stats[mp] - m_new)
                pv = jnp.dot(vt_ref[qi, :, 0:half], p_lo, preferred_element_type=F32)
                pv_hi = jnp.dot(vt_ref[qi, :, half:], p_hi, preferred_element_type=F32)
                acc[mp, :, 0:half] = alpha[:, :half] * acc[mp, :, 0:half] + pv[:, :half]
                acc[mp, :, half:] = alpha[:, half:] * acc[mp, :, half:] + (pv[:, half:] + pv_hi)

        first, second = s_refs

        def pair(i, stats, scores=scores, update=update, first=first, second=second):
            j = 2 * i
            scores(j + 1, second)
            stats = update(j, first, stats)
            scores(j + 2, first)
            return update(j + 1, second, stats)

        idx = jnp.int32
        if qi == 0:
            scores_diag(first)
            update_diag(first, (neg, neg))
        else:
            scores(idx(0), first)
            stats = lax.fori_loop(0, (qi - 1) // 2, pair, (neg, neg))
            if qi % 2 == 1:
                scores_diag(second)
                update_diag(second, update(idx(qi - 1), first, stats))
            else:
                scores(idx(qi - 1), second)
                stats = update(idx(qi - 2), first, stats)
                scores_diag(first)
                update_diag(first, update(idx(qi - 1), second, stats))
        if qi % 2 == 0:
            s_refs = (second, first)

        a1, a2 = acc[0], acc[1]
        ot = a1[:PAIR] / a1[PAIR:PAIR + 1] - lam * (a2[:PAIR] / a2[PAIR:PAIR + 1])
        ot = ot * lax.rsqrt(jnp.mean(ot * ot, axis=0, keepdims=True) + EPS)
        ot = ot * subln_ref[...] * (1.0 - lambda_init)
        o_ref[0, rows, :] = ot.T.astype(o_ref.dtype)


def _diff_attention(p3, slopes, lq1, lk1, lq2, lk2, subln_col, lambda_init, t):
    bsz, seq, _ = p3.shape
    vec = lambda n: pl.BlockSpec((1, n), lambda b, h: (0, 0))
    kernel = functools.partial(_diff_attn_kernel, t=t, lambda_init=lambda_init)
    return pl.pallas_call(
        kernel,
        grid=(bsz, A_HEADS),
        in_specs=[
            pl.BlockSpec(memory_space=pltpu.SMEM),
            pl.BlockSpec((1, seq, PAIR), lambda b, h: (b, 0, h)),
            pl.BlockSpec((1, seq, PAIR), lambda b, h: (b, 0, A_HEADS + h)),
            pl.BlockSpec((1, seq, PAIR), lambda b, h: (b, 0, 2 * A_HEADS + h)),
            vec(A_HEAD_DIM), vec(A_HEAD_DIM), vec(A_HEAD_DIM), vec(A_HEAD_DIM),
            pl.BlockSpec((PAIR, 1), lambda b, h: (0, 0)),
        ],
        out_specs=pl.BlockSpec((1, seq, PAIR), lambda b, h: (b, 0, h)),
        out_shape=jax.ShapeDtypeStruct((bsz, seq, A_COLS), BF16),
        scratch_shapes=[pltpu.VMEM((seq // t, PAIR + BF16_ROWS, t), BF16), pltpu.VMEM((t, t), F32),
                        pltpu.VMEM((t, PAIR), BF16),
                        pltpu.VMEM((2, t, t), F32), pltpu.VMEM((2, t, t), F32),
                        pltpu.VMEM((2, 1, t), F32), pltpu.VMEM((2, 1, t), F32),
                        pltpu.VMEM((2, 2, PAIR + BF16_ROWS, t), F32)],
        compiler_params=pltpu.CompilerParams(
            dimension_semantics=("parallel", "parallel"), vmem_limit_bytes=VMEM_LIMIT),
        name="diff_attn",
    )(slopes, p3, p3, p3, lq1, lk1, lq2, lk2, subln_col)


def _dilated_kernel(q_ref, k_ref, v_ref, kprev_ref, vprev_ref, o_ref, lse_ref, bias_ref, o_scr, lse_scr, *,
                    dil, slopes, tiles_per_seq):
    i = pl.program_id(0)
    blocks = q_ref.shape[2] // STEPS

    @pl.when(i == 0)
    def _():
        qi = lax.broadcasted_iota(jnp.int32, (STEPS, 2 * STEPS), 0)
        kj = lax.broadcasted_iota(jnp.int32, (STEPS, 2 * STEPS), 1)
        step = qi + STEPS - kj
        window = (step >= 0) & (step <= STEPS)
        dist = (step * dil).astype(F32)
        for hd in range(B_HEADS):
            alibi = -float(slopes[hd]) * dist
            bias_ref[0, hd] = jnp.where(window & (kj >= STEPS), alibi, -jnp.inf)
            bias_ref[1, hd] = jnp.where(window, alibi, -jnp.inf)

    lane = lax.broadcasted_iota(jnp.int32, (STEPS, PAIR), 1)
    low = lane < B_HEAD_DIM
    low_keys = lax.broadcasted_iota(jnp.int32, (2 * STEPS, PAIR), 1) < B_HEAD_DIM
    ones_even = jnp.where(low_keys, 1.0, 0.0).astype(BF16)
    ones_odd = jnp.where(low_keys, 0.0, 1.0).astype(BF16)

    first_variant = (i % tiles_per_seq != 0).astype(jnp.int32)
    for u in range(dil * blocks):
        r, nl = divmod(u, blocks)
        variant = 1 if nl > 0 else first_variant
        cur = slice(nl * STEPS, (nl + 1) * STEPS)
        before = slice((nl - 1) * STEPS, nl * STEPS)
        q = q_ref[0, r, cur, :].astype(F32) * (B_HEAD_DIM ** -0.5)
        k_before = k_ref[0, r, before, :] if nl > 0 else kprev_ref[0, r]
        v_before = v_ref[0, r, before, :] if nl > 0 else vprev_ref[0, r]
        kk = jnp.concatenate([k_before, k_ref[0, r, cur, :]], axis=0)
        vv = jnp.concatenate([v_before, v_ref[0, r, cur, :]], axis=0)
        start = nl * (STEPS * dil) + r
        rows = pl.ds(start, STEPS, stride=dil) if dil > 1 else pl.ds(start, STEPS)
        for pr in range(B_HEADS // 2):
            cols = slice(pr * PAIR, (pr + 1) * PAIR)
            qp, kp = q[:, cols], kk[:, cols]
            vp = vv[:, cols].astype(F32)
            v_blocks = jnp.concatenate([
                jnp.concatenate([jnp.where(low_keys, vp, 0.0).astype(BF16), ones_even], axis=1),
                jnp.concatenate([jnp.where(low_keys, 0.0, vp).astype(BF16), ones_odd], axis=1)], axis=0)
            probs, maxes = [], []
            for par in range(2):
                qm = jnp.where(low, qp, 0.0) if par == 0 else jnp.where(low, 0.0, qp)
                s = _nt_dot(qm.astype(BF16), kp) + bias_ref[variant, 2 * pr + par]
                m = jnp.max(s, axis=-1, keepdims=True)
                probs.append(jnp.exp(s - m).astype(BF16))
                maxes.append(m)
            o_den = jnp.dot(jnp.concatenate(probs, axis=1), v_blocks, preferred_element_type=F32)
            den = o_den[:, PAIR:]
            o_scr[pr, rows, :] = o_den[:, :PAIR] / den
            lse_scr[pr, rows, :] = jnp.where(low, maxes[0], maxes[1]) + jnp.log(den)

    for pr in range(B_HEADS // 2):
        cols = slice(pr * PAIR, (pr + 1) * PAIR)
        o_ref[:, cols] = o_scr[pr].astype(o_ref.dtype)
        lse_ref[:, cols] = lse_scr[pr]


def _dilated_attention(pb, bsz, seq, dil):
    tiles, _, per_res, _ = pb.shape
    tm = dil * per_res
    tile_blk = (1, dil, per_res, B_COLS)
    prev_blk = (1, dil, STEPS, B_COLS)
    last = per_res // STEPS - 1
    prev = lambda i: jnp.maximum(i - 1, 0)
    kernel = functools.partial(_dilated_kernel, dil=dil, slopes=_alibi_slopes(B_HEADS),
                               tiles_per_seq=tiles // bsz)
    out_block = pl.BlockSpec((tm, B_COLS), lambda i: (i, 0))
    return pl.pallas_call(
        kernel,
        grid=(tiles,),
        in_specs=[
            pl.BlockSpec(tile_blk, lambda i: (i, 0, 0, 0)),
            pl.BlockSpec(tile_blk, lambda i: (i, 0, 0, 1)),
            pl.BlockSpec(tile_blk, lambda i: (i, 0, 0, 2)),
            pl.BlockSpec(prev_blk, lambda i: (prev(i), 0, last, 1)),
            pl.BlockSpec(prev_blk, lambda i: (prev(i), 0, last, 2)),
        ],
        out_specs=[out_block, out_block],
        out_shape=[jax.ShapeDtypeStruct((bsz * seq, B_COLS), BF16),
                   jax.ShapeDtypeStruct((bsz * seq, B_COLS), F32)],
        scratch_shapes=[pltpu.VMEM((2, B_HEADS, STEPS, 2 * STEPS), F32),
                        pltpu.VMEM((B_HEADS // 2, tm, PAIR), F32), pltpu.VMEM((B_HEADS // 2, tm, PAIR), F32)],
        compiler_params=pltpu.CompilerParams(
            dimension_semantics=("arbitrary",), vmem_limit_bytes=VMEM_LIMIT),
        name=f"dilated_attn_d{dil}",
    )(pb, pb, pb, pb, pb)


MERGE_SLABS = 2


def _merge_kernel(oa_ref, o0_ref, o1_ref, o2_ref, l0_ref, l1_ref, l2_ref, ga_ref, gb_ref, x_ref,
                  wpa_ref, wpb_ref, wo_ref, g_ref, b_ref, out_ref):
    slab = out_ref.shape[0] // MERGE_SLABS
    for part in range(MERGE_SLABS):
        rows = slice(part * slab, (part + 1) * slab)
        l0, l1, l2 = l0_ref[rows, :], l1_ref[rows, :], l2_ref[rows, :]
        mx = jnp.maximum(jnp.maximum(l0, l1), l2)
        e0, e1, e2 = jnp.exp(l0 - mx), jnp.exp(l1 - mx), jnp.exp(l2 - mx)
        ob = (e0 * o0_ref[rows, :].astype(F32) + e1 * o1_ref[rows, :].astype(F32)
              + e2 * o2_ref[rows, :].astype(F32)) / (e0 + e1 + e2)
        ya = jnp.dot(oa_ref[rows, :], wpa_ref[...], preferred_element_type=F32)
        yb = jnp.dot(ob.astype(BF16), wpb_ref[...], preferred_element_type=F32)
        y = ga_ref[rows, :].astype(F32) * ya + gb_ref[rows, :].astype(F32) * yb
        z = ALPHA * x_ref[rows, :] + jnp.dot(y.astype(BF16), wo_ref[...], preferred_element_type=F32)
        out_ref[rows, :] = _layer_norm(z, g_ref[...], b_ref[...])


def _merge(oa, obs, lses, gates, x2, wpa, wpb, wo, g, b, tm):
    m = x2.shape[0]
    row = lambda w: pl.BlockSpec((tm, w), lambda i: (i, 0))
    full = lambda a: pl.BlockSpec(a.shape, lambda i: (0, 0))
    return pl.pallas_call(
        _merge_kernel,
        grid=(m // tm,),
        in_specs=[row(A_COLS), row(B_COLS), row(B_COLS), row(B_COLS), row(B_COLS), row(B_COLS), row(B_COLS),
                  pl.BlockSpec((tm, D_MODEL), lambda i: (i, 0)), pl.BlockSpec((tm, D_MODEL), lambda i: (i, 1)),
                  row(D_MODEL), full(wpa), full(wpb), full(wo), full(g), full(b)],
        out_specs=row(D_MODEL),
        out_shape=jax.ShapeDtypeStruct((m, D_MODEL), F32),
        compiler_params=pltpu.CompilerParams(dimension_semantics=("parallel",), vmem_limit_bytes=VMEM_LIMIT),
        name="merge_ln1",
    )(oa, *obs, *lses, gates, gates, x2, wpa, wpb, wo, g, b)


HALO = BF16_ROWS


def _ffn_kernel(x_ref, halo_ref, wup_ref, wconv_ref, bconv_ref, wd_ref, g_ref, b_ref,
                out_ref, xcat_ref, acc_ref, *, tiles_per_seq):
    i = pl.program_id(0)
    halo = jnp.where(i % tiles_per_seq == 0, 0.0, halo_ref[...])
    xcat_ref[0:HALO, :] = halo.astype(BF16)
    xcat_ref[HALO:, :] = x_ref[...].astype(BF16)
    xcat = xcat_ref[...]

    def conv(col0, width):
        cols = slice(col0, col0 + width)
        hfull = jnp.dot(xcat, wup_ref[:, cols], preferred_element_type=F32)
        out = bconv_ref[:, cols] + wconv_ref[CONV_WIDTH - 1:CONV_WIDTH, cols] * hfull[HALO:, :]
        for back in range(1, CONV_WIDTH):
            shifted = pltpu.roll(hfull, back, axis=0)[HALO:, :]
            out = out + wconv_ref[CONV_WIDTH - 1 - back:CONV_WIDTH - back, cols] * shifted
        return out

    for col0, width in FF_CHUNKS:
        a = conv(col0, width)
        gv = conv(D_FF + col0, width)
        f = 0.5 * a * (1.0 + lax.erf(a * (2.0 ** -0.5))) * gv
        y = jnp.dot(f.astype(BF16), wd_ref[col0:col0 + width, :], preferred_element_type=F32)
        if col0 == 0:
            acc_ref[...] = y
        else:
            acc_ref[...] += y

    z = ALPHA * x_ref[...] + acc_ref[...]
    out_ref[...] = _layer_norm(z, g_ref[...], b_ref[...])


def _ffn(x1, w_up, w_conv, b_conv, w_down, g, b, seq, tm):
    m = x1.shape[0]
    halo_blocks = tm // HALO
    kernel = functools.partial(_ffn_kernel, tiles_per_seq=seq // tm)
    full = lambda a: pl.BlockSpec(a.shape, lambda i: (0, 0))
    return pl.pallas_call(
        kernel,
        grid=(m // tm,),
        in_specs=[
            pl.BlockSpec((tm, D_MODEL), lambda i: (i, 0)),
            pl.BlockSpec((HALO, D_MODEL), lambda i: (jnp.maximum(i * halo_blocks - 1, 0), 0)),
            full(w_up), full(w_conv), full(b_conv), full(w_down), full(g), full(b),
        ],
        out_specs=pl.BlockSpec((tm, D_MODEL), lambda i: (i, 0)),
        out_shape=jax.ShapeDtypeStruct((m, D_MODEL), F32),
        scratch_shapes=[pltpu.VMEM((HALO + tm, D_MODEL), BF16), pltpu.VMEM((tm, D_MODEL), F32)],
        compiler_params=pltpu.CompilerParams(dimension_semantics=("parallel",), vmem_limit_bytes=VMEM_LIMIT),
        name="ffn_ln2",
    )(x1, x1, w_up, w_conv, b_conv, w_down, g, b)


def kernel(x, w_in, b_gate, lambda_q1, lambda_k1, lambda_q2, lambda_k2, subln_w, w_pa, w_pb, w_o, ln1_g, ln1_b,
           w_up, w_conv, b_conv, w_down, ln2_g, ln2_b):
    bsz, seq, d = x.shape
    assert (seq, d) == (4096, D_MODEL) and w_in.shape[0] == DEPTH
    slopes_a = jnp.asarray(_alibi_slopes(A_HEADS))
    for l in range(DEPTH):
        lambda_init = 0.8 - 0.6 * math.exp(-0.3 * l)
        x2 = x.reshape(bsz * seq, d)
        w_wide = jnp.concatenate([w_in[l][:, :3 * A_COLS], w_in[l][:, QKV_COLS:]], axis=1).astype(BF16)
        proj_a, gates, xb = _projection_wide(x2, w_wide, b_gate[l][None, :], tm=PROJ_ROWS)
        group_cols = [3 * A_COLS + (part * N_GROUPS + g) * B_COLS for g in range(N_GROUPS) for part in range(3)]
        w_groups = jnp.concatenate([w_in[l][:, s:s + B_COLS] for s in group_cols], axis=1).astype(BF16)
        proj_groups = _projection_groups(xb, w_groups, tm=PROJ_ROWS)
        oa = _diff_attention(proj_a.reshape(bsz, seq, 3 * A_COLS), slopes_a, lambda_q1[l][None],
                             lambda_k1[l][None], lambda_q2[l][None], lambda_k2[l][None], subln_w[l][:, None],
                             lambda_init, t=512)
        obs, lses = [], []
        for pb, (window, dil) in zip(proj_groups, B_PATTERNS):
            assert window // dil == STEPS
            o, lse = _dilated_attention(pb, bsz, seq, dil)
            obs.append(o)
            lses.append(lse)

        x1 = _merge(oa.reshape(bsz * seq, A_COLS), obs, lses, gates, x2,
                    w_pa[l].astype(BF16), w_pb[l].astype(BF16), w_o[l].astype(BF16),
                    ln1_g[l][None], ln1_b[l][None], tm=512)
        x2 = _ffn(x1, w_up[l].astype(BF16), w_conv[l], b_conv[l][None], w_down[l].astype(BF16),
                  ln2_g[l][None], ln2_b[l][None], seq, tm=512)
        x = x2.reshape(bsz, seq, d)
    return x
```

```python
import functools
import math

import numpy as np
import jax
import jax.numpy as jnp
from jax import lax
from jax.experimental import pallas as pl
from jax.experimental.pallas import tpu as pltpu

BF16 = jnp.bfloat16
F32 = jnp.float32

D_MODEL = 1024
A_HEADS = 8
A_HEAD_DIM = 64
B_PATTERNS = ((128, 1), (512, 4), (2048, 16))
B_HEADS = 8
B_HEAD_DIM = 64
D_FF = 2816
CONV_WIDTH = 3
EPS = 1e-5
DEPTH = 1
ALPHA = (2.0 * DEPTH) ** 0.25

A_COLS = A_HEADS * 2 * A_HEAD_DIM
B_COLS = B_HEADS * B_HEAD_DIM
N_GROUPS = len(B_PATTERNS)
QKV_COLS = 3 * A_COLS + 3 * N_GROUPS * B_COLS
PAIR = 2 * A_HEAD_DIM
STEPS = 128
BF16_ROWS = 16
LOG2E = math.log2(math.e)
BF16_EXACT = 256
SLOPE_TERMS = 3
PROJ_ROWS = 2048
MXU_WIDTH = 256
FF_CHUNKS = ((0, 6 * MXU_WIDTH), (6 * MXU_WIDTH, 5 * MXU_WIDTH))

VMEM_LIMIT = 56 * 1024 * 1024


def _alibi_slopes(n):
    return np.power(np.float32(2.0), -8.0 * (np.arange(n, dtype=np.float32) + 1) / n).astype(np.float32)


def _nt_dot(a, b):
    return lax.dot_general(a, b, (((1,), (1,)), ((), ())), preferred_element_type=F32)


def _layer_norm(z, g, b):
    mu = jnp.mean(z, axis=-1, keepdims=True)
    zc = z - mu
    var = jnp.mean(zc * zc, axis=-1, keepdims=True)
    return zc * lax.rsqrt(var + EPS) * g + b


WIDE_COLS = 1024
A_TILES = 3 * A_COLS // WIDE_COLS
GROUP_TILES = 3
SPLIT = 4


def _proj_wide_kernel(x_ref, w_ref, b_ref, pa_ref, g_ref, xb_ref):
    j = pl.program_id(1)

    @pl.when(j == 0)
    def _():
        xb_ref[...] = x_ref[...].astype(BF16)

    @pl.when(j < A_TILES)
    def _():
        pa_ref[...] = jnp.dot(xb_ref[...], w_ref[...], preferred_element_type=F32).astype(pa_ref.dtype)

    @pl.when(j >= A_TILES)
    def _():
        z = jnp.dot(xb_ref[...], w_ref[...], preferred_element_type=F32) + b_ref[...]
        g_ref[...] = (0.5 * jnp.tanh(0.5 * z) + 0.5).astype(g_ref.dtype)


def _projection_wide(x2, wb, b_gate, tm):
    m, k = x2.shape
    tn = WIDE_COLS
    gate_tiles = b_gate.shape[1] // tn
    clamp = lambda j, first, count: jnp.clip(j - first, 0, count - 1)
    return pl.pallas_call(
        _proj_wide_kernel,
        grid=(m // tm, A_TILES + gate_tiles),
        in_specs=[pl.BlockSpec((tm, k), lambda i, j: (i, 0)),
                  pl.BlockSpec((k, tn), lambda i, j: (0, j)),
                  pl.BlockSpec((1, tn), lambda i, j: (0, clamp(j, A_TILES, gate_tiles)))],
        out_specs=[pl.BlockSpec((tm, tn), lambda i, j: (i, clamp(j, 0, A_TILES))),
                   pl.BlockSpec((tm, tn), lambda i, j: (i, clamp(j, A_TILES, gate_tiles))),
                   pl.BlockSpec((tm, k), lambda i, j: (i, 0))],
        out_shape=[jax.ShapeDtypeStruct((m, A_TILES * tn), BF16),
                   jax.ShapeDtypeStruct((m, gate_tiles * tn), BF16),
                   jax.ShapeDtypeStruct((m, k), BF16)],
        compiler_params=pltpu.CompilerParams(
            dimension_semantics=("parallel", "arbitrary"), vmem_limit_bytes=VMEM_LIMIT),
        name="proj_wide",
    )(x2, wb, b_gate)


def _proj_groups_kernel(x_ref, w_ref, *rest):
    group_refs, (acc_ref, tmp_ref) = rest[:N_GROUPS], rest[N_GROUPS:]
    j = pl.program_id(1)
    lane_tiles, tm, _ = acc_ref.shape
    for g, (o_ref, (_, dil)) in enumerate(zip(group_refs, B_PATTERNS)):
        @pl.when((j >= g * GROUP_TILES) & (j < (g + 1) * GROUP_TILES))
        def _(o_ref=o_ref, dil=dil):
            acc = jnp.dot(x_ref[...], w_ref[...].astype(BF16), preferred_element_type=F32)
            if dil == 1:
                o_ref[0, 0] = acc.astype(o_ref.dtype)
                return
            for c in range(lane_tiles):
                acc_ref[c] = acc[:, c * PAIR:(c + 1) * PAIR]
            for c in range(lane_tiles):
                cols = slice(c * PAIR, (c + 1) * PAIR)
                if dil == SPLIT:
                    for r in range(dil):
                        o_ref[0, r, :, cols] = acc_ref[c, pl.ds(r, tm // dil, stride=dil), :].astype(o_ref.dtype)
                else:
                    assert dil == SPLIT * SPLIT
                    part = tm // SPLIT
                    for r in range(SPLIT):
                        tmp_ref[c, r * part:(r + 1) * part, :] = acc_ref[c, pl.ds(r, part, stride=SPLIT), :]
                    for r in range(dil):
                        start = (r % SPLIT) * part + r // SPLIT
                        o_ref[0, r, :, cols] = \
                            tmp_ref[c, pl.ds(start, tm // dil, stride=SPLIT), :].astype(o_ref.dtype)


def _projection_groups(xb, w_in, tm):
    m, k = xb.shape
    tn = B_COLS
    clamp = lambda j, first: jnp.clip(j - first, 0, GROUP_TILES - 1)
    w_block = lambda j: 3 * A_COLS // tn + (j % GROUP_TILES) * N_GROUPS + j // GROUP_TILES
    return pl.pallas_call(
        _proj_groups_kernel,
        grid=(m // tm, N_GROUPS * GROUP_TILES),
        in_specs=[pl.BlockSpec((tm, k), lambda i, j: (i, 0)),
                  pl.BlockSpec((k, tn), lambda i, j: (0, w_block(j)))],
        out_specs=[pl.BlockSpec((1, dil, tm // dil, tn), lambda i, j, g=g: (i, 0, 0, clamp(j, g * GROUP_TILES)))
                   for g, (_, dil) in enumerate(B_PATTERNS)],
        out_shape=[jax.ShapeDtypeStruct((m // tm, dil, tm // dil, GROUP_TILES * tn), BF16)
                   for _, dil in B_PATTERNS],
        scratch_shapes=[pltpu.VMEM((tn // PAIR, tm, PAIR), F32), pltpu.VMEM((tn // PAIR, tm, PAIR), F32)],
        compiler_params=pltpu.CompilerParams(
            dimension_semantics=("parallel", "arbitrary"), vmem_limit_bytes=VMEM_LIMIT),
        name="proj_groups",
    )(xb, w_in)


def _diff_attn_kernel(slopes_ref, q_ref, k_ref, v_ref, lq1_ref, lk1_ref, lq2_ref, lk2_ref, subln_ref,
                      o_ref, vt_ref, mask_ref, kaug_ref, sa_ref, sb_ref, ma_ref, mb_ref, acc_ref, *,
                      t, lambda_init):
    slope = slopes_ref[pl.program_id(1)] * LOG2E
    nchunks = vt_ref.shape[0]

    def transpose_v(c, carry):
        r0 = pl.multiple_of(c * t, t)
        vt_ref[c, 0:PAIR, :] = v_ref[0, pl.ds(r0, t), :].astype(F32).T.astype(BF16)
        vt_ref[c, PAIR:, :] = jnp.ones((BF16_ROWS, t), BF16)
        return carry
    lax.fori_loop(0, nchunks, transpose_v, 0)
    krow = lax.broadcasted_iota(jnp.int32, (t, t), 0)
    qcol = lax.broadcasted_iota(jnp.int32, (t, t), 1)
    mask_ref[...] = jnp.where(krow <= qcol, 0.0, -jnp.inf)

    r = lax.broadcasted_iota(jnp.int32, (t, PAIR), 0)
    c = lax.broadcasted_iota(jnp.int32, (t, PAIR), 1)
    r_low = (r % BF16_EXACT).astype(F32)
    r_high = (r - r % BF16_EXACT).astype(F32)
    kaug_ref[...] = jnp.where(c < SLOPE_TERMS, r_low, jnp.where(c < 2 * SLOPE_TERMS, r_high, 0.0)).astype(BF16)
    rest = jnp.full((PAIR, t), slope, F32)
    wrow = lax.broadcasted_iota(jnp.int32, (PAIR, t), 0)
    waug = jnp.zeros((PAIR, t), F32)
    for term in range(SLOPE_TERMS):
        part = rest.astype(BF16).astype(F32)
        waug = jnp.where((wrow == term) | (wrow == SLOPE_TERMS + term), part, waug)
        rest = rest - part
    waug = waug.astype(BF16)
    dim = lax.broadcasted_iota(jnp.int32, (PAIR, t), 0)

    lam = (jnp.exp(jnp.sum(lq1_ref[...] * lk1_ref[...], axis=-1, keepdims=True))
           - jnp.exp(jnp.sum(lq2_ref[...] * lk2_ref[...], axis=-1, keepdims=True))
           + lambda_init)
    neg = jnp.full((1, t), -jnp.inf, F32)
    half = t // 2
    s_refs = ((sa_ref, ma_ref), (sb_ref, mb_ref))

    for qi in range(nchunks):
        rows = slice(qi * t, (qi + 1) * t)
        acc = acc_ref.at[qi % 2]
        qt = q_ref[0, rows, :].astype(F32).T * (A_HEAD_DIM ** -0.5 * LOG2E)
        qt_maps = (jnp.concatenate([jnp.where(dim < A_HEAD_DIM, qt, 0.0).astype(BF16), waug], axis=0),
                   jnp.concatenate([jnp.where(dim < A_HEAD_DIM, 0.0, qt).astype(BF16), waug], axis=0))
        acc[...] = jnp.zeros(acc.shape, F32)

        def scores(j, buf, qt_maps=qt_maps):
            s_ref, max_ref = buf
            kc = k_ref[0, pl.ds(pl.multiple_of(j * t, t), t), :]
            kc = jnp.concatenate([kc, kaug_ref[...]], axis=1)
            for mp in range(2):
                s = jnp.dot(kc, qt_maps[mp], preferred_element_type=F32)
                s_ref[mp] = s
                max_ref[mp] = jnp.max(s, axis=0, keepdims=True)

        def update(j, buf, stats, qi=qi, acc=acc):
            s_ref, max_ref = buf
            vt = vt_ref[j]
            shift = slope * jnp.asarray((j - qi) * t, F32)
            out = []
            for mp in range(2):
                m_new = jnp.maximum(stats[mp], max_ref[mp] + shift)
                p = jnp.exp2(s_ref[mp] - (m_new - shift)).astype(BF16)
                alpha = jnp.exp2(stats[mp] - m_new)
                acc[mp] = alpha * acc[mp] + jnp.dot(vt, p, preferred_element_type=F32)
                out.append(m_new)
            return tuple(out)

        def scores_diag(buf, qi=qi, qt_maps=qt_maps):
            s_ref, max_ref = buf
            k_lo = jnp.concatenate([k_ref[0, qi * t:qi * t + half, :], kaug_ref[0:half, :]], axis=1)
            k_hi = jnp.concatenate([k_ref[0, qi * t + half:(qi + 1) * t, :], kaug_ref[half:, :]], axis=1)
            for mp in range(2):
                s_lo = jnp.dot(k_lo, qt_maps[mp], preferred_element_type=F32) + mask_ref[0:half, :]
                s_hi = jnp.dot(k_hi, qt_maps[mp][:, half:], preferred_element_type=F32) \
                    + mask_ref[half:, half:]
                s_ref[mp, 0:half, :] = s_lo
                s_ref[mp, half:, half:] = s_hi
                max_lo = jnp.max(s_lo, axis=0, keepdims=True)
                max_ref[mp, :, 0:half] = max_lo[:, :half]
                max_ref[mp, :, half:] = jnp.maximum(max_lo[:, half:], jnp.max(s_hi, axis=0, keepdims=True))

        def update_diag(buf, stats, qi=qi, acc=acc):
            s_ref, max_ref = buf
            for mp in range(2):
                m_new = jnp.maximum(stats[mp], max_ref[mp])
                p_lo = jnp.exp2(s_ref[mp, 0:half, :] - m_new).astype(BF16)
                p_hi = jnp.exp2(s_ref[mp, half:, half:] - m_new[:, half:]).astype(BF16)
                alpha = jnp.exp2(stats[mp] - m_new)
                pv = jnp.dot(vt_ref[qi, :, 0:half], p_lo, preferred_element_type=F32)
                pv_hi = jnp.dot(vt_ref[qi, :, half:], p_hi, preferred_element_type=F32)
                acc[mp, :, 0:half] = alpha[:, :half] * acc[mp, :, 0:half] + pv[:, :half]
                acc[mp, :, half:] = alpha[:, half:] * acc[mp, :, half:] + (pv[:, half:] + pv_hi)

        first, second = s_refs

        def pair(i, stats, scores=scores, update=update, first=first, second=second):
            j = 2 * i
            scores(j + 1, second)
            stats = update(j, first, stats)
            scores(j + 2, first)
            return update(j + 1, second, stats)

        idx = jnp.int32
        if qi == 0:
            scores_diag(first)
            update_diag(first, (neg, neg))
        else:
            scores(idx(0), first)
            stats = lax.fori_loop(0, (qi - 1) // 2, pair, (neg, neg))
            if qi % 2 == 1:
                scores_diag(second)
                update_diag(second, update(idx(qi - 1), first, stats))
            else:
                scores(idx(qi - 1), second)
                stats = update(idx(qi - 2), first, stats)
                scores_diag(first)
                update_diag(first, update(idx(qi - 1), second, stats))
        if qi % 2 == 0:
            s_refs = (second, first)

        a1, a2 = acc[0], acc[1]
        ot = a1[:PAIR] / a1[PAIR:PAIR + 1] - lam * (a2[:PAIR] / a2[PAIR:PAIR + 1])
        ot = ot * lax.rsqrt(jnp.mean(ot * ot, axis=0, keepdims=True) + EPS)
        ot = ot * subln_ref[...] * (1.0 - lambda_init)
        o_ref[0, rows, :] = ot.T.astype(o_ref.dtype)


def _diff_attention(p3, slopes, lq1, lk1, lq2, lk2, subln_col, lambda_init, t):
    bsz, seq, _ = p3.shape
    vec = lambda n: pl.BlockSpec((1, n), lambda b, h: (0, 0))
    kernel = functools.partial(_diff_attn_kernel, t=t, lambda_init=lambda_init)
    return pl.pallas_call(
        kernel,
        grid=(bsz, A_HEADS),
        in_specs=[
            pl.BlockSpec(memory_space=pltpu.SMEM),
            pl.BlockSpec((1, seq, PAIR), lambda b, h: (b, 0, h)),
            pl.BlockSpec((1, seq, PAIR), lambda b, h: (b, 0, A_HEADS + h)),
            pl.BlockSpec((1, seq, PAIR), lambda b, h: (b, 0, 2 * A_HEADS + h)),
            vec(A_HEAD_DIM), vec(A_HEAD_DIM), vec(A_HEAD_DIM), vec(A_HEAD_DIM),
            pl.BlockSpec((PAIR, 1), lambda b, h: (0, 0)),
        ],
        out_specs=pl.BlockSpec((1, seq, PAIR), lambda b, h: (b, 0, h)),
        out_shape=jax.ShapeDtypeStruct((bsz, seq, A_COLS), BF16),
        scratch_shapes=[pltpu.VMEM((seq // t, PAIR + BF16_ROWS, t), BF16), pltpu.VMEM((t, t), F32),
                        pltpu.VMEM((t, PAIR), BF16),
                        pltpu.VMEM((2, t, t), F32), pltpu.VMEM((2, t, t), F32),
                        pltpu.VMEM((2, 1, t), F32), pltpu.VMEM((2, 1, t), F32),
                        pltpu.VMEM((2, 2, PAIR + BF16_ROWS, t), F32)],
        compiler_params=pltpu.CompilerParams(
            dimension_semantics=("parallel", "parallel"), vmem_limit_bytes=VMEM_LIMIT),
        name="diff_attn",
    )(slopes, p3, p3, p3, lq1, lk1, lq2, lk2, subln_col)


def _dilated_kernel(q_ref, k_ref, v_ref, kprev_ref, vprev_ref, o_ref, lse_ref, bias_ref, o_scr, lse_scr, *,
                    dil, slopes, tiles_per_seq):
    i = pl.program_id(0)
    blocks = q_ref.shape[2] // STEPS

    @pl.when(i == 0)
    def _():
        qi = lax.broadcasted_iota(jnp.int32, (STEPS, 2 * STEPS), 0)
        kj = lax.broadcasted_iota(jnp.int32, (STEPS, 2 * STEPS), 1)
        step = qi + STEPS - kj
        window = (step >= 0) & (step <= STEPS)
        dist = (step * dil).astype(F32)
        for hd in range(B_HEADS):
            alibi = -float(slopes[hd]) * dist
            bias_ref[0, hd] = jnp.where(window & (kj >= STEPS), alibi, -jnp.inf)
            bias_ref[1, hd] = jnp.where(window, alibi, -jnp.inf)

    lane = lax.broadcasted_iota(jnp.int32, (STEPS, PAIR), 1)
    low = lane < B_HEAD_DIM
    low_keys = lax.broadcasted_iota(jnp.int32, (2 * STEPS, PAIR), 1) < B_HEAD_DIM
    ones_even = jnp.where(low_keys, 1.0, 0.0).astype(BF16)
    ones_odd = jnp.where(low_keys, 0.0, 1.0).astype(BF16)

    first_variant = (i % tiles_per_seq != 0).astype(jnp.int32)
    for u in range(dil * blocks):
        r, nl = divmod(u, blocks)
        variant = 1 if nl > 0 else first_variant
        cur = slice(nl * STEPS, (nl + 1) * STEPS)
        before = slice((nl - 1) * STEPS, nl * STEPS)
        q = q_ref[0, r, cur, :].astype(F32) * (B_HEAD_DIM ** -0.5)
        k_before = k_ref[0, r, before, :] if nl > 0 else kprev_ref[0, r]
        v_before = v_ref[0, r, before, :] if nl > 0 else vprev_ref[0, r]
        kk = jnp.concatenate([k_before, k_ref[0, r, cur, :]], axis=0)
        vv = jnp.concatenate([v_before, v_ref[0, r, cur, :]], axis=0)
        start = nl * (STEPS * dil) + r
        rows = pl.ds(start, STEPS, stride=dil) if dil > 1 else pl.ds(start, STEPS)
        for pr in range(B_HEADS // 2):
            cols = slice(pr * PAIR, (pr + 1) * PAIR)
            qp, kp = q[:, cols], kk[:, cols]
            vp = vv[:, cols].astype(F32)
            v_blocks = jnp.concatenate([
                jnp.concatenate([jnp.where(low_keys, vp, 0.0).astype(BF16), ones_even], axis=1),
                jnp.concatenate([jnp.where(low_keys, 0.0, vp).astype(BF16), ones_odd], axis=1)], axis=0)
            probs, maxes = [], []
            for par in range(2):
                qm = jnp.where(low, qp, 0.0) if par == 0 else jnp.where(low, 0.0, qp)
                s = _nt_dot(qm.astype(BF16), kp) + bias_ref[variant, 2 * pr + par]
                m = jnp.max(s, axis=-1, keepdims=True)
                probs.append(jnp.exp(s - m).astype(BF16))
                maxes.append(m)
            o_den = jnp.dot(jnp.concatenate(probs, axis=1), v_blocks, preferred_element_type=F32)
            den = o_den[:, PAIR:]
            o_scr[pr, rows, :] = o_den[:, :PAIR] / den
            lse_scr[pr, rows, :] = jnp.where(low, maxes[0], maxes[1]) + jnp.log(den)

    for pr in range(B_HEADS // 2):
        cols = slice(pr * PAIR, (pr + 1) * PAIR)
        o_ref[:, cols] = o_scr[pr].astype(o_ref.dtype)
        lse_ref[:, cols] = lse_scr[pr]


def _dilated_attention(pb, bsz, seq, dil):
    tiles, _, per_res, _ = pb.shape
    tm = dil * per_res
    tile_blk = (1, dil, per_res, B_COLS)
    prev_blk = (1, dil, STEPS, B_COLS)
    last = per_res // STEPS - 1
    prev = lambda i: jnp.maximum(i - 1, 0)
    kernel = functools.partial(_dilated_kernel, dil=dil, slopes=_alibi_slopes(B_HEADS),
                               tiles_per_seq=tiles // bsz)
    out_block = pl.BlockSpec((tm, B_COLS), lambda i: (i, 0))
    return pl.pallas_call(
        kernel,
        grid=(tiles,),
        in_specs=[
            pl.BlockSpec(tile_blk, lambda i: (i, 0, 0, 0)),
            pl.BlockSpec(tile_blk, lambda i: (i, 0, 0, 1)),
            pl.BlockSpec(tile_blk, lambda i: (i, 0, 0, 2)),
            pl.BlockSpec(prev_blk, lambda i: (prev(i), 0, last, 1)),
            pl.BlockSpec(prev_blk, lambda i: (prev(i), 0, last, 2)),
        ],
        out_specs=[out_block, out_block],
        out_shape=[jax.ShapeDtypeStruct((bsz * seq, B_COLS), BF16),
                   jax.ShapeDtypeStruct((bsz * seq, B_COLS), F32)],
        scratch_shapes=[pltpu.VMEM((2, B_HEADS, STEPS, 2 * STEPS), F32),
                        pltpu.VMEM((B_HEADS // 2, tm, PAIR), F32), pltpu.VMEM((B_HEADS // 2, tm, PAIR), F32)],
        compiler_params=pltpu.CompilerParams(
            dimension_semantics=("arbitrary",), vmem_limit_bytes=VMEM_LIMIT),
        name=f"dilated_attn_d{dil}",
    )(pb, pb, pb, pb, pb)


MERGE_SLABS = 2


def _merge_kernel(oa_ref, o0_ref, o1_ref, o2_ref, l0_ref, l1_ref, l2_ref, ga_ref, gb_ref, x_ref,
                  wpa_ref, wpb_ref, wo_ref, g_ref, b_ref, out_ref):
    slab = out_ref.shape[0] // MERGE_SLABS
    for part in range(MERGE_SLABS):
        rows = slice(part * slab, (part + 1) * slab)
        l0, l1, l2 = l0_ref[rows, :], l1_ref[rows, :], l2_ref[rows, :]
        mx = jnp.maximum(jnp.maximum(l0, l1), l2)
        e0, e1, e2 = jnp.exp(l0 - mx), jnp.exp(l1 - mx), jnp.exp(l2 - mx)
        ob = (e0 * o0_ref[rows, :].astype(F32) + e1 * o1_ref[rows, :].astype(F32)
              + e2 * o2_ref[rows, :].astype(F32)) / (e0 + e1 + e2)
        ya = jnp.dot(oa_ref[rows, :], wpa_ref[...], preferred_element_type=F32)
        yb = jnp.dot(ob.astype(BF16), wpb_ref[...], preferred_element_type=F32)
        y = ga_ref[rows, :].astype(F32) * ya + gb_ref[rows, :].astype(F32) * yb
        z = ALPHA * x_ref[rows, :] + jnp.dot(y.astype(BF16), wo_ref[...], preferred_element_type=F32)
        out_ref[rows, :] = _layer_norm(z, g_ref[...], b_ref[...])


def _merge(oa, obs, lses, gates, x2, wpa, wpb, wo, g, b, tm):
    m = x2.shape[0]
    row = lambda w: pl.BlockSpec((tm, w), lambda i: (i, 0))
    full = lambda a: pl.BlockSpec(a.shape, lambda i: (0, 0))
    return pl.pallas_call(
        _merge_kernel,
        grid=(m // tm,),
        in_specs=[row(A_COLS), row(B_COLS), row(B_COLS), row(B_COLS), row(B_COLS), row(B_COLS), row(B_COLS),
                  pl.BlockSpec((tm, D_MODEL), lambda i: (i, 0)), pl.BlockSpec((tm, D_MODEL), lambda i: (i, 1)),
                  row(D_MODEL), full(wpa), full(wpb), full(wo), full(g), full(b)],
        out_specs=row(D_MODEL),
        out_shape=jax.ShapeDtypeStruct((m, D_MODEL), F32),
        compiler_params=pltpu.CompilerParams(dimension_semantics=("parallel",), vmem_limit_bytes=VMEM_LIMIT),
        name="merge_ln1",
    )(oa, *obs, *lses, gates, gates, x2, wpa, wpb, wo, g, b)


HALO = BF16_ROWS


def _ffn_kernel(x_ref, halo_ref, wup_ref, wconv_ref, bconv_ref, wd_ref, g_ref, b_ref,
                out_ref, xcat_ref, acc_ref, *, tiles_per_seq):
    i = pl.program_id(0)
    halo = jnp.where(i % tiles_per_seq == 0, 0.0, halo_ref[...])
    xcat_ref[0:HALO, :] = halo.astype(BF16)
    xcat_ref[HALO:, :] = x_ref[...].astype(BF16)
    xcat = xcat_ref[...]

    def conv(col0, width):
        cols = slice(col0, col0 + width)
        hfull = jnp.dot(xcat, wup_ref[:, cols], preferred_element_type=F32)
        out = bconv_ref[:, cols] + wconv_ref[CONV_WIDTH - 1:CONV_WIDTH, cols] * hfull[HALO:, :]
        for back in range(1, CONV_WIDTH):
            shifted = pltpu.roll(hfull, back, axis=0)[HALO:, :]
            out = out + wconv_ref[CONV_WIDTH - 1 - back:CONV_WIDTH - back, cols] * shifted
        return out

    for col0, width in FF_CHUNKS:
        a = conv(col0, width)
        gv = conv(D_FF + col0, width)
        f = 0.5 * a * (1.0 + lax.erf(a * (2.0 ** -0.5))) * gv
        y = jnp.dot(f.astype(BF16), wd_ref[col0:col0 + width, :], preferred_element_type=F32)
        if col0 == 0:
            acc_ref[...] = y
        else:
            acc_ref[...] += y

    z = ALPHA * x_ref[...] + acc_ref[...]
    out_ref[...] = _layer_norm(z, g_ref[...], b_ref[...])


def _ffn(x1, w_up, w_conv, b_conv, w_down, g, b, seq, tm):
    m = x1.shape[0]
    halo_blocks = tm // HALO
    kernel = functools.partial(_ffn_kernel, tiles_per_seq=seq // tm)
    full = lambda a: pl.BlockSpec(a.shape, lambda i: (0, 0))
    once = lambda a: pl.BlockSpec(a.shape, lambda i: (0, 0), pipeline_mode=pl.Buffered(1))
    return pl.pallas_call(
        kernel,
        grid=(m // tm,),
        in_specs=[
            pl.BlockSpec((tm, D_MODEL), lambda i: (i, 0)),
            pl.BlockSpec((HALO, D_MODEL), lambda i: (jnp.maximum(i * halo_blocks - 1, 0), 0)),
            once(w_up), full(w_conv), full(b_conv), once(w_down), full(g), full(b),
        ],
        out_specs=pl.BlockSpec((tm, D_MODEL), lambda i: (i, 0)),
        out_shape=jax.ShapeDtypeStruct((m, D_MODEL), F32),
        scratch_shapes=[pltpu.VMEM((HALO + tm, D_MODEL), BF16), pltpu.VMEM((tm, D_MODEL), F32)],
        compiler_params=pltpu.CompilerParams(dimension_semantics=("parallel",), vmem_limit_bytes=VMEM_LIMIT),
        name="ffn_ln2",
    )(x1, x1, w_up, w_conv, b_conv, w_down, g, b)


def kernel(x, w_in, b_gate, lambda_q1, lambda_k1, lambda_q2, lambda_k2, subln_w, w_pa, w_pb, w_o, ln1_g, ln1_b,
           w_up, w_conv, b_conv, w_down, ln2_g, ln2_b):
    bsz, seq, d = x.shape
    assert (seq, d) == (4096, D_MODEL) and w_in.shape[0] == DEPTH
    slopes_a = jnp.asarray(_alibi_slopes(A_HEADS))
    for l in range(DEPTH):
        lambda_init = 0.8 - 0.6 * math.exp(-0.3 * l)
        x2 = x.reshape(bsz * seq, d)
        w_wide = jnp.concatenate([w_in[l][:, :3 * A_COLS], w_in[l][:, QKV_COLS:]], axis=1).astype(BF16)
        proj_a, gates, xb = _projection_wide(x2, w_wide, b_gate[l][None, :], tm=PROJ_ROWS)
        proj_groups = _projection_groups(xb, w_in[l], tm=PROJ_ROWS)
        oa = _diff_attention(proj_a.reshape(bsz, seq, 3 * A_COLS), slopes_a, lambda_q1[l][None],
                             lambda_k1[l][None], lambda_q2[l][None], lambda_k2[l][None], subln_w[l][:, None],
                             lambda_init, t=512)
        obs, lses = [], []
        for pb, (window, dil) in zip(proj_groups, B_PATTERNS):
            assert window // dil == STEPS
            o, lse = _dilated_attention(pb, bsz, seq, dil)
            obs.append(o)
            lses.append(lse)

        x1 = _merge(oa.reshape(bsz * seq, A_COLS), obs, lses, gates, x2,
                    w_pa[l].astype(BF16), w_pb[l].astype(BF16), w_o[l].astype(BF16),
                    ln1_g[l][None], ln1_b[l][None], tm=512)
        x2 = _ffn(x1, w_up[l].astype(BF16), w_conv[l], b_conv[l][None], w_down[l].astype(BF16),
                  ln2_g[l][None], ln2_b[l][None], seq, tm=1024)
        x = x2.reshape(bsz, seq, d)
    return x
```

```python
import functools
import math

import numpy as np
import jax
import jax.numpy as jnp
from jax import lax
from jax.experimental import pallas as pl
from jax.experimental.pallas import tpu as pltpu

BF16 = jnp.bfloat16
F32 = jnp.float32

D_MODEL = 1024
A_HEADS = 8
A_HEAD_DIM = 64
B_PATTERNS = ((128, 1), (512, 4), (2048, 16))
B_HEADS = 8
B_HEAD_DIM = 64
D_FF = 2816
CONV_WIDTH = 3
EPS = 1e-5
DEPTH = 1
ALPHA = (2.0 * DEPTH) ** 0.25

A_COLS = A_HEADS * 2 * A_HEAD_DIM
B_COLS = B_HEADS * B_HEAD_DIM
N_GROUPS = len(B_PATTERNS)
QKV_COLS = 3 * A_COLS + 3 * N_GROUPS * B_COLS
PAIR = 2 * A_HEAD_DIM
STEPS = 128
BF16_ROWS = 16
LOG2E = math.log2(math.e)
BF16_EXACT = 256
SLOPE_TERMS = 3
PROJ_ROWS = 2048
MXU_WIDTH = 256
FF_CHUNKS = ((0, 6 * MXU_WIDTH), (6 * MXU_WIDTH, 5 * MXU_WIDTH))

VMEM_LIMIT = 56 * 1024 * 1024


def _alibi_slopes(n):
    return np.power(np.float32(2.0), -8.0 * (np.arange(n, dtype=np.float32) + 1) / n).astype(np.float32)


def _nt_dot(a, b):
    return lax.dot_general(a, b, (((1,), (1,)), ((), ())), preferred_element_type=F32)


def _layer_norm(z, g, b):
    mu = jnp.mean(z, axis=-1, keepdims=True)
    zc = z - mu
    var = jnp.mean(zc * zc, axis=-1, keepdims=True)
    return zc * lax.rsqrt(var + EPS) * g + b


WIDE_COLS = 1024
A_TILES = 3 * A_COLS // WIDE_COLS
GROUP_TILES = 3
SPLIT = 4


def _proj_wide_kernel(x_ref, wa_ref, wg_ref, b_ref, pa_ref, g_ref, xb_ref):
    j = pl.program_id(1)

    @pl.when(j == 0)
    def _():
        xb_ref[...] = x_ref[...].astype(BF16)

    @pl.when(j < A_TILES)
    def _():
        pa_ref[...] = jnp.dot(xb_ref[...], wa_ref[...], preferred_element_type=F32).astype(pa_ref.dtype)

    @pl.when(j >= A_TILES)
    def _():
        z = jnp.dot(xb_ref[...], wg_ref[...], preferred_element_type=F32) + b_ref[...]
        g_ref[...] = (0.5 * jnp.tanh(0.5 * z) + 0.5).astype(g_ref.dtype)


def _projection_wide(x2, wa, wg, b_gate, tm):
    m, k = x2.shape
    tn = WIDE_COLS
    gate_tiles = b_gate.shape[1] // tn
    clamp = lambda j, first, count: jnp.clip(j - first, 0, count - 1)
    return pl.pallas_call(
        _proj_wide_kernel,
        grid=(m // tm, A_TILES + gate_tiles),
        in_specs=[pl.BlockSpec((tm, k), lambda i, j: (i, 0)),
                  pl.BlockSpec((k, tn), lambda i, j: (0, clamp(j, 0, A_TILES))),
                  pl.BlockSpec((k, tn), lambda i, j: (0, clamp(j, A_TILES, gate_tiles))),
                  pl.BlockSpec((1, tn), lambda i, j: (0, clamp(j, A_TILES, gate_tiles)))],
        out_specs=[pl.BlockSpec((tm, tn), lambda i, j: (i, clamp(j, 0, A_TILES))),
                   pl.BlockSpec((tm, tn), lambda i, j: (i, clamp(j, A_TILES, gate_tiles))),
                   pl.BlockSpec((tm, k), lambda i, j: (i, 0))],
        out_shape=[jax.ShapeDtypeStruct((m, A_TILES * tn), BF16),
                   jax.ShapeDtypeStruct((m, gate_tiles * tn), BF16),
                   jax.ShapeDtypeStruct((m, k), BF16)],
        compiler_params=pltpu.CompilerParams(
            dimension_semantics=("parallel", "arbitrary"), vmem_limit_bytes=VMEM_LIMIT),
        name="proj_wide",
    )(x2, wa, wg, b_gate)


def _proj_groups_kernel(x_ref, w_ref, *rest):
    group_refs, (acc_ref, tmp_ref) = rest[:N_GROUPS], rest[N_GROUPS:]
    j = pl.program_id(1)
    lane_tiles, tm, _ = acc_ref.shape
    for g, (o_ref, (_, dil)) in enumerate(zip(group_refs, B_PATTERNS)):
        @pl.when((j >= g * GROUP_TILES) & (j < (g + 1) * GROUP_TILES))
        def _(o_ref=o_ref, dil=dil):
            acc = jnp.dot(x_ref[...], w_ref[...].astype(BF16), preferred_element_type=F32)
            if dil == 1:
                o_ref[0, 0] = acc.astype(o_ref.dtype)
                return
            for c in range(lane_tiles):
                acc_ref[c] = acc[:, c * PAIR:(c + 1) * PAIR]
            for c in range(lane_tiles):
                cols = slice(c * PAIR, (c + 1) * PAIR)
                if dil == SPLIT:
                    for r in range(dil):
                        o_ref[0, r, :, cols] = acc_ref[c, pl.ds(r, tm // dil, stride=dil), :].astype(o_ref.dtype)
                else:
                    assert dil == SPLIT * SPLIT
                    part = tm // SPLIT
                    for r in range(SPLIT):
                        tmp_ref[c, r * part:(r + 1) * part, :] = acc_ref[c, pl.ds(r, part, stride=SPLIT), :]
                    for r in range(dil):
                        start = (r % SPLIT) * part + r // SPLIT
                        o_ref[0, r, :, cols] = \
                            tmp_ref[c, pl.ds(start, tm // dil, stride=SPLIT), :].astype(o_ref.dtype)


def _projection_groups(xb, w_in, tm):
    m, k = xb.shape
    tn = B_COLS
    clamp = lambda j, first: jnp.clip(j - first, 0, GROUP_TILES - 1)
    w_block = lambda j: 3 * A_COLS // tn + (j % GROUP_TILES) * N_GROUPS + j // GROUP_TILES
    return pl.pallas_call(
        _proj_groups_kernel,
        grid=(m // tm, N_GROUPS * GROUP_TILES),
        in_specs=[pl.BlockSpec((tm, k), lambda i, j: (i, 0)),
                  pl.BlockSpec((k, tn), lambda i, j: (0, w_block(j)))],
        out_specs=[pl.BlockSpec((1, dil, tm // dil, tn), lambda i, j, g=g: (i, 0, 0, clamp(j, g * GROUP_TILES)))
                   for g, (_, dil) in enumerate(B_PATTERNS)],
        out_shape=[jax.ShapeDtypeStruct((m // tm, dil, tm // dil, GROUP_TILES * tn), BF16)
                   for _, dil in B_PATTERNS],
        scratch_shapes=[pltpu.VMEM((tn // PAIR, tm, PAIR), F32), pltpu.VMEM((tn // PAIR, tm, PAIR), F32)],
        compiler_params=pltpu.CompilerParams(
            dimension_semantics=("parallel", "arbitrary"), vmem_limit_bytes=VMEM_LIMIT),
        name="proj_groups",
    )(xb, w_in)


def _diff_attn_kernel(slopes_ref, q_ref, k_ref, v_ref, lq1_ref, lk1_ref, lq2_ref, lk2_ref, subln_ref,
                      o_ref, vt_ref, mask_ref, kaug_ref, sa_ref, sb_ref, ma_ref, mb_ref, acc_ref, *,
                      t, lambda_init):
    slope = slopes_ref[pl.program_id(1)] * LOG2E
    nchunks = vt_ref.shape[0]

    def transpose_v(c, carry):
        r0 = pl.multiple_of(c * t, t)
        vt_ref[c, 0:PAIR, :] = v_ref[0, pl.ds(r0, t), :].astype(F32).T.astype(BF16)
        vt_ref[c, PAIR:, :] = jnp.ones((BF16_ROWS, t), BF16)
        return carry
    lax.fori_loop(0, nchunks, transpose_v, 0)
    krow = lax.broadcasted_iota(jnp.int32, (t, t), 0)
    qcol = lax.broadcasted_iota(jnp.int32, (t, t), 1)
    mask_ref[...] = jnp.where(krow <= qcol, 0.0, -jnp.inf)

    r = lax.broadcasted_iota(jnp.int32, (t, PAIR), 0)
    c = lax.broadcasted_iota(jnp.int32, (t, PAIR), 1)
    r_low = (r % BF16_EXACT).astype(F32)
    r_high = (r - r % BF16_EXACT).astype(F32)
    kaug_ref[...] = jnp.where(c < SLOPE_TERMS, r_low, jnp.where(c < 2 * SLOPE_TERMS, r_high, 0.0)).astype(BF16)
    rest = jnp.full((PAIR, t), slope, F32)
    wrow = lax.broadcasted_iota(jnp.int32, (PAIR, t), 0)
    waug = jnp.zeros((PAIR, t), F32)
    for term in range(SLOPE_TERMS):
        part = rest.astype(BF16).astype(F32)
        waug = jnp.where((wrow == term) | (wrow == SLOPE_TERMS + term), part, waug)
        rest = rest - part
    waug = waug.astype(BF16)
    dim = lax.broadcasted_iota(jnp.int32, (PAIR, t), 0)

    lam = (jnp.exp(jnp.sum(lq1_ref[...] * lk1_ref[...], axis=-1, keepdims=True))
           - jnp.exp(jnp.sum(lq2_ref[...] * lk2_ref[...], axis=-1, keepdims=True))
           + lambda_init)
    neg = jnp.full((1, t), -jnp.inf, F32)
    half = t // 2
    s_refs = ((sa_ref, ma_ref), (sb_ref, mb_ref))

    for qi in range(nchunks):
        rows = slice(qi * t, (qi + 1) * t)
        acc = acc_ref.at[qi % 2]
        qt = q_ref[0, rows, :].astype(F32).T * (A_HEAD_DIM ** -0.5 * LOG2E)
        qt_maps = (jnp.concatenate([jnp.where(dim < A_HEAD_DIM, qt, 0.0).astype(BF16), waug], axis=0),
                   jnp.concatenate([jnp.where(dim < A_HEAD_DIM, 0.0, qt).astype(BF16), waug], axis=0))
        acc[...] = jnp.zeros(acc.shape, F32)

        def scores(j, buf, qt_maps=qt_maps):
            s_ref, max_ref = buf
            kc = k_ref[0, pl.ds(pl.multiple_of(j * t, t), t), :]
            kc = jnp.concatenate([kc, kaug_ref[...]], axis=1)
            for mp in range(2):
                s = jnp.dot(kc, qt_maps[mp], preferred_element_type=F32)
                s_ref[mp] = s
                max_ref[mp] = jnp.max(s, axis=0, keepdims=True)

        def update(j, buf, stats, qi=qi, acc=acc):
            s_ref, max_ref = buf
            vt = vt_ref[j]
            shift = slope * jnp.asarray((j - qi) * t, F32)
            out = []
            for mp in range(2):
                m_new = jnp.maximum(stats[mp], max_ref[mp] + shift)
                p = jnp.exp2(s_ref[mp] - (m_new - shift)).astype(BF16)
                alpha = jnp.exp2(stats[mp] - m_new)
                acc[mp] = alpha * acc[mp] + jnp.dot(vt, p, preferred_element_type=F32)
                out.append(m_new)
            return tuple(out)

        def scores_diag(buf, qi=qi, qt_maps=qt_maps):
            s_ref, max_ref = buf
            k_lo = jnp.concatenate([k_ref[0, qi * t:qi * t + half, :], kaug_ref[0:half, :]], axis=1)
            k_hi = jnp.concatenate([k_ref[0, qi * t + half:(qi + 1) * t, :], kaug_ref[half:, :]], axis=1)
            for mp in range(2):
                s_lo = jnp.dot(k_lo, qt_maps[mp], preferred_element_type=F32) + mask_ref[0:half, :]
                s_hi = jnp.dot(k_hi, qt_maps[mp][:, half:], preferred_element_type=F32) \
                    + mask_ref[half:, half:]
                s_ref[mp, 0:half, :] = s_lo
                s_ref[mp, half:, half:] = s_hi
                max_lo = jnp.max(s_lo, axis=0, keepdims=True)
                max_ref[mp, :, 0:half] = max_lo[:, :half]
                max_ref[mp, :, half:] = jnp.maximum(max_lo[:, half:], jnp.max(s_hi, axis=0, keepdims=True))

        def update_diag(buf, stats, qi=qi, acc=acc):
            s_ref, max_ref = buf
            for mp in range(2):
                m_new = jnp.maximum(stats[mp], max_ref[mp])
                p_lo = jnp.exp2(s_ref[mp, 0:half, :] - m_new).astype(BF16)
                p_hi = jnp.exp2(s_ref[mp, half:, half:] - m_new[:, half:]).astype(BF16)
                alpha = jnp.exp2(stats[mp] - m_new)
                pv = jnp.dot(vt_ref[qi, :, 0:half], p_lo, preferred_element_type=F32)
                pv_hi = jnp.dot(vt_ref[qi, :, half:], p_hi, preferred_element_type=F32)
                acc[mp, :, 0:half] = alpha[:, :half] * acc[mp, :, 0:half] + pv[:, :half]
                acc[mp, :, half:] = alpha[:, half:] * acc[mp, :, half:] + (pv[:, half:] + pv_hi)

        first, second = s_refs

        def pair(i, stats, scores=scores, update=update, first=first, second=second):
            j = 2 * i
            scores(j + 1, second)
            stats = update(j, first, stats)
            scores(j + 2, first)
            return update(j + 1, second, stats)

        idx = jnp.int32
        if qi == 0:
            scores_diag(first)
            update_diag(first, (neg, neg))
        else:
            scores(idx(0), first)
            stats = lax.fori_loop(0, (qi - 1) // 2, pair, (neg, neg))
            if qi % 2 == 1:
                scores_diag(second)
                update_diag(second, update(idx(qi - 1), first, stats))
            else:
                scores(idx(qi - 1), second)
                stats = update(idx(qi - 2), first, stats)
                scores_diag(first)
                update_diag(first, update(idx(qi - 1), second, stats))
        if qi % 2 == 0:
            s_refs = (second, first)

        a1, a2 = acc[0], acc[1]
        ot = a1[:PAIR] / a1[PAIR:PAIR + 1] - lam * (a2[:PAIR] / a2[PAIR:PAIR + 1])
        ot = ot * lax.rsqrt(jnp.mean(ot * ot, axis=0, keepdims=True) + EPS)
        ot = ot * subln_ref[...] * (1.0 - lambda_init)
        o_ref[0, rows, :] = ot.T.astype(o_ref.dtype)


def _diff_attention(p3, slopes, lq1, lk1, lq2, lk2, subln_col, lambda_init, t):
    bsz, seq, _ = p3.shape
    vec = lambda n: pl.BlockSpec((1, n), lambda b, h: (0, 0))
    kernel = functools.partial(_diff_attn_kernel, t=t, lambda_init=lambda_init)
    return pl.pallas_call(
        kernel,
        grid=(bsz, A_HEADS),
        in_specs=[
            pl.BlockSpec(memory_space=pltpu.SMEM),
            pl.BlockSpec((1, seq, PAIR), lambda b, h: (b, 0, h)),
            pl.BlockSpec((1, seq, PAIR), lambda b, h: (b, 0, A_HEADS + h)),
            pl.BlockSpec((1, seq, PAIR), lambda b, h: (b, 0, 2 * A_HEADS + h)),
            vec(A_HEAD_DIM), vec(A_HEAD_DIM), vec(A_HEAD_DIM), vec(A_HEAD_DIM),
            pl.BlockSpec((PAIR, 1), lambda b, h: (0, 0)),
        ],
        out_specs=pl.BlockSpec((1, seq, PAIR), lambda b, h: (b, 0, h)),
        out_shape=jax.ShapeDtypeStruct((bsz, seq, A_COLS), BF16),
        scratch_shapes=[pltpu.VMEM((seq // t, PAIR + BF16_ROWS, t), BF16), pltpu.VMEM((t, t), F32),
                        pltpu.VMEM((t, PAIR), BF16),
                        pltpu.VMEM((2, t, t), F32), pltpu.VMEM((2, t, t), F32),
                        pltpu.VMEM((2, 1, t), F32), pltpu.VMEM((2, 1, t), F32),
                        pltpu.VMEM((2, 2, PAIR + BF16_ROWS, t), F32)],
        compiler_params=pltpu.CompilerParams(
            dimension_semantics=("parallel", "parallel"), vmem_limit_bytes=VMEM_LIMIT),
        name="diff_attn",
    )(slopes, p3, p3, p3, lq1, lk1, lq2, lk2, subln_col)


def _dilated_kernel(q_ref, k_ref, v_ref, kprev_ref, vprev_ref, o_ref, lse_ref, bias_ref, o_scr, lse_scr,
                    *maybe_order_scr, dil, slopes, tiles_per_seq):
    i = pl.program_id(0)
    blocks = q_ref.shape[2] // STEPS
    tm = o_ref.shape[0]
    two_pass = bool(maybe_order_scr)
    if two_pass:
        assert dil == SPLIT * SPLIT and blocks == 1
        order_scr, = maybe_order_scr

    @pl.when(i == 0)
    def _():
        qi = lax.broadcasted_iota(jnp.int32, (STEPS, 2 * STEPS), 0)
        kj = lax.broadcasted_iota(jnp.int32, (STEPS, 2 * STEPS), 1)
        step = qi + STEPS - kj
        window = (step >= 0) & (step <= STEPS)
        dist = (step * dil).astype(F32)
        for hd in range(B_HEADS):
            alibi = -float(slopes[hd]) * dist
            bias_ref[0, hd] = jnp.where(window & (kj >= STEPS), alibi, -jnp.inf)
            bias_ref[1, hd] = jnp.where(window, alibi, -jnp.inf)

    lane = lax.broadcasted_iota(jnp.int32, (STEPS, PAIR), 1)
    low = lane < B_HEAD_DIM
    low_keys = lax.broadcasted_iota(jnp.int32, (2 * STEPS, PAIR), 1) < B_HEAD_DIM
    ones_even = jnp.where(low_keys, 1.0, 0.0).astype(BF16)
    ones_odd = jnp.where(low_keys, 0.0, 1.0).astype(BF16)

    first_variant = (i % tiles_per_seq != 0).astype(jnp.int32)
    for u in range(dil * blocks):
        r, nl = divmod(u, blocks)
        variant = 1 if nl > 0 else first_variant
        cur = slice(nl * STEPS, (nl + 1) * STEPS)
        before = slice((nl - 1) * STEPS, nl * STEPS)
        q = q_ref[0, r, cur, :].astype(F32) * (B_HEAD_DIM ** -0.5)
        k_before = k_ref[0, r, before, :] if nl > 0 else kprev_ref[0, r]
        v_before = v_ref[0, r, before, :] if nl > 0 else vprev_ref[0, r]
        kk = jnp.concatenate([k_before, k_ref[0, r, cur, :]], axis=0)
        vv = jnp.concatenate([v_before, v_ref[0, r, cur, :]], axis=0)
        start = nl * (STEPS * dil) + r
        if two_pass:
            rows = pl.ds((r % SPLIT) * (tm // SPLIT) + r // SPLIT, STEPS, stride=SPLIT)
        else:
            rows = pl.ds(start, STEPS, stride=dil) if dil > 1 else pl.ds(start, STEPS)
        for pr in range(B_HEADS // 2):
            cols = slice(pr * PAIR, (pr + 1) * PAIR)
            qp, kp = q[:, cols], kk[:, cols]
            vp = vv[:, cols].astype(F32)
            v_blocks = jnp.concatenate([
                jnp.concatenate([jnp.where(low_keys, vp, 0.0).astype(BF16), ones_even], axis=1),
                jnp.concatenate([jnp.where(low_keys, 0.0, vp).astype(BF16), ones_odd], axis=1)], axis=0)
            probs, maxes = [], []
            for par in range(2):
                qm = jnp.where(low, qp, 0.0) if par == 0 else jnp.where(low, 0.0, qp)
                s = _nt_dot(qm.astype(BF16), kp) + bias_ref[variant, 2 * pr + par]
                m = jnp.max(s, axis=-1, keepdims=True)
                probs.append(jnp.exp(s - m).astype(BF16))
                maxes.append(m)
            o_den = jnp.dot(jnp.concatenate(probs, axis=1), v_blocks, preferred_element_type=F32)
            den = o_den[:, PAIR:]
            o_scr[pr, rows, :] = o_den[:, :PAIR] / den
            lse_scr[pr, rows, :] = jnp.where(low, maxes[0], maxes[1]) + jnp.log(den)

    for src, dst in ((o_scr, o_ref), (lse_scr, lse_ref)):
        for pr in range(B_HEADS // 2):
            if two_pass:
                part = tm // SPLIT
                for r in range(SPLIT):
                    order_scr[pr, pl.ds(r, part, stride=SPLIT), :] = src[pr, r * part:(r + 1) * part, :]
            ordered = order_scr if two_pass else src
            dst[:, pr * PAIR:(pr + 1) * PAIR] = ordered[pr].astype(dst.dtype)


def _dilated_attention(pb, bsz, seq, dil):
    tiles, _, per_res, _ = pb.shape
    tm = dil * per_res
    tile_blk = (1, dil, per_res, B_COLS)
    prev_blk = (1, dil, STEPS, B_COLS)
    last = per_res // STEPS - 1
    prev = lambda i: jnp.maximum(i - 1, 0)
    kernel = functools.partial(_dilated_kernel, dil=dil, slopes=_alibi_slopes(B_HEADS),
                               tiles_per_seq=tiles // bsz)
    out_block = pl.BlockSpec((tm, B_COLS), lambda i: (i, 0))
    return pl.pallas_call(
        kernel,
        grid=(tiles,),
        in_specs=[
            pl.BlockSpec(tile_blk, lambda i: (i, 0, 0, 0)),
            pl.BlockSpec(tile_blk, lambda i: (i, 0, 0, 1)),
            pl.BlockSpec(tile_blk, lambda i: (i, 0, 0, 2)),
            pl.BlockSpec(prev_blk, lambda i: (prev(i), 0, last, 1)),
            pl.BlockSpec(prev_blk, lambda i: (prev(i), 0, last, 2)),
        ],
        out_specs=[out_block, out_block],
        out_shape=[jax.ShapeDtypeStruct((bsz * seq, B_COLS), BF16),
                   jax.ShapeDtypeStruct((bsz * seq, B_COLS), F32)],
        scratch_shapes=[pltpu.VMEM((2, B_HEADS, STEPS, 2 * STEPS), F32),
                        pltpu.VMEM((B_HEADS // 2, tm, PAIR), F32), pltpu.VMEM((B_HEADS // 2, tm, PAIR), F32)]
        + [pltpu.VMEM((B_HEADS // 2, tm, PAIR), F32)] * (dil == SPLIT * SPLIT),
        compiler_params=pltpu.CompilerParams(
            dimension_semantics=("arbitrary",), vmem_limit_bytes=VMEM_LIMIT),
        name=f"dilated_attn_d{dil}",
    )(pb, pb, pb, pb, pb)


MERGE_SLABS = 2


def _merge_kernel(oa_ref, o0_ref, o1_ref, o2_ref, l0_ref, l1_ref, l2_ref, ga_ref, gb_ref, x_ref,
                  wpa_ref, wpb_ref, wo_ref, g_ref, b_ref, out_ref):
    slab = out_ref.shape[0] // MERGE_SLABS
    for part in range(MERGE_SLABS):
        rows = slice(part * slab, (part + 1) * slab)
        l0, l1, l2 = l0_ref[rows, :], l1_ref[rows, :], l2_ref[rows, :]
        mx = jnp.maximum(jnp.maximum(l0, l1), l2)
        e0, e1, e2 = jnp.exp(l0 - mx), jnp.exp(l1 - mx), jnp.exp(l2 - mx)
        ob = (e0 * o0_ref[rows, :].astype(F32) + e1 * o1_ref[rows, :].astype(F32)
              + e2 * o2_ref[rows, :].astype(F32)) / (e0 + e1 + e2)
        ya = jnp.dot(oa_ref[rows, :], wpa_ref[...], preferred_element_type=F32)
        yb = jnp.dot(ob.astype(BF16), wpb_ref[...], preferred_element_type=F32)
        y = ga_ref[rows, :].astype(F32) * ya + gb_ref[rows, :].astype(F32) * yb
        z = ALPHA * x_ref[rows, :] + jnp.dot(y.astype(BF16), wo_ref[...], preferred_element_type=F32)
        out_ref[rows, :] = _layer_norm(z, g_ref[...], b_ref[...])


def _merge(oa, obs, lses, gates, x2, wpa, wpb, wo, g, b, tm):
    m = x2.shape[0]
    row = lambda w: pl.BlockSpec((tm, w), lambda i: (i, 0))
    full = lambda a: pl.BlockSpec(a.shape, lambda i: (0, 0))
    return pl.pallas_call(
        _merge_kernel,
        grid=(m // tm,),
        in_specs=[row(A_COLS), row(B_COLS), row(B_COLS), row(B_COLS), row(B_COLS), row(B_COLS), row(B_COLS),
                  pl.BlockSpec((tm, D_MODEL), lambda i: (i, 0)), pl.BlockSpec((tm, D_MODEL), lambda i: (i, 1)),
                  row(D_MODEL), full(wpa), full(wpb), full(wo), full(g), full(b)],
        out_specs=row(D_MODEL),
        out_shape=jax.ShapeDtypeStruct((m, D_MODEL), F32),
        compiler_params=pltpu.CompilerParams(dimension_semantics=("parallel",), vmem_limit_bytes=VMEM_LIMIT),
        name="merge_ln1",
    )(oa, *obs, *lses, gates, gates, x2, wpa, wpb, wo, g, b)


HALO = BF16_ROWS


def _ffn_kernel(x_ref, halo_ref, wup_ref, wconv_ref, bconv_ref, wd_ref, g_ref, b_ref,
                out_ref, xcat_ref, acc_ref, *, tiles_per_seq):
    i = pl.program_id(0)
    halo = jnp.where(i % tiles_per_seq == 0, 0.0, halo_ref[...])
    xcat_ref[0:HALO, :] = halo.astype(BF16)
    xcat_ref[HALO:, :] = x_ref[...].astype(BF16)
    xcat = xcat_ref[...]

    def conv(col0, width):
        cols = slice(col0, col0 + width)
        hfull = jnp.dot(xcat, wup_ref[:, cols], preferred_element_type=F32)
        out = bconv_ref[:, cols] + wconv_ref[CONV_WIDTH - 1:CONV_WIDTH, cols] * hfull[HALO:, :]
        for back in range(1, CONV_WIDTH):
            shifted = pltpu.roll(hfull, back, axis=0)[HALO:, :]
            out = out + wconv_ref[CONV_WIDTH - 1 - back:CONV_WIDTH - back, cols] * shifted
        return out

    for col0, width in FF_CHUNKS:
        a = conv(col0, width)
        gv = conv(D_FF + col0, width)
        f = 0.5 * a * (1.0 + lax.erf(a * (2.0 ** -0.5))) * gv
        y = jnp.dot(f.astype(BF16), wd_ref[col0:col0 + width, :], preferred_element_type=F32)
        if col0 == 0:
            acc_ref[...] = y
        else:
            acc_ref[...] += y

    z = ALPHA * x_ref[...] + acc_ref[...]
    out_ref[...] = _layer_norm(z, g_ref[...], b_ref[...])


def _ffn(x1, w_up, w_conv, b_conv, w_down, g, b, seq, tm):
    m = x1.shape[0]
    halo_blocks = tm // HALO
    kernel = functools.partial(_ffn_kernel, tiles_per_seq=seq // tm)
    full = lambda a: pl.BlockSpec(a.shape, lambda i: (0, 0))
    once = lambda a: pl.BlockSpec(a.shape, lambda i: (0, 0), pipeline_mode=pl.Buffered(1))
    return pl.pallas_call(
        kernel,
        grid=(m // tm,),
        in_specs=[
            pl.BlockSpec((tm, D_MODEL), lambda i: (i, 0)),
            pl.BlockSpec((HALO, D_MODEL), lambda i: (jnp.maximum(i * halo_blocks - 1, 0), 0)),
            once(w_up), full(w_conv), full(b_conv), once(w_down), full(g), full(b),
        ],
        out_specs=pl.BlockSpec((tm, D_MODEL), lambda i: (i, 0)),
        out_shape=jax.ShapeDtypeStruct((m, D_MODEL), F32),
        scratch_shapes=[pltpu.VMEM((HALO + tm, D_MODEL), BF16), pltpu.VMEM((tm, D_MODEL), F32)],
        compiler_params=pltpu.CompilerParams(dimension_semantics=("parallel",), vmem_limit_bytes=VMEM_LIMIT),
        name="ffn_ln2",
    )(x1, x1, w_up, w_conv, b_conv, w_down, g, b)


def kernel(x, w_in, b_gate, lambda_q1, lambda_k1, lambda_q2, lambda_k2, subln_w, w_pa, w_pb, w_o, ln1_g, ln1_b,
           w_up, w_conv, b_conv, w_down, ln2_g, ln2_b):
    bsz, seq, d = x.shape
    assert (seq, d) == (4096, D_MODEL) and w_in.shape[0] == DEPTH
    slopes_a = jnp.asarray(_alibi_slopes(A_HEADS))
    for l in range(DEPTH):
        lambda_init = 0.8 - 0.6 * math.exp(-0.3 * l)
        x2 = x.reshape(bsz * seq, d)
        proj_a, gates, xb = _projection_wide(x2, w_in[l][:, :3 * A_COLS].astype(BF16),
                                             w_in[l][:, QKV_COLS:].astype(BF16), b_gate[l][None, :], tm=PROJ_ROWS)
        proj_groups = _projection_groups(xb, w_in[l], tm=PROJ_ROWS)
        oa = _diff_attention(proj_a.reshape(bsz, seq, 3 * A_COLS), slopes_a, lambda_q1[l][None],
                             lambda_k1[l][None], lambda_q2[l][None], lambda_k2[l][None], subln_w[l][:, None],
                             lambda_init, t=512)
        obs, lses = [], []
        for pb, (window, dil) in zip(proj_groups, B_PATTERNS):
            assert window // dil == STEPS
            o, lse = _dilated_attention(pb, bsz, seq, dil)
            obs.append(o)
            lses.append(lse)

        x1 = _merge(oa.reshape(bsz * seq, A_COLS), obs, lses, gates, x2,
                    w_pa[l].astype(BF16), w_pb[l].astype(BF16), w_o[l].astype(BF16),
                    ln1_g[l][None], ln1_b[l][None], tm=512)
        x2 = _ffn(x1, w_up[l].astype(BF16), w_conv[l], b_conv[l][None], w_down[l].astype(BF16),
                  ln2_g[l][None], ln2_b[l][None], seq, tm=1024)
        x = x2.reshape(bsz, seq, d)
    return x
```

```python
import functools
import math

import numpy as np
import jax
import jax.numpy as jnp
from jax import lax
from jax.experimental import pallas as pl
from jax.experimental.pallas import tpu as pltpu

BF16 = jnp.bfloat16
F32 = jnp.float32

D_MODEL = 1024
A_HEADS = 8
A_HEAD_DIM = 64
B_PATTERNS = ((128, 1), (512, 4), (2048, 16))
B_HEADS = 8
B_HEAD_DIM = 64
D_FF = 2816
CONV_WIDTH = 3
EPS = 1e-5
DEPTH = 1
ALPHA = (2.0 * DEPTH) ** 0.25

A_COLS = A_HEADS * 2 * A_HEAD_DIM
B_COLS = B_HEADS * B_HEAD_DIM
N_GROUPS = len(B_PATTERNS)
QKV_COLS = 3 * A_COLS + 3 * N_GROUPS * B_COLS
PAIR = 2 * A_HEAD_DIM
STEPS = 128
BF16_ROWS = 16
LOG2E = math.log2(math.e)
BF16_EXACT = 256
SLOPE_TERMS = 3
PROJ_ROWS = 2048
MXU_WIDTH = 256
FF_CHUNKS = ((0, 6 * MXU_WIDTH), (6 * MXU_WIDTH, 5 * MXU_WIDTH))

VMEM_LIMIT = 56 * 1024 * 1024


def _alibi_slopes(n):
    return np.power(np.float32(2.0), -8.0 * (np.arange(n, dtype=np.float32) + 1) / n).astype(np.float32)


def _nt_dot(a, b):
    return lax.dot_general(a, b, (((1,), (1,)), ((), ())), preferred_element_type=F32)


def _layer_norm(z, g, b):
    mu = jnp.mean(z, axis=-1, keepdims=True)
    zc = z - mu
    var = jnp.mean(zc * zc, axis=-1, keepdims=True)
    return zc * lax.rsqrt(var + EPS) * g + b


WIDE_COLS = 1024
A_TILES = 3 * A_COLS // WIDE_COLS
GROUP_TILES = 3
SPLIT = 4


def _proj_wide_kernel(x_ref, wa_ref, wg_lo_ref, wg_hi_ref, b_ref, pa_ref, g_ref, xb_ref):
    j = pl.program_id(1)

    @pl.when(j == 0)
    def _():
        xb_ref[...] = x_ref[...].astype(BF16)

    @pl.when(j < A_TILES)
    def _():
        pa_ref[...] = jnp.dot(xb_ref[...], wa_ref[...], preferred_element_type=F32).astype(pa_ref.dtype)

    @pl.when(j >= A_TILES)
    def _():
        wg = jnp.concatenate([wg_lo_ref[...], wg_hi_ref[...]], axis=1)
        z = jnp.dot(xb_ref[...], wg, preferred_element_type=F32) + b_ref[...]
        g_ref[...] = (0.5 * jnp.tanh(0.5 * z) + 0.5).astype(g_ref.dtype)


def _projection_wide(x2, wb, b_gate, tm):
    m, k = x2.shape
    tn = WIDE_COLS
    gate_tiles = b_gate.shape[1] // tn
    gate_first = QKV_COLS // (tn // 2)
    clamp = lambda j, first, count: jnp.clip(j - first, 0, count - 1)
    return pl.pallas_call(
        _proj_wide_kernel,
        grid=(m // tm, A_TILES + gate_tiles),
        in_specs=[pl.BlockSpec((tm, k), lambda i, j: (i, 0)),
                  pl.BlockSpec((k, tn), lambda i, j: (0, clamp(j, 0, A_TILES))),
                  pl.BlockSpec((k, tn // 2), lambda i, j: (0, gate_first + 2 * clamp(j, A_TILES, gate_tiles))),
                  pl.BlockSpec((k, tn // 2), lambda i, j: (0, gate_first + 2 * clamp(j, A_TILES, gate_tiles) + 1)),
                  pl.BlockSpec((1, tn), lambda i, j: (0, clamp(j, A_TILES, gate_tiles)))],
        out_specs=[pl.BlockSpec((tm, tn), lambda i, j: (i, clamp(j, 0, A_TILES))),
                   pl.BlockSpec((tm, tn), lambda i, j: (i, clamp(j, A_TILES, gate_tiles))),
                   pl.BlockSpec((tm, k), lambda i, j: (i, 0))],
        out_shape=[jax.ShapeDtypeStruct((m, A_TILES * tn), BF16),
                   jax.ShapeDtypeStruct((m, gate_tiles * tn), BF16),
                   jax.ShapeDtypeStruct((m, k), BF16)],
        compiler_params=pltpu.CompilerParams(
            dimension_semantics=("parallel", "arbitrary"), vmem_limit_bytes=VMEM_LIMIT),
        name="proj_wide",
    )(x2, wb, wb, wb, b_gate)


def _proj_groups_kernel(x_ref, w_ref, *rest):
    group_refs, (acc_ref, tmp_ref) = rest[:N_GROUPS], rest[N_GROUPS:]
    j = pl.program_id(1)
    lane_tiles, tm, _ = acc_ref.shape
    for g, (o_ref, (_, dil)) in enumerate(zip(group_refs, B_PATTERNS)):
        @pl.when((j >= g * GROUP_TILES) & (j < (g + 1) * GROUP_TILES))
        def _(o_ref=o_ref, dil=dil):
            acc = jnp.dot(x_ref[...], w_ref[...], preferred_element_type=F32)
            if dil == 1:
                o_ref[0, 0] = acc.astype(o_ref.dtype)
                return
            for c in range(lane_tiles):
                acc_ref[c] = acc[:, c * PAIR:(c + 1) * PAIR]
            for c in range(lane_tiles):
                cols = slice(c * PAIR, (c + 1) * PAIR)
                if dil == SPLIT:
                    for r in range(dil):
                        o_ref[0, r, :, cols] = acc_ref[c, pl.ds(r, tm // dil, stride=dil), :].astype(o_ref.dtype)
                else:
                    assert dil == SPLIT * SPLIT
                    part = tm // SPLIT
                    for r in range(SPLIT):
                        tmp_ref[c, r * part:(r + 1) * part, :] = acc_ref[c, pl.ds(r, part, stride=SPLIT), :]
                    for r in range(dil):
                        start = (r % SPLIT) * part + r // SPLIT
                        o_ref[0, r, :, cols] = \
                            tmp_ref[c, pl.ds(start, tm // dil, stride=SPLIT), :].astype(o_ref.dtype)


def _projection_groups(xb, w_in, tm):
    m, k = xb.shape
    tn = B_COLS
    clamp = lambda j, first: jnp.clip(j - first, 0, GROUP_TILES - 1)
    w_block = lambda j: 3 * A_COLS // tn + (j % GROUP_TILES) * N_GROUPS + j // GROUP_TILES
    return pl.pallas_call(
        _proj_groups_kernel,
        grid=(m // tm, N_GROUPS * GROUP_TILES),
        in_specs=[pl.BlockSpec((tm, k), lambda i, j: (i, 0)),
                  pl.BlockSpec((k, tn), lambda i, j: (0, w_block(j)))],
        out_specs=[pl.BlockSpec((1, dil, tm // dil, tn), lambda i, j, g=g: (i, 0, 0, clamp(j, g * GROUP_TILES)))
                   for g, (_, dil) in enumerate(B_PATTERNS)],
        out_shape=[jax.ShapeDtypeStruct((m // tm, dil, tm // dil, GROUP_TILES * tn), BF16)
                   for _, dil in B_PATTERNS],
        scratch_shapes=[pltpu.VMEM((tn // PAIR, tm, PAIR), F32), pltpu.VMEM((tn // PAIR, tm, PAIR), F32)],
        compiler_params=pltpu.CompilerParams(
            dimension_semantics=("parallel", "arbitrary"), vmem_limit_bytes=VMEM_LIMIT),
        name="proj_groups",
    )(xb, w_in)


def _diff_attn_kernel(slopes_ref, q_ref, k_ref, v_ref, lq1_ref, lk1_ref, lq2_ref, lk2_ref, subln_ref,
                      o_ref, vt_ref, mask_ref, kaug_ref, sa_ref, sb_ref, ma_ref, mb_ref, acc_ref, *,
                      t, lambda_init):
    slope = slopes_ref[pl.program_id(1)] * LOG2E
    nchunks = vt_ref.shape[0]

    def transpose_v(c, carry):
        r0 = pl.multiple_of(c * t, t)
        vt_ref[c, 0:PAIR, :] = v_ref[0, pl.ds(r0, t), :].astype(F32).T.astype(BF16)
        vt_ref[c, PAIR:, :] = jnp.ones((BF16_ROWS, t), BF16)
        return carry
    lax.fori_loop(0, nchunks, transpose_v, 0)
    krow = lax.broadcasted_iota(jnp.int32, (t, t), 0)
    qcol = lax.broadcasted_iota(jnp.int32, (t, t), 1)
    mask_ref[...] = jnp.where(krow <= qcol, 0.0, -jnp.inf)

    r = lax.broadcasted_iota(jnp.int32, (t, PAIR), 0)
    c = lax.broadcasted_iota(jnp.int32, (t, PAIR), 1)
    r_low = (r % BF16_EXACT).astype(F32)
    r_high = (r - r % BF16_EXACT).astype(F32)
    kaug_ref[...] = jnp.where(c < SLOPE_TERMS, r_low, jnp.where(c < 2 * SLOPE_TERMS, r_high, 0.0)).astype(BF16)
    rest = jnp.full((PAIR, t), slope, F32)
    wrow = lax.broadcasted_iota(jnp.int32, (PAIR, t), 0)
    waug = jnp.zeros((PAIR, t), F32)
    for term in range(SLOPE_TERMS):
        part = rest.astype(BF16).astype(F32)
        waug = jnp.where((wrow == term) | (wrow == SLOPE_TERMS + term), part, waug)
        rest = rest - part
    waug = waug.astype(BF16)
    dim = lax.broadcasted_iota(jnp.int32, (PAIR, t), 0)

    lam = (jnp.exp(jnp.sum(lq1_ref[...] * lk1_ref[...], axis=-1, keepdims=True))
           - jnp.exp(jnp.sum(lq2_ref[...] * lk2_ref[...], axis=-1, keepdims=True))
           + lambda_init)
    neg = jnp.full((1, t), -jnp.inf, F32)
    half = t // 2
    s_refs = ((sa_ref, ma_ref), (sb_ref, mb_ref))

    for qi in range(nchunks):
        rows = slice(qi * t, (qi + 1) * t)
        acc = acc_ref.at[qi % 2]
        qt = q_ref[0, rows, :].astype(F32).T * (A_HEAD_DIM ** -0.5 * LOG2E)
        qt_maps = (jnp.concatenate([jnp.where(dim < A_HEAD_DIM, qt, 0.0).astype(BF16), waug], axis=0),
                   jnp.concatenate([jnp.where(dim < A_HEAD_DIM, 0.0, qt).astype(BF16), waug], axis=0))
        acc[...] = jnp.zeros(acc.shape, F32)

        def scores(j, buf, qt_maps=qt_maps):
            s_ref, max_ref = buf
            kc = k_ref[0, pl.ds(pl.multiple_of(j * t, t), t), :]
            kc = jnp.concatenate([kc, kaug_ref[...]], axis=1)
            for mp in range(2):
                s = jnp.dot(kc, qt_maps[mp], preferred_element_type=F32)
                s_ref[mp] = s
                max_ref[mp] = jnp.max(s, axis=0, keepdims=True)

        def update(j, buf, stats, qi=qi, acc=acc):
            s_ref, max_ref = buf
            vt = vt_ref[j]
            shift = slope * jnp.asarray((j - qi) * t, F32)
            out = []
            for mp in range(2):
                m_new = jnp.maximum(stats[mp], max_ref[mp] + shift)
                p = jnp.exp2(s_ref[mp] - (m_new - shift)).astype(BF16)
                alpha = jnp.exp2(stats[mp] - m_new)
                acc[mp] = alpha * acc[mp] + jnp.dot(vt, p, preferred_element_type=F32)
                out.append(m_new)
            return tuple(out)

        def scores_diag(buf, qi=qi, qt_maps=qt_maps):
            s_ref, max_ref = buf
            k_lo = jnp.concatenate([k_ref[0, qi * t:qi * t + half, :], kaug_ref[0:half, :]], axis=1)
            k_hi = jnp.concatenate([k_ref[0, qi * t + half:(qi + 1) * t, :], kaug_ref[half:, :]], axis=1)
            for mp in range(2):
                s_lo = jnp.dot(k_lo, qt_maps[mp], preferred_element_type=F32) + mask_ref[0:half, :]
                s_hi = jnp.dot(k_hi, qt_maps[mp][:, half:], preferred_element_type=F32) \
                    + mask_ref[half:, half:]
                s_ref[mp, 0:half, :] = s_lo
                s_ref[mp, half:, half:] = s_hi
                max_lo = jnp.max(s_lo, axis=0, keepdims=True)
                max_ref[mp, :, 0:half] = max_lo[:, :half]
                max_ref[mp, :, half:] = jnp.maximum(max_lo[:, half:], jnp.max(s_hi, axis=0, keepdims=True))

        def update_diag(buf, stats, qi=qi, acc=acc):
            s_ref, max_ref = buf
            for mp in range(2):
                m_new = jnp.maximum(stats[mp], max_ref[mp])
                p_lo = jnp.exp2(s_ref[mp, 0:half, :] - m_new).astype(BF16)
                p_hi = jnp.exp2(s_ref[mp, half:, half:] - m_new[:, half:]).astype(BF16)
                alpha = jnp.exp2(stats[mp] - m_new)
                pv = jnp.dot(vt_ref[qi, :, 0:half], p_lo, preferred_element_type=F32)
                pv_hi = jnp.dot(vt_ref[qi, :, half:], p_hi, preferred_element_type=F32)
                acc[mp, :, 0:half] = alpha[:, :half] * acc[mp, :, 0:half] + pv[:, :half]
                acc[mp, :, half:] = alpha[:, half:] * acc[mp, :, half:] + (pv[:, half:] + pv_hi)

        first, second = s_refs

        def pair(i, stats, scores=scores, update=update, first=first, second=second):
            j = 2 * i
            scores(j + 1, second)
            stats = update(j, first, stats)
            scores(j + 2, first)
            return update(j + 1, second, stats)

        idx = jnp.int32
        if qi == 0:
            scores_diag(first)
            update_diag(first, (neg, neg))
        else:
            scores(idx(0), first)
            stats = lax.fori_loop(0, (qi - 1) // 2, pair, (neg, neg))
            if qi % 2 == 1:
                scores_diag(second)
                update_diag(second, update(idx(qi - 1), first, stats))
            else:
                scores(idx(qi - 1), second)
                stats = update(idx(qi - 2), first, stats)
                scores_diag(first)
                update_diag(first, update(idx(qi - 1), second, stats))
        if qi % 2 == 0:
            s_refs = (second, first)

        a1, a2 = acc[0], acc[1]
        ot = a1[:PAIR] / a1[PAIR:PAIR + 1] - lam * (a2[:PAIR] / a2[PAIR:PAIR + 1])
        ot = ot * lax.rsqrt(jnp.mean(ot * ot, axis=0, keepdims=True) + EPS)
        ot = ot * subln_ref[...] * (1.0 - lambda_init)
        o_ref[0, rows, :] = ot.T.astype(o_ref.dtype)


def _diff_attention(p3, slopes, lq1, lk1, lq2, lk2, subln_col, lambda_init, t):
    bsz, seq, _ = p3.shape
    vec = lambda n: pl.BlockSpec((1, n), lambda b, h: (0, 0))
    kernel = functools.partial(_diff_attn_kernel, t=t, lambda_init=lambda_init)
    return pl.pallas_call(
        kernel,
        grid=(bsz, A_HEADS),
        in_specs=[
            pl.BlockSpec(memory_space=pltpu.SMEM),
            pl.BlockSpec((1, seq, PAIR), lambda b, h: (b, 0, h)),
            pl.BlockSpec((1, seq, PAIR), lambda b, h: (b, 0, A_HEADS + h)),
            pl.BlockSpec((1, seq, PAIR), lambda b, h: (b, 0, 2 * A_HEADS + h)),
            vec(A_HEAD_DIM), vec(A_HEAD_DIM), vec(A_HEAD_DIM), vec(A_HEAD_DIM),
            pl.BlockSpec((PAIR, 1), lambda b, h: (0, 0)),
        ],
        out_specs=pl.BlockSpec((1, seq, PAIR), lambda b, h: (b, 0, h)),
        out_shape=jax.ShapeDtypeStruct((bsz, seq, A_COLS), BF16),
        scratch_shapes=[pltpu.VMEM((seq // t, PAIR + BF16_ROWS, t), BF16), pltpu.VMEM((t, t), F32),
                        pltpu.VMEM((t, PAIR), BF16),
                        pltpu.VMEM((2, t, t), F32), pltpu.VMEM((2, t, t), F32),
                        pltpu.VMEM((2, 1, t), F32), pltpu.VMEM((2, 1, t), F32),
                        pltpu.VMEM((2, 2, PAIR + BF16_ROWS, t), F32)],
        compiler_params=pltpu.CompilerParams(
            dimension_semantics=("parallel", "parallel"), vmem_limit_bytes=VMEM_LIMIT),
        name="diff_attn",
    )(slopes, p3, p3, p3, lq1, lk1, lq2, lk2, subln_col)


def _dilated_kernel(q_ref, k_ref, v_ref, kprev_ref, vprev_ref, o_ref, lse_ref, bias_ref, o_scr, lse_scr,
                    *maybe_order_scr, dil, slopes, tiles_per_seq):
    i = pl.program_id(0)
    blocks = q_ref.shape[2] // STEPS
    tm = o_ref.shape[0]
    two_pass = bool(maybe_order_scr)
    if two_pass:
        assert dil == SPLIT * SPLIT and blocks == 1
        order_scr, = maybe_order_scr

    @pl.when(i == 0)
    def _():
        qi = lax.broadcasted_iota(jnp.int32, (STEPS, 2 * STEPS), 0)
        kj = lax.broadcasted_iota(jnp.int32, (STEPS, 2 * STEPS), 1)
        step = qi + STEPS - kj
        window = (step >= 0) & (step <= STEPS)
        dist = (step * dil).astype(F32)
        for hd in range(B_HEADS):
            alibi = -float(slopes[hd]) * dist
            bias_ref[0, hd] = jnp.where(window & (kj >= STEPS), alibi, -jnp.inf)
            bias_ref[1, hd] = jnp.where(window, alibi, -jnp.inf)

    lane = lax.broadcasted_iota(jnp.int32, (STEPS, PAIR), 1)
    low = lane < B_HEAD_DIM
    low_keys = lax.broadcasted_iota(jnp.int32, (2 * STEPS, PAIR), 1) < B_HEAD_DIM
    ones_even = jnp.where(low_keys, 1.0, 0.0).astype(BF16)
    ones_odd = jnp.where(low_keys, 0.0, 1.0).astype(BF16)

    first_variant = (i % tiles_per_seq != 0).astype(jnp.int32)
    for u in range(dil * blocks):
        r, nl = divmod(u, blocks)
        variant = 1 if nl > 0 else first_variant
        cur = slice(nl * STEPS, (nl + 1) * STEPS)
        before = slice((nl - 1) * STEPS, nl * STEPS)
        q = q_ref[0, r, cur, :].astype(F32) * (B_HEAD_DIM ** -0.5)
        k_before = k_ref[0, r, before, :] if nl > 0 else kprev_ref[0, r]
        v_before = v_ref[0, r, before, :] if nl > 0 else vprev_ref[0, r]
        kk = jnp.concatenate([k_before, k_ref[0, r, cur, :]], axis=0)
        vv = jnp.concatenate([v_before, v_ref[0, r, cur, :]], axis=0)
        start = nl * (STEPS * dil) + r
        if two_pass:
            rows = pl.ds((r % SPLIT) * (tm // SPLIT) + r // SPLIT, STEPS, stride=SPLIT)
        else:
            rows = pl.ds(start, STEPS, stride=dil) if dil > 1 else pl.ds(start, STEPS)
        for pr in range(B_HEADS // 2):
            cols = slice(pr * PAIR, (pr + 1) * PAIR)
            qp, kp = q[:, cols], kk[:, cols]
            vp = vv[:, cols].astype(F32)
            v_blocks = jnp.concatenate([
                jnp.concatenate([jnp.where(low_keys, vp, 0.0).astype(BF16), ones_even], axis=1),
                jnp.concatenate([jnp.where(low_keys, 0.0, vp).astype(BF16), ones_odd], axis=1)], axis=0)
            probs, maxes = [], []
            for par in range(2):
                qm = jnp.where(low, qp, 0.0) if par == 0 else jnp.where(low, 0.0, qp)
                s = _nt_dot(qm.astype(BF16), kp) + bias_ref[variant, 2 * pr + par]
                m = jnp.max(s, axis=-1, keepdims=True)
                probs.append(jnp.exp(s - m).astype(BF16))
                maxes.append(m)
            o_den = jnp.dot(jnp.concatenate(probs, axis=1), v_blocks, preferred_element_type=F32)
            den = o_den[:, PAIR:]
            o_scr[pr, rows, :] = o_den[:, :PAIR] / den
            lse_scr[pr, rows, :] = jnp.where(low, maxes[0], maxes[1]) + jnp.log(den)

    for src, dst in ((o_scr, o_ref), (lse_scr, lse_ref)):
        for pr in range(B_HEADS // 2):
            if two_pass:
                part = tm // SPLIT
                for r in range(SPLIT):
                    order_scr[pr, pl.ds(r, part, stride=SPLIT), :] = src[pr, r * part:(r + 1) * part, :]
            ordered = order_scr if two_pass else src
            dst[:, pr * PAIR:(pr + 1) * PAIR] = ordered[pr].astype(dst.dtype)


def _dilated_attention(pb, bsz, seq, dil):
    tiles, _, per_res, _ = pb.shape
    tm = dil * per_res
    tile_blk = (1, dil, per_res, B_COLS)
    prev_blk = (1, dil, STEPS, B_COLS)
    last = per_res // STEPS - 1
    prev = lambda i: jnp.maximum(i - 1, 0)
    kernel = functools.partial(_dilated_kernel, dil=dil, slopes=_alibi_slopes(B_HEADS),
                               tiles_per_seq=tiles // bsz)
    out_block = pl.BlockSpec((tm, B_COLS), lambda i: (i, 0))
    return pl.pallas_call(
        kernel,
        grid=(tiles,),
        in_specs=[
            pl.BlockSpec(tile_blk, lambda i: (i, 0, 0, 0)),
            pl.BlockSpec(tile_blk, lambda i: (i, 0, 0, 1)),
            pl.BlockSpec(tile_blk, lambda i: (i, 0, 0, 2)),
            pl.BlockSpec(prev_blk, lambda i: (prev(i), 0, last, 1)),
            pl.BlockSpec(prev_blk, lambda i: (prev(i), 0, last, 2)),
        ],
        out_specs=[out_block, out_block],
        out_shape=[jax.ShapeDtypeStruct((bsz * seq, B_COLS), BF16),
                   jax.ShapeDtypeStruct((bsz * seq, B_COLS), F32)],
        scratch_shapes=[pltpu.VMEM((2, B_HEADS, STEPS, 2 * STEPS), F32),
                        pltpu.VMEM((B_HEADS // 2, tm, PAIR), F32), pltpu.VMEM((B_HEADS // 2, tm, PAIR), F32)]
        + [pltpu.VMEM((B_HEADS // 2, tm, PAIR), F32)] * (dil == SPLIT * SPLIT),
        compiler_params=pltpu.CompilerParams(
            dimension_semantics=("arbitrary",), vmem_limit_bytes=VMEM_LIMIT),
        name=f"dilated_attn_d{dil}",
    )(pb, pb, pb, pb, pb)


MERGE_SLABS = 2


def _merge_kernel(oa_ref, o0_ref, o1_ref, o2_ref, l0_ref, l1_ref, l2_ref, ga_ref, gb_ref, x_ref,
                  wpa_ref, wpb_ref, wo_ref, g_ref, b_ref, out_ref):
    slab = out_ref.shape[0] // MERGE_SLABS
    for part in range(MERGE_SLABS):
        rows = slice(part * slab, (part + 1) * slab)
        l0, l1, l2 = l0_ref[rows, :], l1_ref[rows, :], l2_ref[rows, :]
        mx = jnp.maximum(jnp.maximum(l0, l1), l2)
        e0, e1, e2 = jnp.exp(l0 - mx), jnp.exp(l1 - mx), jnp.exp(l2 - mx)
        ob = (e0 * o0_ref[rows, :].astype(F32) + e1 * o1_ref[rows, :].astype(F32)
              + e2 * o2_ref[rows, :].astype(F32)) / (e0 + e1 + e2)
        ya = jnp.dot(oa_ref[rows, :], wpa_ref[...], preferred_element_type=F32)
        yb = jnp.dot(ob.astype(BF16), wpb_ref[...], preferred_element_type=F32)
        y = ga_ref[rows, :].astype(F32) * ya + gb_ref[rows, :].astype(F32) * yb
        z = ALPHA * x_ref[rows, :] + jnp.dot(y.astype(BF16), wo_ref[...], preferred_element_type=F32)
        out_ref[rows, :] = _layer_norm(z, g_ref[...], b_ref[...])


def _merge(oa, obs, lses, gates, x2, wpa, wpb, wo, g, b, tm):
    m = x2.shape[0]
    row = lambda w: pl.BlockSpec((tm, w), lambda i: (i, 0))
    full = lambda a: pl.BlockSpec(a.shape, lambda i: (0, 0))
    return pl.pallas_call(
        _merge_kernel,
        grid=(m // tm,),
        in_specs=[row(A_COLS), row(B_COLS), row(B_COLS), row(B_COLS), row(B_COLS), row(B_COLS), row(B_COLS),
                  pl.BlockSpec((tm, D_MODEL), lambda i: (i, 0)), pl.BlockSpec((tm, D_MODEL), lambda i: (i, 1)),
                  row(D_MODEL), full(wpa), full(wpb), full(wo), full(g), full(b)],
        out_specs=row(D_MODEL),
        out_shape=jax.ShapeDtypeStruct((m, D_MODEL), F32),
        compiler_params=pltpu.CompilerParams(dimension_semantics=("parallel",), vmem_limit_bytes=VMEM_LIMIT),
        name="merge_ln1",
    )(oa, *obs, *lses, gates, gates, x2, wpa, wpb, wo, g, b)


HALO = BF16_ROWS


def _ffn_kernel(x_ref, halo_ref, wup_ref, wconv_ref, bconv_ref, wd_ref, g_ref, b_ref,
                out_ref, xcat_ref, acc_ref, *, tiles_per_seq):
    i = pl.program_id(0)
    halo = jnp.where(i % tiles_per_seq == 0, 0.0, halo_ref[...])
    xcat_ref[0:HALO, :] = halo.astype(BF16)
    xcat_ref[HALO:, :] = x_ref[...].astype(BF16)
    xcat = xcat_ref[...]

    def conv(col0, width):
        cols = slice(col0, col0 + width)
        hfull = jnp.dot(xcat, wup_ref[:, cols], preferred_element_type=F32)
        out = bconv_ref[:, cols] + wconv_ref[CONV_WIDTH - 1:CONV_WIDTH, cols] * hfull[HALO:, :]
        for back in range(1, CONV_WIDTH):
            shifted = pltpu.roll(hfull, back, axis=0)[HALO:, :]
            out = out + wconv_ref[CONV_WIDTH - 1 - back:CONV_WIDTH - back, cols] * shifted
        return out

    for col0, width in FF_CHUNKS:
        a = conv(col0, width)
        gv = conv(D_FF + col0, width)
        f = 0.5 * a * (1.0 + lax.erf(a * (2.0 ** -0.5))) * gv
        y = jnp.dot(f.astype(BF16), wd_ref[col0:col0 + width, :], preferred_element_type=F32)
        if col0 == 0:
            acc_ref[...] = y
        else:
            acc_ref[...] += y

    z = ALPHA * x_ref[...] + acc_ref[...]
    out_ref[...] = _layer_norm(z, g_ref[...], b_ref[...])


def _ffn(x1, w_up, w_conv, b_conv, w_down, g, b, seq, tm):
    m = x1.shape[0]
    halo_blocks = tm // HALO
    kernel = functools.partial(_ffn_kernel, tiles_per_seq=seq // tm)
    full = lambda a: pl.BlockSpec(a.shape, lambda i: (0, 0))
    once = lambda a: pl.BlockSpec(a.shape, lambda i: (0, 0), pipeline_mode=pl.Buffered(1))
    return pl.pallas_call(
        kernel,
        grid=(m // tm,),
        in_specs=[
            pl.BlockSpec((tm, D_MODEL), lambda i: (i, 0)),
            pl.BlockSpec((HALO, D_MODEL), lambda i: (jnp.maximum(i * halo_blocks - 1, 0), 0)),
            once(w_up), full(w_conv), full(b_conv), once(w_down), full(g), full(b),
        ],
        out_specs=pl.BlockSpec((tm, D_MODEL), lambda i: (i, 0)),
        out_shape=jax.ShapeDtypeStruct((m, D_MODEL), F32),
        scratch_shapes=[pltpu.VMEM((HALO + tm, D_MODEL), BF16), pltpu.VMEM((tm, D_MODEL), F32)],
        compiler_params=pltpu.CompilerParams(dimension_semantics=("parallel",), vmem_limit_bytes=VMEM_LIMIT),
        name="ffn_ln2",
    )(x1, x1, w_up, w_conv, b_conv, w_down, g, b)


def kernel(x, w_in, b_gate, lambda_q1, lambda_k1, lambda_q2, lambda_k2, subln_w, w_pa, w_pb, w_o, ln1_g, ln1_b,
           w_up, w_conv, b_conv, w_down, ln2_g, ln2_b):
    bsz, seq, d = x.shape
    assert (seq, d) == (4096, D_MODEL) and w_in.shape[0] == DEPTH
    slopes_a = jnp.asarray(_alibi_slopes(A_HEADS))
    for l in range(DEPTH):
        lambda_init = 0.8 - 0.6 * math.exp(-0.3 * l)
        x2 = x.reshape(bsz * seq, d)
        wb = w_in[l].astype(BF16)
        proj_a, gates, xb = _projection_wide(x2, wb, b_gate[l][None, :], tm=PROJ_ROWS)
        proj_groups = _projection_groups(xb, wb, tm=PROJ_ROWS)
        oa = _diff_attention(proj_a.reshape(bsz, seq, 3 * A_COLS), slopes_a, lambda_q1[l][None],
                             lambda_k1[l][None], lambda_q2[l][None], lambda_k2[l][None], subln_w[l][:, None],
                             lambda_init, t=512)
        obs, lses = [], []
        for pb, (window, dil) in zip(proj_groups, B_PATTERNS):
            assert window // dil == STEPS
            o, lse = _dilated_attention(pb, bsz, seq, dil)
            obs.append(o)
            lses.append(lse)

        x1 = _merge(oa.reshape(bsz * seq, A_COLS), obs, lses, gates, x2,
                    w_pa[l].astype(BF16), w_pb[l].astype(BF16), w_o[l].astype(BF16),
                    ln1_g[l][None], ln1_b[l][None], tm=512)
        x2 = _ffn(x1, w_up[l].astype(BF16), w_conv[l], b_conv[l][None], w_down[l].astype(BF16),
                  ln2_g[l][None], ln2_b[l][None], seq, tm=1024)
        x = x2.reshape(bsz, seq, d)
    return x
```

```python
import functools
import math

import numpy as np
import jax
import jax.numpy as jnp
from jax import lax
from jax.experimental import pallas as pl
from jax.experimental.pallas import tpu as pltpu

BF16 = jnp.bfloat16
F32 = jnp.float32

D_MODEL = 1024
A_HEADS = 8
A_HEAD_DIM = 64
B_PATTERNS = ((128, 1), (512, 4), (2048, 16))
B_HEADS = 8
B_HEAD_DIM = 64
D_FF = 2816
CONV_WIDTH = 3
EPS = 1e-5
DEPTH = 1
ALPHA = (2.0 * DEPTH) ** 0.25

A_COLS = A_HEADS * 2 * A_HEAD_DIM
B_COLS = B_HEADS * B_HEAD_DIM
N_GROUPS = len(B_PATTERNS)
QKV_COLS = 3 * A_COLS + 3 * N_GROUPS * B_COLS
PAIR = 2 * A_HEAD_DIM
STEPS = 128
BF16_ROWS = 16
LOG2E = math.log2(math.e)
BF16_EXACT = 256
SLOPE_TERMS = 3
PROJ_ROWS = 2048
MXU_WIDTH = 256
FF_CHUNKS = ((0, 6 * MXU_WIDTH), (6 * MXU_WIDTH, 5 * MXU_WIDTH))

VMEM_LIMIT = 56 * 1024 * 1024


def _alibi_slopes(n):
    return np.power(np.float32(2.0), -8.0 * (np.arange(n, dtype=np.float32) + 1) / n).astype(np.float32)


def _nt_dot(a, b):
    return lax.dot_general(a, b, (((1,), (1,)), ((), ())), preferred_element_type=F32)


def _layer_norm(z, g, b):
    mu = jnp.mean(z, axis=-1, keepdims=True)
    zc = z - mu
    var = jnp.mean(zc * zc, axis=-1, keepdims=True)
    return zc * lax.rsqrt(var + EPS) * g + b


WIDE_COLS = 1024
A_TILES = 3 * A_COLS // WIDE_COLS
GROUP_TILES = 3
SPLIT = 4


def _proj_wide_kernel(x_ref, wa_ref, wg_lo_ref, wg_hi_ref, b_ref, pa_ref, g_ref, xb_ref):
    j = pl.program_id(1)

    @pl.when(j == 0)
    def _():
        xb_ref[...] = x_ref[...].astype(BF16)

    @pl.when(j < A_TILES)
    def _():
        pa_ref[...] = jnp.dot(xb_ref[...], wa_ref[...], preferred_element_type=F32).astype(pa_ref.dtype)

    @pl.when(j >= A_TILES)
    def _():
        wg = jnp.concatenate([wg_lo_ref[...], wg_hi_ref[...]], axis=1)
        z = jnp.dot(xb_ref[...], wg, preferred_element_type=F32) + b_ref[...]
        g_ref[...] = (0.5 * jnp.tanh(0.5 * z) + 0.5).astype(g_ref.dtype)


def _projection_wide(x2, wb, b_gate, tm):
    m, k = x2.shape
    tn = WIDE_COLS
    gate_tiles = b_gate.shape[1] // tn
    gate_first = QKV_COLS // (tn // 2)
    clamp = lambda j, first, count: jnp.clip(j - first, 0, count - 1)
    return pl.pallas_call(
        _proj_wide_kernel,
        grid=(m // tm, A_TILES + gate_tiles),
        in_specs=[pl.BlockSpec((tm, k), lambda i, j: (i, 0)),
                  pl.BlockSpec((k, tn), lambda i, j: (0, clamp(j, 0, A_TILES))),
                  pl.BlockSpec((k, tn // 2), lambda i, j: (0, gate_first + 2 * clamp(j, A_TILES, gate_tiles))),
                  pl.BlockSpec((k, tn // 2), lambda i, j: (0, gate_first + 2 * clamp(j, A_TILES, gate_tiles) + 1)),
                  pl.BlockSpec((1, tn), lambda i, j: (0, clamp(j, A_TILES, gate_tiles)))],
        out_specs=[pl.BlockSpec((tm, tn), lambda i, j: (i, clamp(j, 0, A_TILES))),
                   pl.BlockSpec((tm, tn), lambda i, j: (i, clamp(j, A_TILES, gate_tiles))),
                   pl.BlockSpec((tm, k), lambda i, j: (i, 0))],
        out_shape=[jax.ShapeDtypeStruct((m, A_TILES * tn), BF16),
                   jax.ShapeDtypeStruct((m, gate_tiles * tn), BF16),
                   jax.ShapeDtypeStruct((m, k), BF16)],
        compiler_params=pltpu.CompilerParams(
            dimension_semantics=("parallel", "arbitrary"), vmem_limit_bytes=VMEM_LIMIT),
        name="proj_wide",
    )(x2, wb, wb, wb, b_gate)


def _proj_groups_kernel(x_ref, w_ref, *rest):
    group_refs, (acc_ref, tmp_ref) = rest[:N_GROUPS], rest[N_GROUPS:]
    j = pl.program_id(1)
    lane_tiles, tm, _ = acc_ref.shape
    for g, (o_ref, (_, dil)) in enumerate(zip(group_refs, B_PATTERNS)):
        @pl.when((j >= g * GROUP_TILES) & (j < (g + 1) * GROUP_TILES))
        def _(o_ref=o_ref, dil=dil):
            acc = jnp.dot(x_ref[...], w_ref[...], preferred_element_type=F32)
            if dil == 1:
                o_ref[0, 0] = acc.astype(o_ref.dtype)
                return
            for c in range(lane_tiles):
                acc_ref[c] = acc[:, c * PAIR:(c + 1) * PAIR]
            for c in range(lane_tiles):
                cols = slice(c * PAIR, (c + 1) * PAIR)
                if dil == SPLIT:
                    for r in range(dil):
                        o_ref[0, r, :, cols] = acc_ref[c, pl.ds(r, tm // dil, stride=dil), :].astype(o_ref.dtype)
                else:
                    assert dil == SPLIT * SPLIT
                    part = tm // SPLIT
                    for r in range(SPLIT):
                        tmp_ref[c, r * part:(r + 1) * part, :] = acc_ref[c, pl.ds(r, part, stride=SPLIT), :]
                    for r in range(dil):
                        start = (r % SPLIT) * part + r // SPLIT
                        o_ref[0, r, :, cols] = \
                            tmp_ref[c, pl.ds(start, tm // dil, stride=SPLIT), :].astype(o_ref.dtype)


def _projection_groups(xb, w_in, tm):
    m, k = xb.shape
    tn = B_COLS
    clamp = lambda j, first: jnp.clip(j - first, 0, GROUP_TILES - 1)
    w_block = lambda j: 3 * A_COLS // tn + (j % GROUP_TILES) * N_GROUPS + j // GROUP_TILES
    return pl.pallas_call(
        _proj_groups_kernel,
        grid=(m // tm, N_GROUPS * GROUP_TILES),
        in_specs=[pl.BlockSpec((tm, k), lambda i, j: (i, 0)),
                  pl.BlockSpec((k, tn), lambda i, j: (0, w_block(j)))],
        out_specs=[pl.BlockSpec((1, dil, tm // dil, tn), lambda i, j, g=g: (i, 0, 0, clamp(j, g * GROUP_TILES)))
                   for g, (_, dil) in enumerate(B_PATTERNS)],
        out_shape=[jax.ShapeDtypeStruct((m // tm, dil, tm // dil, GROUP_TILES * tn), BF16)
                   for _, dil in B_PATTERNS],
        scratch_shapes=[pltpu.VMEM((tn // PAIR, tm, PAIR), F32), pltpu.VMEM((tn // PAIR, tm, PAIR), F32)],
        compiler_params=pltpu.CompilerParams(
            dimension_semantics=("parallel", "arbitrary"), vmem_limit_bytes=VMEM_LIMIT),
        name="proj_groups",
    )(xb, w_in)


def _diff_attn_kernel(slopes_ref, q_ref, k_ref, v_ref, lq1_ref, lk1_ref, lq2_ref, lk2_ref, subln_ref,
                      o_ref, vt_ref, mask_ref, kaug_ref, sa_ref, sb_ref, ma_ref, mb_ref, acc_ref, *,
                      t, lambda_init):
    slope = slopes_ref[pl.program_id(1)] * LOG2E
    nchunks = vt_ref.shape[0]

    @pl.when((pl.program_id(0) == 0) & (pl.program_id(1) == 0))
    def _():
        krow = lax.broadcasted_iota(jnp.int32, (t, t), 0)
        qcol = lax.broadcasted_iota(jnp.int32, (t, t), 1)
        mask_ref[...] = jnp.where(krow <= qcol, 0.0, -jnp.inf)
        r = lax.broadcasted_iota(jnp.int32, (t, PAIR), 0)
        c = lax.broadcasted_iota(jnp.int32, (t, PAIR), 1)
        r_low = (r % BF16_EXACT).astype(F32)
        r_high = (r - r % BF16_EXACT).astype(F32)
        kaug_ref[...] = jnp.where(c < SLOPE_TERMS, r_low,
                                  jnp.where(c < 2 * SLOPE_TERMS, r_high, 0.0)).astype(BF16)

    rest = jnp.full((PAIR, t), slope, F32)
    wrow = lax.broadcasted_iota(jnp.int32, (PAIR, t), 0)
    waug = jnp.zeros((PAIR, t), F32)
    for term in range(SLOPE_TERMS):
        part = rest.astype(BF16).astype(F32)
        waug = jnp.where((wrow == term) | (wrow == SLOPE_TERMS + term), part, waug)
        rest = rest - part
    waug = waug.astype(BF16)
    dim = lax.broadcasted_iota(jnp.int32, (PAIR, t), 0)

    lam = (jnp.exp(jnp.sum(lq1_ref[...] * lk1_ref[...], axis=-1, keepdims=True))
           - jnp.exp(jnp.sum(lq2_ref[...] * lk2_ref[...], axis=-1, keepdims=True))
           + lambda_init)
    neg = jnp.full((1, t), -jnp.inf, F32)
    half = t // 2
    s_refs = ((sa_ref, ma_ref), (sb_ref, mb_ref))

    for qi in range(nchunks):
        rows = slice(qi * t, (qi + 1) * t)
        acc = acc_ref.at[qi % 2]
        vt_ref[qi, 0:PAIR, :] = v_ref[0, rows, :].astype(F32).T.astype(BF16)
        vt_ref[qi, PAIR:, :] = jnp.ones((BF16_ROWS, t), BF16)
        qt = q_ref[0, rows, :].astype(F32).T * (A_HEAD_DIM ** -0.5 * LOG2E)
        qt_maps = (jnp.concatenate([jnp.where(dim < A_HEAD_DIM, qt, 0.0).astype(BF16), waug], axis=0),
                   jnp.concatenate([jnp.where(dim < A_HEAD_DIM, 0.0, qt).astype(BF16), waug], axis=0))
        acc[...] = jnp.zeros(acc.shape, F32)

        def scores(j, buf, qt_maps=qt_maps):
            s_ref, max_ref = buf
            kc = k_ref[0, pl.ds(pl.multiple_of(j * t, t), t), :]
            kc = jnp.concatenate([kc, kaug_ref[...]], axis=1)
            for mp in range(2):
                s = jnp.dot(kc, qt_maps[mp], preferred_element_type=F32)
                s_ref[mp] = s
                max_ref[mp] = jnp.max(s, axis=0, keepdims=True)

        def update(j, buf, stats, qi=qi, acc=acc):
            s_ref, max_ref = buf
            vt = vt_ref[j]
            shift = slope * jnp.asarray((j - qi) * t, F32)
            out = []
            for mp in range(2):
                m_new = jnp.maximum(stats[mp], max_ref[mp] + shift)
                p = jnp.exp2(s_ref[mp] - (m_new - shift)).astype(BF16)
                alpha = jnp.exp2(stats[mp] - m_new)
                acc[mp] = alpha * acc[mp] + jnp.dot(vt, p, preferred_element_type=F32)
                out.append(m_new)
            return tuple(out)

        def scores_diag(buf, qi=qi, qt_maps=qt_maps):
            s_ref, max_ref = buf
            k_lo = jnp.concatenate([k_ref[0, qi * t:qi * t + half, :], kaug_ref[0:half, :]], axis=1)
            k_hi = jnp.concatenate([k_ref[0, qi * t + half:(qi + 1) * t, :], kaug_ref[half:, :]], axis=1)
            for mp in range(2):
                s_lo = jnp.dot(k_lo, qt_maps[mp], preferred_element_type=F32) + mask_ref[0:half, :]
                s_hi = jnp.dot(k_hi, qt_maps[mp][:, half:], preferred_element_type=F32) \
                    + mask_ref[half:, half:]
                s_ref[mp, 0:half, :] = s_lo
                s_ref[mp, half:, half:] = s_hi
                max_lo = jnp.max(s_lo, axis=0, keepdims=True)
                max_ref[mp, :, 0:half] = max_lo[:, :half]
                max_ref[mp, :, half:] = jnp.maximum(max_lo[:, half:], jnp.max(s_hi, axis=0, keepdims=True))

        def update_diag(buf, stats, qi=qi, acc=acc):
            s_ref, max_ref = buf
            for mp in range(2):
                m_new = jnp.maximum(stats[mp], max_ref[mp])
                p_lo = jnp.exp2(s_ref[mp, 0:half, :] - m_new).astype(BF16)
                p_hi = jnp.exp2(s_ref[mp, half:, half:] - m_new[:, half:]).astype(BF16)
                alpha = jnp.exp2(stats[mp] - m_new)
                pv = jnp.dot(vt_ref[qi, :, 0:half], p_lo, preferred_element_type=F32)
                pv_hi = jnp.dot(vt_ref[qi, :, half:], p_hi, preferred_element_type=F32)
                acc[mp, :, 0:half] = alpha[:, :half] * acc[mp, :, 0:half] + pv[:, :half]
                acc[mp, :, half:] = alpha[:, half:] * acc[mp, :, half:] + (pv[:, half:] + pv_hi)

        first, second = s_refs

        def pair(i, stats, scores=scores, update=update, first=first, second=second):
            j = 2 * i
            scores(j + 1, second)
            stats = update(j, first, stats)
            scores(j + 2, first)
            return update(j + 1, second, stats)

        idx = jnp.int32
        if qi == 0:
            scores_diag(first)
            update_diag(first, (neg, neg))
        else:
            scores(idx(0), first)
            stats = lax.fori_loop(0, (qi - 1) // 2, pair, (neg, neg))
            if qi % 2 == 1:
                scores_diag(second)
                update_diag(second, update(idx(qi - 1), first, stats))
            else:
                scores(idx(qi - 1), second)
                stats = update(idx(qi - 2), first, stats)
                scores_diag(first)
                update_diag(first, update(idx(qi - 1), second, stats))
        if qi % 2 == 0:
            s_refs = (second, first)

        a1, a2 = acc[0], acc[1]
        ot = a1[:PAIR] / a1[PAIR:PAIR + 1] - lam * (a2[:PAIR] / a2[PAIR:PAIR + 1])
        ot = ot * lax.rsqrt(jnp.mean(ot * ot, axis=0, keepdims=True) + EPS)
        ot = ot * subln_ref[...] * (1.0 - lambda_init)
        o_ref[0, rows, :] = ot.T.astype(o_ref.dtype)


def _diff_attention(p3, slopes, lq1, lk1, lq2, lk2, subln_col, lambda_init, t):
    bsz, seq, _ = p3.shape
    vec = lambda n: pl.BlockSpec((1, n), lambda b, h: (0, 0))
    kernel = functools.partial(_diff_attn_kernel, t=t, lambda_init=lambda_init)
    return pl.pallas_call(
        kernel,
        grid=(bsz, A_HEADS),
        in_specs=[
            pl.BlockSpec(memory_space=pltpu.SMEM),
            pl.BlockSpec((1, seq, PAIR), lambda b, h: (b, 0, h)),
            pl.BlockSpec((1, seq, PAIR), lambda b, h: (b, 0, A_HEADS + h)),
            pl.BlockSpec((1, seq, PAIR), lambda b, h: (b, 0, 2 * A_HEADS + h)),
            vec(A_HEAD_DIM), vec(A_HEAD_DIM), vec(A_HEAD_DIM), vec(A_HEAD_DIM),
            pl.BlockSpec((PAIR, 1), lambda b, h: (0, 0)),
        ],
        out_specs=pl.BlockSpec((1, seq, PAIR), lambda b, h: (b, 0, h)),
        out_shape=jax.ShapeDtypeStruct((bsz, seq, A_COLS), BF16),
        scratch_shapes=[pltpu.VMEM((seq // t, PAIR + BF16_ROWS, t), BF16), pltpu.VMEM((t, t), F32),
                        pltpu.VMEM((t, PAIR), BF16),
                        pltpu.VMEM((2, t, t), F32), pltpu.VMEM((2, t, t), F32),
                        pltpu.VMEM((2, 1, t), F32), pltpu.VMEM((2, 1, t), F32),
                        pltpu.VMEM((2, 2, PAIR + BF16_ROWS, t), F32)],
        compiler_params=pltpu.CompilerParams(
            dimension_semantics=("arbitrary", "arbitrary"), vmem_limit_bytes=VMEM_LIMIT),
        name="diff_attn",
    )(slopes, p3, p3, p3, lq1, lk1, lq2, lk2, subln_col)


def _dilated_kernel(q_ref, k_ref, v_ref, kprev_ref, vprev_ref, o_ref, lse_ref, bias_ref, o_scr, lse_scr,
                    *maybe_order_scr, dil, slopes, tiles_per_seq):
    i = pl.program_id(0)
    blocks = q_ref.shape[2] // STEPS
    tm = o_ref.shape[0]
    two_pass = bool(maybe_order_scr)
    if two_pass:
        assert dil == SPLIT * SPLIT and blocks == 1
        order_scr, = maybe_order_scr

    @pl.when(i == 0)
    def _():
        qi = lax.broadcasted_iota(jnp.int32, (STEPS, 2 * STEPS), 0)
        kj = lax.broadcasted_iota(jnp.int32, (STEPS, 2 * STEPS), 1)
        step = qi + STEPS - kj
        window = (step >= 0) & (step <= STEPS)
        dist = (step * dil).astype(F32)
        for hd in range(B_HEADS):
            alibi = -float(slopes[hd]) * dist
            bias_ref[0, hd] = jnp.where(window & (kj >= STEPS), alibi, -jnp.inf)
            bias_ref[1, hd] = jnp.where(window, alibi, -jnp.inf)

    lane = lax.broadcasted_iota(jnp.int32, (STEPS, PAIR), 1)
    low = lane < B_HEAD_DIM
    low_keys = lax.broadcasted_iota(jnp.int32, (2 * STEPS, PAIR), 1) < B_HEAD_DIM
    ones_even = jnp.where(low_keys, 1.0, 0.0).astype(BF16)
    ones_odd = jnp.where(low_keys, 0.0, 1.0).astype(BF16)

    first_variant = (i % tiles_per_seq != 0).astype(jnp.int32)
    for u in range(dil * blocks):
        r, nl = divmod(u, blocks)
        variant = 1 if nl > 0 else first_variant
        cur = slice(nl * STEPS, (nl + 1) * STEPS)
        before = slice((nl - 1) * STEPS, nl * STEPS)
        q = q_ref[0, r, cur, :].astype(F32) * (B_HEAD_DIM ** -0.5)
        k_before = k_ref[0, r, before, :] if nl > 0 else kprev_ref[0, r]
        v_before = v_ref[0, r, before, :] if nl > 0 else vprev_ref[0, r]
        kk = jnp.concatenate([k_before, k_ref[0, r, cur, :]], axis=0)
        vv = jnp.concatenate([v_before, v_ref[0, r, cur, :]], axis=0)
        start = nl * (STEPS * dil) + r
        if two_pass:
            rows = pl.ds((r % SPLIT) * (tm // SPLIT) + r // SPLIT, STEPS, stride=SPLIT)
        else:
            rows = pl.ds(start, STEPS, stride=dil) if dil > 1 else pl.ds(start, STEPS)
        for pr in range(B_HEADS // 2):
            cols = slice(pr * PAIR, (pr + 1) * PAIR)
            qp, kp = q[:, cols], kk[:, cols]
            vp = vv[:, cols].astype(F32)
            v_blocks = jnp.concatenate([
                jnp.concatenate([jnp.where(low_keys, vp, 0.0).astype(BF16), ones_even], axis=1),
                jnp.concatenate([jnp.where(low_keys, 0.0, vp).astype(BF16), ones_odd], axis=1)], axis=0)
            probs, maxes = [], []
            for par in range(2):
                qm = jnp.where(low, qp, 0.0) if par == 0 else jnp.where(low, 0.0, qp)
                s = _nt_dot(qm.astype(BF16), kp) + bias_ref[variant, 2 * pr + par]
                m = jnp.max(s, axis=-1, keepdims=True)
                probs.append(jnp.exp(s - m).astype(BF16))
                maxes.append(m)
            o_den = jnp.dot(jnp.concatenate(probs, axis=1), v_blocks, preferred_element_type=F32)
            den = o_den[:, PAIR:]
            o_scr[pr, rows, :] = o_den[:, :PAIR] / den
            lse_scr[pr, rows, :] = jnp.where(low, maxes[0], maxes[1]) + jnp.log(den)

    for src, dst in ((o_scr, o_ref), (lse_scr, lse_ref)):
        for pr in range(B_HEADS // 2):
            if two_pass:
                part = tm // SPLIT
                for r in range(SPLIT):
                    order_scr[pr, pl.ds(r, part, stride=SPLIT), :] = src[pr, r * part:(r + 1) * part, :]
            ordered = order_scr if two_pass else src
            dst[:, pr * PAIR:(pr + 1) * PAIR] = ordered[pr].astype(dst.dtype)


def _dilated_attention(pb, bsz, seq, dil):
    tiles, _, per_res, _ = pb.shape
    tm = dil * per_res
    tile_blk = (1, dil, per_res, B_COLS)
    prev_blk = (1, dil, STEPS, B_COLS)
    last = per_res // STEPS - 1
    prev = lambda i: jnp.maximum(i - 1, 0)
    kernel = functools.partial(_dilated_kernel, dil=dil, slopes=_alibi_slopes(B_HEADS),
                               tiles_per_seq=tiles // bsz)
    out_block = pl.BlockSpec((tm, B_COLS), lambda i: (i, 0))
    return pl.pallas_call(
        kernel,
        grid=(tiles,),
        in_specs=[
            pl.BlockSpec(tile_blk, lambda i: (i, 0, 0, 0)),
            pl.BlockSpec(tile_blk, lambda i: (i, 0, 0, 1)),
            pl.BlockSpec(tile_blk, lambda i: (i, 0, 0, 2)),
            pl.BlockSpec(prev_blk, lambda i: (prev(i), 0, last, 1)),
            pl.BlockSpec(prev_blk, lambda i: (prev(i), 0, last, 2)),
        ],
        out_specs=[out_block, out_block],
        out_shape=[jax.ShapeDtypeStruct((bsz * seq, B_COLS), BF16),
                   jax.ShapeDtypeStruct((bsz * seq, B_COLS), F32)],
        scratch_shapes=[pltpu.VMEM((2, B_HEADS, STEPS, 2 * STEPS), F32),
                        pltpu.VMEM((B_HEADS // 2, tm, PAIR), F32), pltpu.VMEM((B_HEADS // 2, tm, PAIR), F32)]
        + [pltpu.VMEM((B_HEADS // 2, tm, PAIR), F32)] * (dil == SPLIT * SPLIT),
        compiler_params=pltpu.CompilerParams(
            dimension_semantics=("arbitrary",), vmem_limit_bytes=VMEM_LIMIT),
        name=f"dilated_attn_d{dil}",
    )(pb, pb, pb, pb, pb)


MERGE_SLABS = 2


def _merge_kernel(oa_ref, o0_ref, o1_ref, o2_ref, l0_ref, l1_ref, l2_ref, ga_ref, gb_ref, x_ref,
                  wpa_ref, wpb_ref, wo_ref, g_ref, b_ref, out_ref):
    slab = out_ref.shape[0] // MERGE_SLABS
    for part in range(MERGE_SLABS):
        rows = slice(part * slab, (part + 1) * slab)
        l0, l1, l2 = l0_ref[rows, :], l1_ref[rows, :], l2_ref[rows, :]
        mx = jnp.maximum(jnp.maximum(l0, l1), l2)
        e0, e1, e2 = jnp.exp(l0 - mx), jnp.exp(l1 - mx), jnp.exp(l2 - mx)
        ob = (e0 * o0_ref[rows, :].astype(F32) + e1 * o1_ref[rows, :].astype(F32)
              + e2 * o2_ref[rows, :].astype(F32)) / (e0 + e1 + e2)
        ya = jnp.dot(oa_ref[rows, :], wpa_ref[...], preferred_element_type=F32)
        yb = jnp.dot(ob.astype(BF16), wpb_ref[...], preferred_element_type=F32)
        y = ga_ref[rows, :].astype(F32) * ya + gb_ref[rows, :].astype(F32) * yb
        z = ALPHA * x_ref[rows, :] + jnp.dot(y.astype(BF16), wo_ref[...], preferred_element_type=F32)
        out_ref[rows, :] = _layer_norm(z, g_ref[...], b_ref[...])


def _merge(oa, obs, lses, gates, x2, wpa, wpb, wo, g, b, tm):
    m = x2.shape[0]
    row = lambda w: pl.BlockSpec((tm, w), lambda i: (i, 0))
    full = lambda a: pl.BlockSpec(a.shape, lambda i: (0, 0))
    return pl.pallas_call(
        _merge_kernel,
        grid=(m // tm,),
        in_specs=[row(A_COLS), row(B_COLS), row(B_COLS), row(B_COLS), row(B_COLS), row(B_COLS), row(B_COLS),
                  pl.BlockSpec((tm, D_MODEL), lambda i: (i, 0)), pl.BlockSpec((tm, D_MODEL), lambda i: (i, 1)),
                  row(D_MODEL), full(wpa), full(wpb), full(wo), full(g), full(b)],
        out_specs=row(D_MODEL),
        out_shape=jax.ShapeDtypeStruct((m, D_MODEL), F32),
        compiler_params=pltpu.CompilerParams(dimension_semantics=("parallel",), vmem_limit_bytes=VMEM_LIMIT),
        name="merge_ln1",
    )(oa, *obs, *lses, gates, gates, x2, wpa, wpb, wo, g, b)


HALO = BF16_ROWS


def _ffn_kernel(x_ref, halo_ref, wup_ref, wconv_ref, bconv_ref, wd_ref, g_ref, b_ref,
                out_ref, xcat_ref, acc_ref, *, tiles_per_seq):
    i = pl.program_id(0)
    halo = jnp.where(i % tiles_per_seq == 0, 0.0, halo_ref[...])
    xcat_ref[0:HALO, :] = halo.astype(BF16)
    xcat_ref[HALO:, :] = x_ref[...].astype(BF16)
    xcat = xcat_ref[...]

    def conv(col0, width):
        cols = slice(col0, col0 + width)
        hfull = jnp.dot(xcat, wup_ref[:, cols], preferred_element_type=F32)
        out = bconv_ref[:, cols] + wconv_ref[CONV_WIDTH - 1:CONV_WIDTH, cols] * hfull[HALO:, :]
        for back in range(1, CONV_WIDTH):
            shifted = pltpu.roll(hfull, back, axis=0)[HALO:, :]
            out = out + wconv_ref[CONV_WIDTH - 1 - back:CONV_WIDTH - back, cols] * shifted
        return out

    for col0, width in FF_CHUNKS:
        a = conv(col0, width)
        gv = conv(D_FF + col0, width)
        f = 0.5 * a * (1.0 + lax.erf(a * (2.0 ** -0.5))) * gv
        y = jnp.dot(f.astype(BF16), wd_ref[col0:col0 + width, :], preferred_element_type=F32)
        if col0 == 0:
            acc_ref[...] = y
        else:
            acc_ref[...] += y

    z = ALPHA * x_ref[...] + acc_ref[...]
    out_ref[...] = _layer_norm(z, g_ref[...], b_ref[...])


def _ffn(x1, w_up, w_conv, b_conv, w_down, g, b, seq, tm):
    m = x1.shape[0]
    halo_blocks = tm // HALO
    kernel = functools.partial(_ffn_kernel, tiles_per_seq=seq // tm)
    full = lambda a: pl.BlockSpec(a.shape, lambda i: (0, 0))
    once = lambda a: pl.BlockSpec(a.shape, lambda i: (0, 0), pipeline_mode=pl.Buffered(1))
    return pl.pallas_call(
        kernel,
        grid=(m // tm,),
        in_specs=[
            pl.BlockSpec((tm, D_MODEL), lambda i: (i, 0)),
            pl.BlockSpec((HALO, D_MODEL), lambda i: (jnp.maximum(i * halo_blocks - 1, 0), 0)),
            once(w_up), full(w_conv), full(b_conv), once(w_down), full(g), full(b),
        ],
        out_specs=pl.BlockSpec((tm, D_MODEL), lambda i: (i, 0)),
        out_shape=jax.ShapeDtypeStruct((m, D_MODEL), F32),
        scratch_shapes=[pltpu.VMEM((HALO + tm, D_MODEL), BF16), pltpu.VMEM((tm, D_MODEL), F32)],
        compiler_params=pltpu.CompilerParams(dimension_semantics=("parallel",), vmem_limit_bytes=VMEM_LIMIT),
        name="ffn_ln2",
    )(x1, x1, w_up, w_conv, b_conv, w_down, g, b)


def kernel(x, w_in, b_gate, lambda_q1, lambda_k1, lambda_q2, lambda_k2, subln_w, w_pa, w_pb, w_o, ln1_g, ln1_b,
           w_up, w_conv, b_conv, w_down, ln2_g, ln2_b):
    bsz, seq, d = x.shape
    assert (seq, d) == (4096, D_MODEL) and w_in.shape[0] == DEPTH
    slopes_a = jnp.asarray(_alibi_slopes(A_HEADS))
    for l in range(DEPTH):
        lambda_init = 0.8 - 0.6 * math.exp(-0.3 * l)
        x2 = x.reshape(bsz * seq, d)
        wb = w_in[l].astype(BF16)
        proj_a, gates, xb = _projection_wide(x2, wb, b_gate[l][None, :], tm=PROJ_ROWS)
        proj_groups = _projection_groups(xb, wb, tm=PROJ_ROWS)
        oa = _diff_attention(proj_a.reshape(bsz, seq, 3 * A_COLS), slopes_a, lambda_q1[l][None],
                             lambda_k1[l][None], lambda_q2[l][None], lambda_k2[l][None], subln_w[l][:, None],
                             lambda_init, t=512)
        obs, lses = [], []
        for pb, (window, dil) in zip(proj_groups, B_PATTERNS):
            assert window // dil == STEPS
            o, lse = _dilated_attention(pb, bsz, seq, dil)
            obs.append(o)
            lses.append(lse)

        x1 = _merge(oa.reshape(bsz * seq, A_COLS), obs, lses, gates, x2,
                    w_pa[l].astype(BF16), w_pb[l].astype(BF16), w_o[l].astype(BF16),
                    ln1_g[l][None], ln1_b[l][None], tm=512)
        x2 = _ffn(x1, w_up[l].astype(BF16), w_conv[l], b_conv[l][None], w_down[l].astype(BF16),
                  ln2_g[l][None], ln2_b[l][None], seq, tm=1024)
        x = x2.reshape(bsz, seq, d)
    return x
```

```python
import functools
import math

import numpy as np
import jax
import jax.numpy as jnp
from jax import lax
from jax.experimental import pallas as pl
from jax.experimental.pallas import tpu as pltpu

BF16 = jnp.bfloat16
F32 = jnp.float32

D_MODEL = 1024
A_HEADS = 8
A_HEAD_DIM = 64
B_PATTERNS = ((128, 1), (512, 4), (2048, 16))
B_HEADS = 8
B_HEAD_DIM = 64
D_FF = 2816
CONV_WIDTH = 3
EPS = 1e-5
DEPTH = 1
ALPHA = (2.0 * DEPTH) ** 0.25

A_COLS = A_HEADS * 2 * A_HEAD_DIM
B_COLS = B_HEADS * B_HEAD_DIM
N_GROUPS = len(B_PATTERNS)
QKV_COLS = 3 * A_COLS + 3 * N_GROUPS * B_COLS
PAIR = 2 * A_HEAD_DIM
STEPS = 128
BF16_ROWS = 16
LOG2E = math.log2(math.e)
BF16_EXACT = 256
SLOPE_TERMS = 3
PROJ_ROWS = 2048
MXU_WIDTH = 256
FF_CHUNKS = ((0, 6 * MXU_WIDTH), (6 * MXU_WIDTH, 5 * MXU_WIDTH))

VMEM_LIMIT = 56 * 1024 * 1024


def _alibi_slopes(n):
    return np.power(np.float32(2.0), -8.0 * (np.arange(n, dtype=np.float32) + 1) / n).astype(np.float32)


def _nt_dot(a, b):
    return lax.dot_general(a, b, (((1,), (1,)), ((), ())), preferred_element_type=F32)


def _layer_norm(z, g, b):
    mu = jnp.mean(z, axis=-1, keepdims=True)
    zc = z - mu
    var = jnp.mean(zc * zc, axis=-1, keepdims=True)
    return zc * lax.rsqrt(var + EPS) * g + b


WIDE_COLS = 1024
A_TILES = 3 * A_COLS // WIDE_COLS
GROUP_TILES = 3
SPLIT = 4


def _proj_wide_kernel(x_ref, wa_ref, wg_lo_ref, wg_hi_ref, b_ref, pa_ref, g_ref, xb_ref):
    j = pl.program_id(1)

    @pl.when(j == 0)
    def _():
        xb_ref[...] = x_ref[...].astype(BF16)

    @pl.when(j < A_TILES)
    def _():
        pa_ref[...] = jnp.dot(xb_ref[...], wa_ref[...], preferred_element_type=F32).astype(pa_ref.dtype)

    @pl.when(j >= A_TILES)
    def _():
        wg = jnp.concatenate([wg_lo_ref[...], wg_hi_ref[...]], axis=1)
        z = jnp.dot(xb_ref[...], wg, preferred_element_type=F32) + b_ref[...]
        g_ref[...] = (0.5 * jnp.tanh(0.5 * z) + 0.5).astype(g_ref.dtype)


def _projection_wide(x2, wb, b_gate, tm):
    m, k = x2.shape
    tn = WIDE_COLS
    gate_tiles = b_gate.shape[1] // tn
    gate_first = QKV_COLS // (tn // 2)
    clamp = lambda j, first, count: jnp.clip(j - first, 0, count - 1)
    return pl.pallas_call(
        _proj_wide_kernel,
        grid=(m // tm, A_TILES + gate_tiles),
        in_specs=[pl.BlockSpec((tm, k), lambda i, j: (i, 0)),
                  pl.BlockSpec((k, tn), lambda i, j: (0, clamp(j, 0, A_TILES))),
                  pl.BlockSpec((k, tn // 2), lambda i, j: (0, gate_first + 2 * clamp(j, A_TILES, gate_tiles))),
                  pl.BlockSpec((k, tn // 2), lambda i, j: (0, gate_first + 2 * clamp(j, A_TILES, gate_tiles) + 1)),
                  pl.BlockSpec((1, tn), lambda i, j: (0, clamp(j, A_TILES, gate_tiles)))],
        out_specs=[pl.BlockSpec((tm, tn), lambda i, j: (i, clamp(j, 0, A_TILES))),
                   pl.BlockSpec((tm, tn), lambda i, j: (i, clamp(j, A_TILES, gate_tiles))),
                   pl.BlockSpec((tm, k), lambda i, j: (i, 0))],
        out_shape=[jax.ShapeDtypeStruct((m, A_TILES * tn), BF16),
                   jax.ShapeDtypeStruct((m, gate_tiles * tn), BF16),
                   jax.ShapeDtypeStruct((m, k), BF16)],
        compiler_params=pltpu.CompilerParams(
            dimension_semantics=("parallel", "arbitrary"), vmem_limit_bytes=VMEM_LIMIT),
        name="proj_wide",
    )(x2, wb, wb, wb, b_gate)


def _proj_groups_kernel(x_ref, w_ref, *rest):
    group_refs, (acc_ref, tmp_ref) = rest[:N_GROUPS], rest[N_GROUPS:]
    j = pl.program_id(1)
    lane_tiles, tm, _ = acc_ref.shape
    for g, (o_ref, (_, dil)) in enumerate(zip(group_refs, B_PATTERNS)):
        @pl.when((j >= g * GROUP_TILES) & (j < (g + 1) * GROUP_TILES))
        def _(o_ref=o_ref, dil=dil):
            acc = jnp.dot(x_ref[...], w_ref[...], preferred_element_type=F32)
            if dil == 1:
                o_ref[0, 0] = acc.astype(o_ref.dtype)
                return
            for c in range(lane_tiles):
                acc_ref[c] = acc[:, c * PAIR:(c + 1) * PAIR]
            for c in range(lane_tiles):
                cols = slice(c * PAIR, (c + 1) * PAIR)
                if dil == SPLIT:
                    for r in range(dil):
                        o_ref[0, r, :, cols] = acc_ref[c, pl.ds(r, tm // dil, stride=dil), :].astype(o_ref.dtype)
                else:
                    assert dil == SPLIT * SPLIT
                    part = tm // SPLIT
                    for r in range(SPLIT):
                        tmp_ref[c, r * part:(r + 1) * part, :] = acc_ref[c, pl.ds(r, part, stride=SPLIT), :]
                    for r in range(dil):
                        start = (r % SPLIT) * part + r // SPLIT
                        o_ref[0, r, :, cols] = \
                            tmp_ref[c, pl.ds(start, tm // dil, stride=SPLIT), :].astype(o_ref.dtype)


def _projection_groups(xb, wb, tm):
    m, k = xb.shape
    tn = B_COLS
    clamp = lambda j, first: jnp.clip(j - first, 0, GROUP_TILES - 1)
    w_block = lambda j: 3 * A_COLS // tn + (j % GROUP_TILES) * N_GROUPS + j // GROUP_TILES
    return pl.pallas_call(
        _proj_groups_kernel,
        grid=(m // tm, N_GROUPS * GROUP_TILES),
        in_specs=[pl.BlockSpec((tm, k), lambda i, j: (i, 0)),
                  pl.BlockSpec((k, tn), lambda i, j: (0, w_block(j)))],
        out_specs=[pl.BlockSpec((1, dil, tm // dil, tn), lambda i, j, g=g: (i, 0, 0, clamp(j, g * GROUP_TILES)))
                   for g, (_, dil) in enumerate(B_PATTERNS)],
        out_shape=[jax.ShapeDtypeStruct((m // tm, dil, tm // dil, GROUP_TILES * tn), BF16)
                   for _, dil in B_PATTERNS],
        scratch_shapes=[pltpu.VMEM((tn // PAIR, tm, PAIR), F32), pltpu.VMEM((tn // PAIR, tm, PAIR), F32)],
        compiler_params=pltpu.CompilerParams(
            dimension_semantics=("parallel", "arbitrary"), vmem_limit_bytes=VMEM_LIMIT),
        name="proj_groups",
    )(xb, wb)


def _diff_attn_kernel(slopes_ref, q_ref, k_ref, v_ref, lq1_ref, lk1_ref, lq2_ref, lk2_ref, subln_ref,
                      o_ref, vt_ref, mask_ref, kaug_ref, sa_ref, sb_ref, ma_ref, mb_ref, acc_ref, *,
                      t, lambda_init):
    slope = slopes_ref[pl.program_id(1)] * LOG2E
    nchunks = vt_ref.shape[0]

    @pl.when((pl.program_id(0) == 0) & (pl.program_id(1) == 0))
    def _():
        krow = lax.broadcasted_iota(jnp.int32, (t, t), 0)
        qcol = lax.broadcasted_iota(jnp.int32, (t, t), 1)
        mask_ref[...] = jnp.where(krow <= qcol, 0.0, -jnp.inf)
        r = lax.broadcasted_iota(jnp.int32, (t, PAIR), 0)
        c = lax.broadcasted_iota(jnp.int32, (t, PAIR), 1)
        r_low = (r % BF16_EXACT).astype(F32)
        r_high = (r - r % BF16_EXACT).astype(F32)
        kaug_ref[...] = jnp.where(c < SLOPE_TERMS, r_low,
                                  jnp.where(c < 2 * SLOPE_TERMS, r_high, 0.0)).astype(BF16)

    rest = jnp.full((PAIR, t), slope, F32)
    wrow = lax.broadcasted_iota(jnp.int32, (PAIR, t), 0)
    waug = jnp.zeros((PAIR, t), F32)
    for term in range(SLOPE_TERMS):
        part = rest.astype(BF16).astype(F32)
        waug = jnp.where((wrow == term) | (wrow == SLOPE_TERMS + term), part, waug)
        rest = rest - part
    waug = waug.astype(BF16)
    dim = lax.broadcasted_iota(jnp.int32, (PAIR, t), 0)

    lam = (jnp.exp(jnp.sum(lq1_ref[...] * lk1_ref[...], axis=-1, keepdims=True))
           - jnp.exp(jnp.sum(lq2_ref[...] * lk2_ref[...], axis=-1, keepdims=True))
           + lambda_init)
    neg = jnp.full((1, t), -jnp.inf, F32)
    half = t // 2
    s_refs = ((sa_ref, ma_ref), (sb_ref, mb_ref))

    for qi in range(nchunks):
        rows = slice(qi * t, (qi + 1) * t)
        acc = acc_ref.at[qi % 2]
        vt_ref[qi, 0:PAIR, :] = v_ref[0, rows, :].astype(F32).T.astype(BF16)
        vt_ref[qi, PAIR:, :] = jnp.ones((BF16_ROWS, t), BF16)
        qt = q_ref[0, rows, :].astype(F32).T * (A_HEAD_DIM ** -0.5 * LOG2E)
        qt_maps = (jnp.concatenate([jnp.where(dim < A_HEAD_DIM, qt, 0.0).astype(BF16), waug], axis=0),
                   jnp.concatenate([jnp.where(dim < A_HEAD_DIM, 0.0, qt).astype(BF16), waug], axis=0))
        acc[...] = jnp.zeros(acc.shape, F32)

        def scores(j, buf, qt_maps=qt_maps):
            s_ref, max_ref = buf
            kc = k_ref[0, pl.ds(pl.multiple_of(j * t, t), t), :]
            kc = jnp.concatenate([kc, kaug_ref[...]], axis=1)
            for mp in range(2):
                s = jnp.dot(kc, qt_maps[mp], preferred_element_type=F32)
                s_ref[mp] = s
                max_ref[mp] = jnp.max(s, axis=0, keepdims=True)

        def update(j, buf, stats, qi=qi, acc=acc):
            s_ref, max_ref = buf
            vt = vt_ref[j]
            shift = slope * jnp.asarray((j - qi) * t, F32)
            out = []
            for mp in range(2):
                m_new = jnp.maximum(stats[mp], max_ref[mp] + shift)
                p = jnp.exp2(s_ref[mp] - (m_new - shift)).astype(BF16)
                alpha = jnp.exp2(stats[mp] - m_new)
                acc[mp] = alpha * acc[mp] + jnp.dot(vt, p, preferred_element_type=F32)
                out.append(m_new)
            return tuple(out)

        def scores_diag(buf, qi=qi, qt_maps=qt_maps):
            s_ref, max_ref = buf
            k_lo = jnp.concatenate([k_ref[0, qi * t:qi * t + half, :], kaug_ref[0:half, :]], axis=1)
            k_hi = jnp.concatenate([k_ref[0, qi * t + half:(qi + 1) * t, :], kaug_ref[half:, :]], axis=1)
            for mp in range(2):
                s_lo = jnp.dot(k_lo, qt_maps[mp], preferred_element_type=F32) + mask_ref[0:half, :]
                s_hi = jnp.dot(k_hi, qt_maps[mp][:, half:], preferred_element_type=F32) \
                    + mask_ref[half:, half:]
                s_ref[mp, 0:half, :] = s_lo
                s_ref[mp, half:, half:] = s_hi
                max_lo = jnp.max(s_lo, axis=0, keepdims=True)
                max_ref[mp, :, 0:half] = max_lo[:, :half]
                max_ref[mp, :, half:] = jnp.maximum(max_lo[:, half:], jnp.max(s_hi, axis=0, keepdims=True))

        def update_diag(buf, stats, qi=qi, acc=acc):
            s_ref, max_ref = buf
            for mp in range(2):
                m_new = jnp.maximum(stats[mp], max_ref[mp])
                p_lo = jnp.exp2(s_ref[mp, 0:half, :] - m_new).astype(BF16)
                p_hi = jnp.exp2(s_ref[mp, half:, half:] - m_new[:, half:]).astype(BF16)
                alpha = jnp.exp2(stats[mp] - m_new)
                pv = jnp.dot(vt_ref[qi, :, 0:half], p_lo, preferred_element_type=F32)
                pv_hi = jnp.dot(vt_ref[qi, :, half:], p_hi, preferred_element_type=F32)
                acc[mp, :, 0:half] = alpha[:, :half] * acc[mp, :, 0:half] + pv[:, :half]
                acc[mp, :, half:] = alpha[:, half:] * acc[mp, :, half:] + (pv[:, half:] + pv_hi)

        first, second = s_refs

        def pair(i, stats, scores=scores, update=update, first=first, second=second):
            j = 2 * i
            scores(j + 1, second)
            stats = update(j, first, stats)
            scores(j + 2, first)
            return update(j + 1, second, stats)

        idx = jnp.int32
        if qi == 0:
            scores_diag(first)
            update_diag(first, (neg, neg))
        else:
            scores(idx(0), first)
            stats = lax.fori_loop(0, (qi - 1) // 2, pair, (neg, neg))
            if qi % 2 == 1:
                scores_diag(second)
                update_diag(second, update(idx(qi - 1), first, stats))
            else:
                scores(idx(qi - 1), second)
                stats = update(idx(qi - 2), first, stats)
                scores_diag(first)
                update_diag(first, update(idx(qi - 1), second, stats))
        if qi % 2 == 0:
            s_refs = (second, first)

        a1, a2 = acc[0], acc[1]
        ot = a1[:PAIR] / a1[PAIR:PAIR + 1] - lam * (a2[:PAIR] / a2[PAIR:PAIR + 1])
        ot = ot * lax.rsqrt(jnp.mean(ot * ot, axis=0, keepdims=True) + EPS)
        ot = ot * subln_ref[...] * (1.0 - lambda_init)
        o_ref[0, rows, :] = ot.T.astype(o_ref.dtype)


def _diff_attention(p3, slopes, lq1, lk1, lq2, lk2, subln_col, lambda_init, t):
    bsz, seq, _ = p3.shape
    vec = lambda n: pl.BlockSpec((1, n), lambda b, h: (0, 0))
    kernel = functools.partial(_diff_attn_kernel, t=t, lambda_init=lambda_init)
    return pl.pallas_call(
        kernel,
        grid=(bsz, A_HEADS),
        in_specs=[
            pl.BlockSpec(memory_space=pltpu.SMEM),
            pl.BlockSpec((1, seq, PAIR), lambda b, h: (b, 0, h)),
            pl.BlockSpec((1, seq, PAIR), lambda b, h: (b, 0, A_HEADS + h)),
            pl.BlockSpec((1, seq, PAIR), lambda b, h: (b, 0, 2 * A_HEADS + h)),
            vec(A_HEAD_DIM), vec(A_HEAD_DIM), vec(A_HEAD_DIM), vec(A_HEAD_DIM),
            pl.BlockSpec((PAIR, 1), lambda b, h: (0, 0)),
        ],
        out_specs=pl.BlockSpec((1, seq, PAIR), lambda b, h: (b, 0, h)),
        out_shape=jax.ShapeDtypeStruct((bsz, seq, A_COLS), BF16),
        scratch_shapes=[pltpu.VMEM((seq // t, PAIR + BF16_ROWS, t), BF16), pltpu.VMEM((t, t), F32),
                        pltpu.VMEM((t, PAIR), BF16),
                        pltpu.VMEM((2, t, t), F32), pltpu.VMEM((2, t, t), F32),
                        pltpu.VMEM((2, 1, t), F32), pltpu.VMEM((2, 1, t), F32),
                        pltpu.VMEM((2, 2, PAIR + BF16_ROWS, t), F32)],
        compiler_params=pltpu.CompilerParams(
            dimension_semantics=("arbitrary", "arbitrary"), vmem_limit_bytes=VMEM_LIMIT),
        name="diff_attn",
    )(slopes, p3, p3, p3, lq1, lk1, lq2, lk2, subln_col)


def _dilated_kernel(q_ref, k_ref, v_ref, kprev_ref, vprev_ref, o_ref, lse_ref, bias_ref, o_scr, lse_scr,
                    *maybe_order_scr, dil, slopes, tiles_per_seq):
    i = pl.program_id(0)
    blocks = q_ref.shape[2] // STEPS
    tm = o_ref.shape[0]
    two_pass = bool(maybe_order_scr)
    if two_pass:
        assert dil == SPLIT * SPLIT and blocks == 1
        order_scr, = maybe_order_scr

    @pl.when(i == 0)
    def _():
        qi = lax.broadcasted_iota(jnp.int32, (STEPS, 2 * STEPS), 0)
        kj = lax.broadcasted_iota(jnp.int32, (STEPS, 2 * STEPS), 1)
        step = qi + STEPS - kj
        window = (step >= 0) & (step <= STEPS)
        dist = (step * dil).astype(F32)
        for hd in range(B_HEADS):
            alibi = -float(slopes[hd]) * LOG2E * dist
            bias_ref[0, hd] = jnp.where(window & (kj >= STEPS), alibi, -jnp.inf)
            bias_ref[1, hd] = jnp.where(window, alibi, -jnp.inf)

    lane = lax.broadcasted_iota(jnp.int32, (STEPS, PAIR), 1)
    low = lane < B_HEAD_DIM
    low_keys = lax.broadcasted_iota(jnp.int32, (2 * STEPS, PAIR), 1) < B_HEAD_DIM
    ones_even = jnp.where(low_keys, 1.0, 0.0).astype(BF16)
    ones_odd = jnp.where(low_keys, 0.0, 1.0).astype(BF16)

    first_variant = (i % tiles_per_seq != 0).astype(jnp.int32)
    for u in range(dil * blocks):
        r, nl = divmod(u, blocks)
        variant = 1 if nl > 0 else first_variant
        cur = slice(nl * STEPS, (nl + 1) * STEPS)
        before = slice((nl - 1) * STEPS, nl * STEPS)
        q = q_ref[0, r, cur, :].astype(F32) * (B_HEAD_DIM ** -0.5 * LOG2E)
        k_before = k_ref[0, r, before, :] if nl > 0 else kprev_ref[0, r]
        v_before = v_ref[0, r, before, :] if nl > 0 else vprev_ref[0, r]
        kk = jnp.concatenate([k_before, k_ref[0, r, cur, :]], axis=0)
        vv = jnp.concatenate([v_before, v_ref[0, r, cur, :]], axis=0)
        start = nl * (STEPS * dil) + r
        if two_pass:
            rows = pl.ds((r % SPLIT) * (tm // SPLIT) + r // SPLIT, STEPS, stride=SPLIT)
        else:
            rows = pl.ds(start, STEPS, stride=dil) if dil > 1 else pl.ds(start, STEPS)
        for pr in range(B_HEADS // 2):
            cols = slice(pr * PAIR, (pr + 1) * PAIR)
            qp, kp = q[:, cols], kk[:, cols]
            vp = vv[:, cols].astype(F32)
            v_blocks = jnp.concatenate([
                jnp.concatenate([jnp.where(low_keys, vp, 0.0).astype(BF16), ones_even], axis=1),
                jnp.concatenate([jnp.where(low_keys, 0.0, vp).astype(BF16), ones_odd], axis=1)], axis=0)
            probs, maxes = [], []
            for par in range(2):
                qm = jnp.where(low, qp, 0.0) if par == 0 else jnp.where(low, 0.0, qp)
                s = _nt_dot(qm.astype(BF16), kp) + bias_ref[variant, 2 * pr + par]
                m = jnp.max(s, axis=-1, keepdims=True)
                probs.append(jnp.exp2(s - m).astype(BF16))
                maxes.append(m)
            o_den = jnp.dot(jnp.concatenate(probs, axis=1), v_blocks, preferred_element_type=F32)
            den = o_den[:, PAIR:]
            o_scr[pr, rows, :] = o_den[:, :PAIR] / den
            lse_scr[pr, rows, :] = jnp.where(low, maxes[0], maxes[1]) + jnp.log2(den)

    for src, dst in ((o_scr, o_ref), (lse_scr, lse_ref)):
        for pr in range(B_HEADS // 2):
            if two_pass:
                part = tm // SPLIT
                for r in range(SPLIT):
                    order_scr[pr, pl.ds(r, part, stride=SPLIT), :] = src[pr, r * part:(r + 1) * part, :]
            ordered = order_scr if two_pass else src
            dst[:, pr * PAIR:(pr + 1) * PAIR] = ordered[pr].astype(dst.dtype)


def _dilated_attention(pb, bsz, seq, dil):
    tiles, _, per_res, _ = pb.shape
    tm = dil * per_res
    tile_blk = (1, dil, per_res, B_COLS)
    prev_blk = (1, dil, STEPS, B_COLS)
    last = per_res // STEPS - 1
    prev = lambda i: jnp.maximum(i - 1, 0)
    kernel = functools.partial(_dilated_kernel, dil=dil, slopes=_alibi_slopes(B_HEADS),
                               tiles_per_seq=tiles // bsz)
    out_block = pl.BlockSpec((tm, B_COLS), lambda i: (i, 0))
    return pl.pallas_call(
        kernel,
        grid=(tiles,),
        in_specs=[
            pl.BlockSpec(tile_blk, lambda i: (i, 0, 0, 0)),
            pl.BlockSpec(tile_blk, lambda i: (i, 0, 0, 1)),
            pl.BlockSpec(tile_blk, lambda i: (i, 0, 0, 2)),
            pl.BlockSpec(prev_blk, lambda i: (prev(i), 0, last, 1)),
            pl.BlockSpec(prev_blk, lambda i: (prev(i), 0, last, 2)),
        ],
        out_specs=[out_block, out_block],
        out_shape=[jax.ShapeDtypeStruct((bsz * seq, B_COLS), BF16),
                   jax.ShapeDtypeStruct((bsz * seq, B_COLS), F32)],
        scratch_shapes=[pltpu.VMEM((2, B_HEADS, STEPS, 2 * STEPS), F32),
                        pltpu.VMEM((B_HEADS // 2, tm, PAIR), F32), pltpu.VMEM((B_HEADS // 2, tm, PAIR), F32)]
        + [pltpu.VMEM((B_HEADS // 2, tm, PAIR), F32)] * (dil == SPLIT * SPLIT),
        compiler_params=pltpu.CompilerParams(
            dimension_semantics=("arbitrary",), vmem_limit_bytes=VMEM_LIMIT),
        name=f"dilated_attn_d{dil}",
    )(pb, pb, pb, pb, pb)


MERGE_SLABS = 2


def _merge_kernel(oa_ref, o0_ref, o1_ref, o2_ref, l0_ref, l1_ref, l2_ref, ga_ref, gb_ref, x_ref,
                  wpa_ref, wpb_ref, wo_ref, g_ref, b_ref, out_ref):
    slab = out_ref.shape[0] // MERGE_SLABS
    for part in range(MERGE_SLABS):
        rows = slice(part * slab, (part + 1) * slab)
        l0, l1, l2 = l0_ref[rows, :], l1_ref[rows, :], l2_ref[rows, :]
        mx = jnp.maximum(jnp.maximum(l0, l1), l2)
        e0, e1, e2 = jnp.exp2(l0 - mx), jnp.exp2(l1 - mx), jnp.exp2(l2 - mx)
        ob = (e0 * o0_ref[rows, :].astype(F32) + e1 * o1_ref[rows, :].astype(F32)
              + e2 * o2_ref[rows, :].astype(F32)) / (e0 + e1 + e2)
        ya = jnp.dot(oa_ref[rows, :], wpa_ref[...], preferred_element_type=F32)
        yb = jnp.dot(ob.astype(BF16), wpb_ref[...], preferred_element_type=F32)
        y = ga_ref[rows, :].astype(F32) * ya + gb_ref[rows, :].astype(F32) * yb
        z = ALPHA * x_ref[rows, :] + jnp.dot(y.astype(BF16), wo_ref[...], preferred_element_type=F32)
        out_ref[rows, :] = _layer_norm(z, g_ref[...], b_ref[...])


def _merge(oa, obs, lses, gates, x2, wpa, wpb, wo, g, b, tm):
    m = x2.shape[0]
    row = lambda w: pl.BlockSpec((tm, w), lambda i: (i, 0))
    full = lambda a: pl.BlockSpec(a.shape, lambda i: (0, 0))
    return pl.pallas_call(
        _merge_kernel,
        grid=(m // tm,),
        in_specs=[row(A_COLS), row(B_COLS), row(B_COLS), row(B_COLS), row(B_COLS), row(B_COLS), row(B_COLS),
                  pl.BlockSpec((tm, D_MODEL), lambda i: (i, 0)), pl.BlockSpec((tm, D_MODEL), lambda i: (i, 1)),
                  row(D_MODEL), full(wpa), full(wpb), full(wo), full(g), full(b)],
        out_specs=row(D_MODEL),
        out_shape=jax.ShapeDtypeStruct((m, D_MODEL), F32),
        compiler_params=pltpu.CompilerParams(dimension_semantics=("parallel",), vmem_limit_bytes=VMEM_LIMIT),
        name="merge_ln1",
    )(oa, *obs, *lses, gates, gates, x2, wpa, wpb, wo, g, b)


HALO = BF16_ROWS


def _ffn_kernel(x_ref, halo_ref, wup_ref, wconv_ref, bconv_ref, wd_ref, g_ref, b_ref,
                out_ref, xcat_ref, acc_ref, *, tiles_per_seq):
    i = pl.program_id(0)
    halo = jnp.where(i % tiles_per_seq == 0, 0.0, halo_ref[...])
    xcat_ref[0:HALO, :] = halo.astype(BF16)
    xcat_ref[HALO:, :] = x_ref[...].astype(BF16)
    xcat = xcat_ref[...]

    def conv(col0, width):
        cols = slice(col0, col0 + width)
        hfull = jnp.dot(xcat, wup_ref[:, cols], preferred_element_type=F32)
        out = bconv_ref[:, cols] + wconv_ref[CONV_WIDTH - 1:CONV_WIDTH, cols] * hfull[HALO:, :]
        for back in range(1, CONV_WIDTH):
            shifted = pltpu.roll(hfull, back, axis=0)[HALO:, :]
            out = out + wconv_ref[CONV_WIDTH - 1 - back:CONV_WIDTH - back, cols] * shifted
        return out

    for col0, width in FF_CHUNKS:
        a = conv(col0, width)
        gv = conv(D_FF + col0, width)
        f = 0.5 * a * (1.0 + lax.erf(a * (2.0 ** -0.5))) * gv
        y = jnp.dot(f.astype(BF16), wd_ref[col0:col0 + width, :], preferred_element_type=F32)
        if col0 == 0:
            acc_ref[...] = y
        else:
            acc_ref[...] += y

    z = ALPHA * x_ref[...] + acc_ref[...]
    out_ref[...] = _layer_norm(z, g_ref[...], b_ref[...])


def _ffn(x1, w_up, w_conv, b_conv, w_down, g, b, seq, tm):
    m = x1.shape[0]
    halo_blocks = tm // HALO
    kernel = functools.partial(_ffn_kernel, tiles_per_seq=seq // tm)
    full = lambda a: pl.BlockSpec(a.shape, lambda i: (0, 0))
    once = lambda a: pl.BlockSpec(a.shape, lambda i: (0, 0), pipeline_mode=pl.Buffered(1))
    return pl.pallas_call(
        kernel,
        grid=(m // tm,),
        in_specs=[
            pl.BlockSpec((tm, D_MODEL), lambda i: (i, 0)),
            pl.BlockSpec((HALO, D_MODEL), lambda i: (jnp.maximum(i * halo_blocks - 1, 0), 0)),
            once(w_up), full(w_conv), full(b_conv), once(w_down), full(g), full(b),
        ],
        out_specs=pl.BlockSpec((tm, D_MODEL), lambda i: (i, 0)),
        out_shape=jax.ShapeDtypeStruct((m, D_MODEL), F32),
        scratch_shapes=[pltpu.VMEM((HALO + tm, D_MODEL), BF16), pltpu.VMEM((tm, D_MODEL), F32)],
        compiler_params=pltpu.CompilerParams(dimension_semantics=("parallel",), vmem_limit_bytes=VMEM_LIMIT),
        name="ffn_ln2",
    )(x1, x1, w_up, w_conv, b_conv, w_down, g, b)


def kernel(x, w_in, b_gate, lambda_q1, lambda_k1, lambda_q2, lambda_k2, subln_w, w_pa, w_pb, w_o, ln1_g, ln1_b,
           w_up, w_conv, b_conv, w_down, ln2_g, ln2_b):
    bsz, seq, d = x.shape
    assert (seq, d) == (4096, D_MODEL) and w_in.shape[0] == DEPTH
    slopes_a = jnp.asarray(_alibi_slopes(A_HEADS))
    for l in range(DEPTH):
        lambda_init = 0.8 - 0.6 * math.exp(-0.3 * l)
        x2 = x.reshape(bsz * seq, d)
        wb = w_in[l].astype(BF16)
        proj_a, gates, xb = _projection_wide(x2, wb, b_gate[l][None, :], tm=PROJ_ROWS)
        proj_groups = _projection_groups(xb, wb, tm=PROJ_ROWS)
        oa = _diff_attention(proj_a.reshape(bsz, seq, 3 * A_COLS), slopes_a, lambda_q1[l][None],
                             lambda_k1[l][None], lambda_q2[l][None], lambda_k2[l][None], subln_w[l][:, None],
                             lambda_init, t=512)
        obs, lses = [], []
        for pb, (window, dil) in zip(proj_groups, B_PATTERNS):
            assert window // dil == STEPS
            o, lse = _dilated_attention(pb, bsz, seq, dil)
            obs.append(o)
            lses.append(lse)

        x1 = _merge(oa.reshape(bsz * seq, A_COLS), obs, lses, gates, x2,
                    w_pa[l].astype(BF16), w_pb[l].astype(BF16), w_o[l].astype(BF16),
                    ln1_g[l][None], ln1_b[l][None], tm=512)
        x2 = _ffn(x1, w_up[l].astype(BF16), w_conv[l], b_conv[l][None], w_down[l].astype(BF16),
                  ln2_g[l][None], ln2_b[l][None], seq, tm=1024)
        x = x2.reshape(bsz, seq, d)
    return x
```

```python
import functools
import math

import numpy as np
import jax
import jax.numpy as jnp
from jax import lax
from jax.experimental import pallas as pl
from jax.experimental.pallas import tpu as pltpu

BF16 = jnp.bfloat16
F32 = jnp.float32

D_MODEL = 1024
A_HEADS = 8
A_HEAD_DIM = 64
B_PATTERNS = ((128, 1), (512, 4), (2048, 16))
B_HEADS = 8
B_HEAD_DIM = 64
D_FF = 2816
CONV_WIDTH = 3
EPS = 1e-5
DEPTH = 1
ALPHA = (2.0 * DEPTH) ** 0.25

A_COLS = A_HEADS * 2 * A_HEAD_DIM
B_COLS = B_HEADS * B_HEAD_DIM
N_GROUPS = len(B_PATTERNS)
QKV_COLS = 3 * A_COLS + 3 * N_GROUPS * B_COLS
PAIR = 2 * A_HEAD_DIM
STEPS = 128
BF16_ROWS = 16
LOG2E = math.log2(math.e)
BF16_EXACT = 256
SLOPE_TERMS = 3
PROJ_ROWS = 2048
MXU_WIDTH = 256
FF_CHUNKS = ((0, 6 * MXU_WIDTH), (6 * MXU_WIDTH, 5 * MXU_WIDTH))

VMEM_LIMIT = 56 * 1024 * 1024


def _alibi_slopes(n):
    return np.power(np.float32(2.0), -8.0 * (np.arange(n, dtype=np.float32) + 1) / n).astype(np.float32)


def _nt_dot(a, b):
    return lax.dot_general(a, b, (((1,), (1,)), ((), ())), preferred_element_type=F32)


def _layer_norm(z, g, b):
    mu = jnp.mean(z, axis=-1, keepdims=True)
    zc = z - mu
    var = jnp.mean(zc * zc, axis=-1, keepdims=True)
    return zc * lax.rsqrt(var + EPS) * g + b


WIDE_COLS = 1024
A_TILES = 3 * A_COLS // WIDE_COLS
GROUP_TILES = 3
SPLIT = 4


def _proj_wide_kernel(x_ref, wa_ref, wg_lo_ref, wg_hi_ref, b_ref, pa_ref, g_ref, xb_ref):
    j = pl.program_id(1)

    @pl.when(j == 0)
    def _():
        xb_ref[...] = x_ref[...].astype(BF16)

    @pl.when(j < A_TILES)
    def _():
        pa_ref[...] = jnp.dot(xb_ref[...], wa_ref[...], preferred_element_type=F32).astype(pa_ref.dtype)

    @pl.when(j >= A_TILES)
    def _():
        wg = jnp.concatenate([wg_lo_ref[...], wg_hi_ref[...]], axis=1)
        z = jnp.dot(xb_ref[...], wg, preferred_element_type=F32) + b_ref[...]
        g_ref[...] = (0.5 * jnp.tanh(0.5 * z) + 0.5).astype(g_ref.dtype)


def _projection_wide(x2, wb, b_gate, tm):
    m, k = x2.shape
    tn = WIDE_COLS
    gate_tiles = b_gate.shape[1] // tn
    gate_first = QKV_COLS // (tn // 2)
    clamp = lambda j, first, count: jnp.clip(j - first, 0, count - 1)
    return pl.pallas_call(
        _proj_wide_kernel,
        grid=(m // tm, A_TILES + gate_tiles),
        in_specs=[pl.BlockSpec((tm, k), lambda i, j: (i, 0)),
                  pl.BlockSpec((k, tn), lambda i, j: (0, clamp(j, 0, A_TILES))),
                  pl.BlockSpec((k, tn // 2), lambda i, j: (0, gate_first + 2 * clamp(j, A_TILES, gate_tiles))),
                  pl.BlockSpec((k, tn // 2), lambda i, j: (0, gate_first + 2 * clamp(j, A_TILES, gate_tiles) + 1)),
                  pl.BlockSpec((1, tn), lambda i, j: (0, clamp(j, A_TILES, gate_tiles)))],
        out_specs=[pl.BlockSpec((tm, tn), lambda i, j: (i, clamp(j, 0, A_TILES))),
                   pl.BlockSpec((tm, tn), lambda i, j: (i, clamp(j, A_TILES, gate_tiles))),
                   pl.BlockSpec((tm, k), lambda i, j: (i, 0))],
        out_shape=[jax.ShapeDtypeStruct((m, A_TILES * tn), BF16),
                   jax.ShapeDtypeStruct((m, gate_tiles * tn), BF16),
                   jax.ShapeDtypeStruct((m, k), BF16)],
        compiler_params=pltpu.CompilerParams(
            dimension_semantics=("parallel", "arbitrary"), vmem_limit_bytes=VMEM_LIMIT),
        name="proj_wide",
    )(x2, wb, wb, wb, b_gate)


def _proj_groups_kernel(x_ref, w_ref, *rest):
    group_refs, (acc_ref, tmp_ref) = rest[:N_GROUPS], rest[N_GROUPS:]
    j = pl.program_id(1)
    lane_tiles, tm, _ = acc_ref.shape
    for g, (o_ref, (_, dil)) in enumerate(zip(group_refs, B_PATTERNS)):
        @pl.when((j >= g * GROUP_TILES) & (j < (g + 1) * GROUP_TILES))
        def _(o_ref=o_ref, dil=dil):
            acc = jnp.dot(x_ref[...], w_ref[...], preferred_element_type=F32)
            if dil == 1:
                o_ref[0, 0] = acc.astype(o_ref.dtype)
                return
            for c in range(lane_tiles):
                acc_ref[c] = acc[:, c * PAIR:(c + 1) * PAIR]
            for c in range(lane_tiles):
                cols = slice(c * PAIR, (c + 1) * PAIR)
                if dil == SPLIT:
                    for r in range(dil):
                        o_ref[0, r, :, cols] = acc_ref[c, pl.ds(r, tm // dil, stride=dil), :].astype(o_ref.dtype)
                else:
                    assert dil == SPLIT * SPLIT
                    part = tm // SPLIT
                    for r in range(SPLIT):
                        tmp_ref[c, r * part:(r + 1) * part, :] = acc_ref[c, pl.ds(r, part, stride=SPLIT), :]
                    for r in range(dil):
                        start = (r % SPLIT) * part + r // SPLIT
                        o_ref[0, r, :, cols] = \
                            tmp_ref[c, pl.ds(start, tm // dil, stride=SPLIT), :].astype(o_ref.dtype)


def _projection_groups(xb, wb, tm):
    m, k = xb.shape
    tn = B_COLS
    clamp = lambda j, first: jnp.clip(j - first, 0, GROUP_TILES - 1)
    w_block = lambda j: 3 * A_COLS // tn + (j % GROUP_TILES) * N_GROUPS + j // GROUP_TILES
    return pl.pallas_call(
        _proj_groups_kernel,
        grid=(m // tm, N_GROUPS * GROUP_TILES),
        in_specs=[pl.BlockSpec((tm, k), lambda i, j: (i, 0)),
                  pl.BlockSpec((k, tn), lambda i, j: (0, w_block(j)))],
        out_specs=[pl.BlockSpec((1, dil, tm // dil, tn), lambda i, j, g=g: (i, 0, 0, clamp(j, g * GROUP_TILES)))
                   for g, (_, dil) in enumerate(B_PATTERNS)],
        out_shape=[jax.ShapeDtypeStruct((m // tm, dil, tm // dil, GROUP_TILES * tn), BF16)
                   for _, dil in B_PATTERNS],
        scratch_shapes=[pltpu.VMEM((tn // PAIR, tm, PAIR), F32), pltpu.VMEM((tn // PAIR, tm, PAIR), F32)],
        compiler_params=pltpu.CompilerParams(
            dimension_semantics=("parallel", "arbitrary"), vmem_limit_bytes=VMEM_LIMIT),
        name="proj_groups",
    )(xb, wb)


def _diff_attn_kernel(slopes_ref, q_ref, k_ref, v_ref, lq1_ref, lk1_ref, lq2_ref, lk2_ref, subln_ref,
                      o_ref, vt_ref, mask_ref, kaug_ref, sa_ref, sb_ref, ma_ref, mb_ref, acc_ref, *,
                      t, lambda_init):
    slope = slopes_ref[pl.program_id(1)] * LOG2E
    nchunks = vt_ref.shape[0]

    @pl.when((pl.program_id(0) == 0) & (pl.program_id(1) == 0))
    def _():
        krow = lax.broadcasted_iota(jnp.int32, (t, t), 0)
        qcol = lax.broadcasted_iota(jnp.int32, (t, t), 1)
        mask_ref[...] = jnp.where(krow <= qcol, 0.0, -jnp.inf)
        r = lax.broadcasted_iota(jnp.int32, (t, PAIR), 0)
        c = lax.broadcasted_iota(jnp.int32, (t, PAIR), 1)
        r_low = (r % BF16_EXACT).astype(F32)
        r_high = (r - r % BF16_EXACT).astype(F32)
        kaug_ref[...] = jnp.where(c < SLOPE_TERMS, r_low,
                                  jnp.where(c < 2 * SLOPE_TERMS, r_high, 0.0)).astype(BF16)

    rest = jnp.full((PAIR, t), slope, F32)
    wrow = lax.broadcasted_iota(jnp.int32, (PAIR, t), 0)
    waug = jnp.zeros((PAIR, t), F32)
    for term in range(SLOPE_TERMS):
        part = rest.astype(BF16).astype(F32)
        waug = jnp.where((wrow == term) | (wrow == SLOPE_TERMS + term), part, waug)
        rest = rest - part
    waug = waug.astype(BF16)
    dim = lax.broadcasted_iota(jnp.int32, (PAIR, t), 0)

    lam = (jnp.exp(jnp.sum(lq1_ref[...] * lk1_ref[...], axis=-1, keepdims=True))
           - jnp.exp(jnp.sum(lq2_ref[...] * lk2_ref[...], axis=-1, keepdims=True))
           + lambda_init)
    neg = jnp.full((1, t), -jnp.inf, F32)
    half = t // 2
    s_refs = ((sa_ref, ma_ref), (sb_ref, mb_ref))

    for qi in range(nchunks):
        rows = slice(qi * t, (qi + 1) * t)
        acc = acc_ref.at[qi % 2]
        vt_ref[qi, 0:PAIR, :] = v_ref[0, rows, :].astype(F32).T.astype(BF16)
        vt_ref[qi, PAIR:, :] = jnp.ones((BF16_ROWS, t), BF16)
        qt = q_ref[0, rows, :].astype(F32).T * (A_HEAD_DIM ** -0.5 * LOG2E)
        qt_maps = (jnp.concatenate([jnp.where(dim < A_HEAD_DIM, qt, 0.0).astype(BF16), waug], axis=0),
                   jnp.concatenate([jnp.where(dim < A_HEAD_DIM, 0.0, qt).astype(BF16), waug], axis=0))
        acc[...] = jnp.zeros(acc.shape, F32)

        def scores(j, buf, qt_maps=qt_maps):
            s_ref, max_ref = buf
            kc = k_ref[0, pl.ds(pl.multiple_of(j * t, t), t), :]
            kc = jnp.concatenate([kc, kaug_ref[...]], axis=1)
            for mp in range(2):
                s = jnp.dot(kc, qt_maps[mp], preferred_element_type=F32)
                s_ref[mp] = s
                max_ref[mp] = jnp.max(s, axis=0, keepdims=True)

        def update(j, buf, stats, qi=qi, acc=acc):
            s_ref, max_ref = buf
            vt = vt_ref[j]
            shift = slope * jnp.asarray((j - qi) * t, F32)
            out = []
            for mp in range(2):
                m_new = jnp.maximum(stats[mp], max_ref[mp] + shift)
                p = jnp.exp2(s_ref[mp] - (m_new - shift)).astype(BF16)
                alpha = jnp.exp2(stats[mp] - m_new)
                acc[mp] = alpha * acc[mp] + jnp.dot(vt, p, preferred_element_type=F32)
                out.append(m_new)
            return tuple(out)

        def scores_diag(buf, qi=qi, qt_maps=qt_maps):
            s_ref, max_ref = buf
            k_lo = jnp.concatenate([k_ref[0, qi * t:qi * t + half, :], kaug_ref[0:half, :]], axis=1)
            k_hi = jnp.concatenate([k_ref[0, qi * t + half:(qi + 1) * t, :], kaug_ref[half:, :]], axis=1)
            for mp in range(2):
                s_lo = jnp.dot(k_lo, qt_maps[mp], preferred_element_type=F32) + mask_ref[0:half, :]
                s_hi = jnp.dot(k_hi, qt_maps[mp][:, half:], preferred_element_type=F32) \
                    + mask_ref[half:, half:]
                s_ref[mp, 0:half, :] = s_lo
                s_ref[mp, half:, half:] = s_hi
                max_lo = jnp.max(s_lo, axis=0, keepdims=True)
                max_ref[mp, :, 0:half] = max_lo[:, :half]
                max_ref[mp, :, half:] = jnp.maximum(max_lo[:, half:], jnp.max(s_hi, axis=0, keepdims=True))

        def update_diag(buf, stats, qi=qi, acc=acc):
            s_ref, max_ref = buf
            for mp in range(2):
                m_new = jnp.maximum(stats[mp], max_ref[mp])
                p_lo = jnp.exp2(s_ref[mp, 0:half, :] - m_new).astype(BF16)
                p_hi = jnp.exp2(s_ref[mp, half:, half:] - m_new[:, half:]).astype(BF16)
                alpha = jnp.exp2(stats[mp] - m_new)
                pv = jnp.dot(vt_ref[qi, :, 0:half], p_lo, preferred_element_type=F32)
                pv_hi = jnp.dot(vt_ref[qi, :, half:], p_hi, preferred_element_type=F32)
                acc[mp, :, 0:half] = alpha[:, :half] * acc[mp, :, 0:half] + pv[:, :half]
                acc[mp, :, half:] = alpha[:, half:] * acc[mp, :, half:] + (pv[:, half:] + pv_hi)

        first, second = s_refs

        def pair(i, stats, scores=scores, update=update, first=first, second=second):
            j = 2 * i
            scores(j + 1, second)
            stats = update(j, first, stats)
            scores(j + 2, first)
            return update(j + 1, second, stats)

        idx = jnp.int32
        if qi == 0:
            scores_diag(first)
            update_diag(first, (neg, neg))
        else:
            scores(idx(0), first)
            stats = lax.fori_loop(0, (qi - 1) // 2, pair, (neg, neg))
            if qi % 2 == 1:
                scores_diag(second)
                update_diag(second, update(idx(qi - 1), first, stats))
            else:
                scores(idx(qi - 1), second)
                stats = update(idx(qi - 2), first, stats)
                scores_diag(first)
                update_diag(first, update(idx(qi - 1), second, stats))
        if qi % 2 == 0:
            s_refs = (second, first)

        a1, a2 = acc[0], acc[1]
        ot = a1[:PAIR] / a1[PAIR:PAIR + 1] - lam * (a2[:PAIR] / a2[PAIR:PAIR + 1])
        ot = ot * lax.rsqrt(jnp.mean(ot * ot, axis=0, keepdims=True) + EPS)
        ot = ot * subln_ref[...] * (1.0 - lambda_init)
        o_ref[0, rows, :] = ot.T.astype(o_ref.dtype)


def _diff_attention(p3, slopes, lq1, lk1, lq2, lk2, subln_col, lambda_init, t):
    bsz, seq, _ = p3.shape
    vec = lambda n: pl.BlockSpec((1, n), lambda b, h: (0, 0))
    kernel = functools.partial(_diff_attn_kernel, t=t, lambda_init=lambda_init)
    return pl.pallas_call(
        kernel,
        grid=(bsz, A_HEADS),
        in_specs=[
            pl.BlockSpec(memory_space=pltpu.SMEM),
            pl.BlockSpec((1, seq, PAIR), lambda b, h: (b, 0, h)),
            pl.BlockSpec((1, seq, PAIR), lambda b, h: (b, 0, A_HEADS + h)),
            pl.BlockSpec((1, seq, PAIR), lambda b, h: (b, 0, 2 * A_HEADS + h)),
            vec(A_HEAD_DIM), vec(A_HEAD_DIM), vec(A_HEAD_DIM), vec(A_HEAD_DIM),
            pl.BlockSpec((PAIR, 1), lambda b, h: (0, 0)),
        ],
        out_specs=pl.BlockSpec((1, seq, PAIR), lambda b, h: (b, 0, h)),
        out_shape=jax.ShapeDtypeStruct((bsz, seq, A_COLS), BF16),
        scratch_shapes=[pltpu.VMEM((seq // t, PAIR + BF16_ROWS, t), BF16), pltpu.VMEM((t, t), F32),
                        pltpu.VMEM((t, PAIR), BF16),
                        pltpu.VMEM((2, t, t), F32), pltpu.VMEM((2, t, t), F32),
                        pltpu.VMEM((2, 1, t), F32), pltpu.VMEM((2, 1, t), F32),
                        pltpu.VMEM((2, 2, PAIR + BF16_ROWS, t), F32)],
        compiler_params=pltpu.CompilerParams(
            dimension_semantics=("arbitrary", "arbitrary"), vmem_limit_bytes=VMEM_LIMIT),
        name="diff_attn",
    )(slopes, p3, p3, p3, lq1, lk1, lq2, lk2, subln_col)


def _dilated_kernel(q_ref, k_ref, v_ref, kprev_ref, vprev_ref, o_ref, lse_ref, bias_ref, o_scr, lse_scr,
                    *maybe_order_scr, dil, slopes, tiles_per_seq):
    i = pl.program_id(0)
    blocks = q_ref.shape[2] // STEPS
    tm = o_ref.shape[0]
    two_pass = bool(maybe_order_scr)
    if two_pass:
        assert dil == SPLIT * SPLIT and blocks == 1
        order_scr, = maybe_order_scr

    @pl.when(i == 0)
    def _():
        qi = lax.broadcasted_iota(jnp.int32, (STEPS, 2 * STEPS), 0)
        kj = lax.broadcasted_iota(jnp.int32, (STEPS, 2 * STEPS), 1)
        step = qi + STEPS - kj
        window = (step >= 0) & (step <= STEPS)
        dist = (step * dil).astype(F32)
        for hd in range(B_HEADS):
            alibi = -float(slopes[hd]) * LOG2E * dist
            bias_ref[0, hd] = jnp.where(window & (kj >= STEPS), alibi, -jnp.inf)
            bias_ref[1, hd] = jnp.where(window, alibi, -jnp.inf)

    lane = lax.broadcasted_iota(jnp.int32, (STEPS, PAIR), 1)
    low = lane < B_HEAD_DIM
    low_keys = lax.broadcasted_iota(jnp.int32, (2 * STEPS, PAIR), 1) < B_HEAD_DIM
    ones_even = jnp.where(low_keys, 1.0, 0.0).astype(BF16)
    ones_odd = jnp.where(low_keys, 0.0, 1.0).astype(BF16)

    first_variant = (i % tiles_per_seq != 0).astype(jnp.int32)
    for u in range(dil * blocks):
        r, nl = divmod(u, blocks)
        variant = 1 if nl > 0 else first_variant
        cur = slice(nl * STEPS, (nl + 1) * STEPS)
        before = slice((nl - 1) * STEPS, nl * STEPS)
        q = q_ref[0, r, cur, :].astype(F32) * (B_HEAD_DIM ** -0.5 * LOG2E)
        k_before = k_ref[0, r, before, :] if nl > 0 else kprev_ref[0, r]
        v_before = v_ref[0, r, before, :] if nl > 0 else vprev_ref[0, r]
        kk = jnp.concatenate([k_before, k_ref[0, r, cur, :]], axis=0)
        vv = jnp.concatenate([v_before, v_ref[0, r, cur, :]], axis=0)
        start = nl * (STEPS * dil) + r
        if two_pass:
            rows = pl.ds((r % SPLIT) * (tm // SPLIT) + r // SPLIT, STEPS, stride=SPLIT)
        else:
            rows = pl.ds(start, STEPS, stride=dil) if dil > 1 else pl.ds(start, STEPS)
        for pr in range(B_HEADS // 2):
            cols = slice(pr * PAIR, (pr + 1) * PAIR)
            qp, kp = q[:, cols], kk[:, cols]
            vp = vv[:, cols].astype(F32)
            v_blocks = jnp.concatenate([
                jnp.concatenate([jnp.where(low_keys, vp, 0.0).astype(BF16), ones_even], axis=1),
                jnp.concatenate([jnp.where(low_keys, 0.0, vp).astype(BF16), ones_odd], axis=1)], axis=0)
            probs, maxes = [], []
            for par in range(2):
                qm = jnp.where(low, qp, 0.0) if par == 0 else jnp.where(low, 0.0, qp)
                s = _nt_dot(qm.astype(BF16), kp) + bias_ref[variant, 2 * pr + par]
                m = jnp.max(s, axis=-1, keepdims=True)
                probs.append(jnp.exp2(s - m).astype(BF16))
                maxes.append(m)
            o_den = jnp.dot(jnp.concatenate(probs, axis=1), v_blocks, preferred_element_type=F32)
            den = o_den[:, PAIR:]
            o_scr[pr, rows, :] = o_den[:, :PAIR] / den
            lse_scr[pr, rows, :] = jnp.where(low, maxes[0], maxes[1]) + jnp.log2(den)

    for src, dst in ((o_scr, o_ref), (lse_scr, lse_ref)):
        for pr in range(B_HEADS // 2):
            if two_pass:
                part = tm // SPLIT
                for r in range(SPLIT):
                    order_scr[pr, pl.ds(r, part, stride=SPLIT), :] = src[pr, r * part:(r + 1) * part, :]
            ordered = order_scr if two_pass else src
            dst[:, pr * PAIR:(pr + 1) * PAIR] = ordered[pr].astype(dst.dtype)


def _dilated_attention(pb, bsz, seq, dil):
    tiles, _, per_res, _ = pb.shape
    tm = dil * per_res
    tile_blk = (1, dil, per_res, B_COLS)
    prev_blk = (1, dil, STEPS, B_COLS)
    last = per_res // STEPS - 1
    prev = lambda i: jnp.maximum(i - 1, 0)
    kernel = functools.partial(_dilated_kernel, dil=dil, slopes=_alibi_slopes(B_HEADS),
                               tiles_per_seq=tiles // bsz)
    out_block = pl.BlockSpec((tm, B_COLS), lambda i: (i, 0))
    return pl.pallas_call(
        kernel,
        grid=(tiles,),
        in_specs=[
            pl.BlockSpec(tile_blk, lambda i: (i, 0, 0, 0)),
            pl.BlockSpec(tile_blk, lambda i: (i, 0, 0, 1)),
            pl.BlockSpec(tile_blk, lambda i: (i, 0, 0, 2)),
            pl.BlockSpec(prev_blk, lambda i: (prev(i), 0, last, 1)),
            pl.BlockSpec(prev_blk, lambda i: (prev(i), 0, last, 2)),
        ],
        out_specs=[out_block, out_block],
        out_shape=[jax.ShapeDtypeStruct((bsz * seq, B_COLS), BF16),
                   jax.ShapeDtypeStruct((bsz * seq, B_COLS), F32)],
        scratch_shapes=[pltpu.VMEM((2, B_HEADS, STEPS, 2 * STEPS), F32),
                        pltpu.VMEM((B_HEADS // 2, tm, PAIR), F32), pltpu.VMEM((B_HEADS // 2, tm, PAIR), F32)]
        + [pltpu.VMEM((B_HEADS // 2, tm, PAIR), F32)] * (dil == SPLIT * SPLIT),
        compiler_params=pltpu.CompilerParams(
            dimension_semantics=("arbitrary",), vmem_limit_bytes=VMEM_LIMIT),
        name=f"dilated_attn_d{dil}",
    )(pb, pb, pb, pb, pb)


MERGE_SLAB_ROWS = 256


def _merge_kernel(oa_ref, o0_ref, o1_ref, o2_ref, l0_ref, l1_ref, l2_ref, ga_ref, gb_ref, x_ref,
                  wpa_ref, wpb_ref, wo_ref, g_ref, b_ref, out_ref):
    slab = MERGE_SLAB_ROWS
    for part in range(out_ref.shape[0] // slab):
        rows = slice(part * slab, (part + 1) * slab)
        l0, l1, l2 = l0_ref[rows, :], l1_ref[rows, :], l2_ref[rows, :]
        mx = jnp.maximum(jnp.maximum(l0, l1), l2)
        e0, e1, e2 = jnp.exp2(l0 - mx), jnp.exp2(l1 - mx), jnp.exp2(l2 - mx)
        ob = (e0 * o0_ref[rows, :].astype(F32) + e1 * o1_ref[rows, :].astype(F32)
              + e2 * o2_ref[rows, :].astype(F32)) / (e0 + e1 + e2)
        ya = jnp.dot(oa_ref[rows, :], wpa_ref[...], preferred_element_type=F32)
        yb = jnp.dot(ob.astype(BF16), wpb_ref[...], preferred_element_type=F32)
        y = ga_ref[rows, :].astype(F32) * ya + gb_ref[rows, :].astype(F32) * yb
        z = ALPHA * x_ref[rows, :] + jnp.dot(y.astype(BF16), wo_ref[...], preferred_element_type=F32)
        out_ref[rows, :] = _layer_norm(z, g_ref[...], b_ref[...])


def _merge(oa, obs, lses, gates, x2, wpa, wpb, wo, g, b, tm):
    m = x2.shape[0]
    row = lambda w: pl.BlockSpec((tm, w), lambda i: (i, 0))
    full = lambda a: pl.BlockSpec(a.shape, lambda i: (0, 0))
    once = lambda a: pl.BlockSpec(a.shape, lambda i: (0, 0), pipeline_mode=pl.Buffered(1))
    return pl.pallas_call(
        _merge_kernel,
        grid=(m // tm,),
        in_specs=[row(A_COLS), row(B_COLS), row(B_COLS), row(B_COLS), row(B_COLS), row(B_COLS), row(B_COLS),
                  pl.BlockSpec((tm, D_MODEL), lambda i: (i, 0)), pl.BlockSpec((tm, D_MODEL), lambda i: (i, 1)),
                  row(D_MODEL), once(wpa), once(wpb), once(wo), full(g), full(b)],
        out_specs=row(D_MODEL),
        out_shape=jax.ShapeDtypeStruct((m, D_MODEL), F32),
        compiler_params=pltpu.CompilerParams(dimension_semantics=("parallel",), vmem_limit_bytes=VMEM_LIMIT),
        name="merge_ln1",
    )(oa, *obs, *lses, gates, gates, x2, wpa, wpb, wo, g, b)


HALO = BF16_ROWS


def _ffn_kernel(x_ref, halo_ref, wup_ref, wconv_ref, bconv_ref, wd_ref, g_ref, b_ref,
                out_ref, xcat_ref, acc_ref, *, tiles_per_seq):
    i = pl.program_id(0)
    halo = jnp.where(i % tiles_per_seq == 0, 0.0, halo_ref[...])
    xcat_ref[0:HALO, :] = halo.astype(BF16)
    xcat_ref[HALO:, :] = x_ref[...].astype(BF16)
    xcat = xcat_ref[...]

    def conv(col0, width):
        cols = slice(col0, col0 + width)
        hfull = jnp.dot(xcat, wup_ref[:, cols], preferred_element_type=F32)
        out = bconv_ref[:, cols] + wconv_ref[CONV_WIDTH - 1:CONV_WIDTH, cols] * hfull[HALO:, :]
        for back in range(1, CONV_WIDTH):
            shifted = pltpu.roll(hfull, back, axis=0)[HALO:, :]
            out = out + wconv_ref[CONV_WIDTH - 1 - back:CONV_WIDTH - back, cols] * shifted
        return out

    for col0, width in FF_CHUNKS:
        a = conv(col0, width)
        gv = conv(D_FF + col0, width)
        f = 0.5 * a * (1.0 + lax.erf(a * (2.0 ** -0.5))) * gv
        y = jnp.dot(f.astype(BF16), wd_ref[col0:col0 + width, :], preferred_element_type=F32)
        if col0 == 0:
            acc_ref[...] = y
        else:
            acc_ref[...] += y

    z = ALPHA * x_ref[...] + acc_ref[...]
    out_ref[...] = _layer_norm(z, g_ref[...], b_ref[...])


def _ffn(x1, w_up, w_conv, b_conv, w_down, g, b, seq, tm):
    m = x1.shape[0]
    halo_blocks = tm // HALO
    kernel = functools.partial(_ffn_kernel, tiles_per_seq=seq // tm)
    full = lambda a: pl.BlockSpec(a.shape, lambda i: (0, 0))
    once = lambda a: pl.BlockSpec(a.shape, lambda i: (0, 0), pipeline_mode=pl.Buffered(1))
    return pl.pallas_call(
        kernel,
        grid=(m // tm,),
        in_specs=[
            pl.BlockSpec((tm, D_MODEL), lambda i: (i, 0)),
            pl.BlockSpec((HALO, D_MODEL), lambda i: (jnp.maximum(i * halo_blocks - 1, 0), 0)),
            once(w_up), full(w_conv), full(b_conv), once(w_down), full(g), full(b),
        ],
        out_specs=pl.BlockSpec((tm, D_MODEL), lambda i: (i, 0)),
        out_shape=jax.ShapeDtypeStruct((m, D_MODEL), F32),
        scratch_shapes=[pltpu.VMEM((HALO + tm, D_MODEL), BF16), pltpu.VMEM((tm, D_MODEL), F32)],
        compiler_params=pltpu.CompilerParams(dimension_semantics=("parallel",), vmem_limit_bytes=VMEM_LIMIT),
        name="ffn_ln2",
    )(x1, x1, w_up, w_conv, b_conv, w_down, g, b)


def kernel(x, w_in, b_gate, lambda_q1, lambda_k1, lambda_q2, lambda_k2, subln_w, w_pa, w_pb, w_o, ln1_g, ln1_b,
           w_up, w_conv, b_conv, w_down, ln2_g, ln2_b):
    bsz, seq, d = x.shape
    assert (seq, d) == (4096, D_MODEL) and w_in.shape[0] == DEPTH
    slopes_a = jnp.asarray(_alibi_slopes(A_HEADS))
    for l in range(DEPTH):
        lambda_init = 0.8 - 0.6 * math.exp(-0.3 * l)
        x2 = x.reshape(bsz * seq, d)
        wb = w_in[l].astype(BF16)
        proj_a, gates, xb = _projection_wide(x2, wb, b_gate[l][None, :], tm=PROJ_ROWS)
        proj_groups = _projection_groups(xb, wb, tm=PROJ_ROWS)
        oa = _diff_attention(proj_a.reshape(bsz, seq, 3 * A_COLS), slopes_a, lambda_q1[l][None],
                             lambda_k1[l][None], lambda_q2[l][None], lambda_k2[l][None], subln_w[l][:, None],
                             lambda_init, t=512)
        obs, lses = [], []
        for pb, (window, dil) in zip(proj_groups, B_PATTERNS):
            assert window // dil == STEPS
            o, lse = _dilated_attention(pb, bsz, seq, dil)
            obs.append(o)
            lses.append(lse)

        x1 = _merge(oa.reshape(bsz * seq, A_COLS), obs, lses, gates, x2,
                    w_pa[l].astype(BF16), w_pb[l].astype(BF16), w_o[l].astype(BF16),
                    ln1_g[l][None], ln1_b[l][None], tm=1024)
        x2 = _ffn(x1, w_up[l].astype(BF16), w_conv[l], b_conv[l][None], w_down[l].astype(BF16),
                  ln2_g[l][None], ln2_b[l][None], seq, tm=1024)
        x = x2.reshape(bsz, seq, d)
    return x
```

```python
import functools
import math

import numpy as np
import jax
import jax.numpy as jnp
from jax import lax
from jax.experimental import pallas as pl
from jax.experimental.pallas import tpu as pltpu

BF16 = jnp.bfloat16
F32 = jnp.float32

D_MODEL = 1024
A_HEADS = 8
A_HEAD_DIM = 64
B_PATTERNS = ((128, 1), (512, 4), (2048, 16))
B_HEADS = 8
B_HEAD_DIM = 64
D_FF = 2816
CONV_WIDTH = 3
EPS = 1e-5
DEPTH = 1
ALPHA = (2.0 * DEPTH) ** 0.25

A_COLS = A_HEADS * 2 * A_HEAD_DIM
B_COLS = B_HEADS * B_HEAD_DIM
N_GROUPS = len(B_PATTERNS)
QKV_COLS = 3 * A_COLS + 3 * N_GROUPS * B_COLS
PAIR = 2 * A_HEAD_DIM
STEPS = 128
BF16_ROWS = 16
LOG2E = math.log2(math.e)
BF16_EXACT = 256
SLOPE_TERMS = 3
PROJ_ROWS = 2048
MXU_WIDTH = 256
FF_CHUNKS = ((0, 6 * MXU_WIDTH), (6 * MXU_WIDTH, 5 * MXU_WIDTH))

VMEM_LIMIT = 56 * 1024 * 1024


def _alibi_slopes(n):
    return np.power(np.float32(2.0), -8.0 * (np.arange(n, dtype=np.float32) + 1) / n).astype(np.float32)


def _nt_dot(a, b):
    return lax.dot_general(a, b, (((1,), (1,)), ((), ())), preferred_element_type=F32)


def _layer_norm(z, g, b):
    mu = jnp.mean(z, axis=-1, keepdims=True)
    zc = z - mu
    var = jnp.mean(zc * zc, axis=-1, keepdims=True)
    return zc * lax.rsqrt(var + EPS) * g + b


WIDE_COLS = 1024
A_TILES = 3 * A_COLS // WIDE_COLS
GROUP_TILES = 3
SPLIT = 4


def _proj_wide_kernel(x_ref, wa_ref, wg_lo_ref, wg_hi_ref, b_ref, pa_ref, g_ref, xb_ref):
    j = pl.program_id(1)

    @pl.when(j == 0)
    def _():
        xb_ref[...] = x_ref[...].astype(BF16)

    @pl.when(j < A_TILES)
    def _():
        pa_ref[...] = jnp.dot(xb_ref[...], wa_ref[...], preferred_element_type=F32).astype(pa_ref.dtype)

    @pl.when(j >= A_TILES)
    def _():
        wg = jnp.concatenate([wg_lo_ref[...], wg_hi_ref[...]], axis=1)
        z = jnp.dot(xb_ref[...], wg, preferred_element_type=F32) + b_ref[...]
        g_ref[...] = (0.5 * jnp.tanh(0.5 * z) + 0.5).astype(g_ref.dtype)


def _projection_wide(x2, wb, b_gate, tm):
    m, k = x2.shape
    tn = WIDE_COLS
    gate_tiles = b_gate.shape[1] // tn
    gate_first = QKV_COLS // (tn // 2)
    clamp = lambda j, first, count: jnp.clip(j - first, 0, count - 1)
    return pl.pallas_call(
        _proj_wide_kernel,
        grid=(m // tm, A_TILES + gate_tiles),
        in_specs=[pl.BlockSpec((tm, k), lambda i, j: (i, 0)),
                  pl.BlockSpec((k, tn), lambda i, j: (0, clamp(j, 0, A_TILES))),
                  pl.BlockSpec((k, tn // 2), lambda i, j: (0, gate_first + 2 * clamp(j, A_TILES, gate_tiles))),
                  pl.BlockSpec((k, tn // 2), lambda i, j: (0, gate_first + 2 * clamp(j, A_TILES, gate_tiles) + 1)),
                  pl.BlockSpec((1, tn), lambda i, j: (0, clamp(j, A_TILES, gate_tiles)))],
        out_specs=[pl.BlockSpec((tm, tn), lambda i, j: (i, clamp(j, 0, A_TILES))),
                   pl.BlockSpec((tm, tn), lambda i, j: (i, clamp(j, A_TILES, gate_tiles))),
                   pl.BlockSpec((tm, k), lambda i, j: (i, 0))],
        out_shape=[jax.ShapeDtypeStruct((m, A_TILES * tn), BF16),
                   jax.ShapeDtypeStruct((m, gate_tiles * tn), BF16),
                   jax.ShapeDtypeStruct((m, k), BF16)],
        compiler_params=pltpu.CompilerParams(
            dimension_semantics=("parallel", "arbitrary"), vmem_limit_bytes=VMEM_LIMIT),
        name="proj_wide",
    )(x2, wb, wb, wb, b_gate)


def _proj_groups_kernel(x_ref, w_ref, *rest):
    group_refs, (acc_ref, tmp_ref) = rest[:N_GROUPS], rest[N_GROUPS:]
    j = pl.program_id(1)
    lane_tiles, tm, _ = acc_ref.shape
    for g, (o_ref, (_, dil)) in enumerate(zip(group_refs, B_PATTERNS)):
        @pl.when((j >= g * GROUP_TILES) & (j < (g + 1) * GROUP_TILES))
        def _(o_ref=o_ref, dil=dil):
            acc = jnp.dot(x_ref[...], w_ref[...], preferred_element_type=F32)
            if dil == 1:
                o_ref[0, 0] = acc.astype(o_ref.dtype)
                return
            for c in range(lane_tiles):
                acc_ref[c] = acc[:, c * PAIR:(c + 1) * PAIR]
            for c in range(lane_tiles):
                cols = slice(c * PAIR, (c + 1) * PAIR)
                if dil == SPLIT:
                    for r in range(dil):
                        o_ref[0, r, :, cols] = acc_ref[c, pl.ds(r, tm // dil, stride=dil), :].astype(o_ref.dtype)
                else:
                    assert dil == SPLIT * SPLIT
                    part = tm // SPLIT
                    for r in range(SPLIT):
                        tmp_ref[c, r * part:(r + 1) * part, :] = acc_ref[c, pl.ds(r, part, stride=SPLIT), :]
                    for r in range(dil):
                        start = (r % SPLIT) * part + r // SPLIT
                        o_ref[0, r, :, cols] = \
                            tmp_ref[c, pl.ds(start, tm // dil, stride=SPLIT), :].astype(o_ref.dtype)


def _projection_groups(xb, wb, tm):
    m, k = xb.shape
    tn = B_COLS
    clamp = lambda j, first: jnp.clip(j - first, 0, GROUP_TILES - 1)
    w_block = lambda j: 3 * A_COLS // tn + (j % GROUP_TILES) * N_GROUPS + j // GROUP_TILES
    return pl.pallas_call(
        _proj_groups_kernel,
        grid=(m // tm, N_GROUPS * GROUP_TILES),
        in_specs=[pl.BlockSpec((tm, k), lambda i, j: (i, 0)),
                  pl.BlockSpec((k, tn), lambda i, j: (0, w_block(j)))],
        out_specs=[pl.BlockSpec((1, dil, tm // dil, tn), lambda i, j, g=g: (i, 0, 0, clamp(j, g * GROUP_TILES)))
                   for g, (_, dil) in enumerate(B_PATTERNS)],
        out_shape=[jax.ShapeDtypeStruct((m // tm, dil, tm // dil, GROUP_TILES * tn), BF16)
                   for _, dil in B_PATTERNS],
        scratch_shapes=[pltpu.VMEM((tn // PAIR, tm, PAIR), F32), pltpu.VMEM((tn // PAIR, tm, PAIR), F32)],
        compiler_params=pltpu.CompilerParams(
            dimension_semantics=("parallel", "arbitrary"), vmem_limit_bytes=VMEM_LIMIT),
        name="proj_groups",
    )(xb, wb)


def _diff_attn_kernel(slopes_ref, q_ref, k_ref, v_ref, lq1_ref, lk1_ref, lq2_ref, lk2_ref, subln_ref,
                      o_ref, vt_ref, mask_ref, kaug_ref, sa_ref, sb_ref, ma_ref, mb_ref, acc_ref, *,
                      t, lambda_init):
    slope = slopes_ref[pl.program_id(1)] * LOG2E
    nchunks = vt_ref.shape[0]

    @pl.when((pl.program_id(0) == 0) & (pl.program_id(1) == 0))
    def _():
        krow = lax.broadcasted_iota(jnp.int32, (t, t), 0)
        qcol = lax.broadcasted_iota(jnp.int32, (t, t), 1)
        mask_ref[...] = jnp.where(krow <= qcol, 0.0, -jnp.inf)
        r = lax.broadcasted_iota(jnp.int32, (t, PAIR), 0)
        c = lax.broadcasted_iota(jnp.int32, (t, PAIR), 1)
        r_low = (r % BF16_EXACT).astype(F32)
        r_high = (r - r % BF16_EXACT).astype(F32)
        kaug_ref[...] = jnp.where(c < SLOPE_TERMS, r_low,
                                  jnp.where(c < 2 * SLOPE_TERMS, r_high, 0.0)).astype(BF16)

    rest = jnp.full((PAIR, t), slope, F32)
    wrow = lax.broadcasted_iota(jnp.int32, (PAIR, t), 0)
    waug = jnp.zeros((PAIR, t), F32)
    for term in range(SLOPE_TERMS):
        part = rest.astype(BF16).astype(F32)
        waug = jnp.where((wrow == term) | (wrow == SLOPE_TERMS + term), part, waug)
        rest = rest - part
    waug = waug.astype(BF16)
    dim = lax.broadcasted_iota(jnp.int32, (PAIR, t), 0)

    lam = (jnp.exp(jnp.sum(lq1_ref[...] * lk1_ref[...], axis=-1, keepdims=True))
           - jnp.exp(jnp.sum(lq2_ref[...] * lk2_ref[...], axis=-1, keepdims=True))
           + lambda_init)
    neg = jnp.full((1, t), -jnp.inf, F32)
    half = t // 2
    s_refs = ((sa_ref, ma_ref), (sb_ref, mb_ref))

    for qi in range(nchunks):
        rows = slice(qi * t, (qi + 1) * t)
        acc = acc_ref.at[qi % 2]
        vt_ref[qi, 0:PAIR, :] = v_ref[0, rows, :].astype(F32).T.astype(BF16)
        vt_ref[qi, PAIR:, :] = jnp.ones((BF16_ROWS, t), BF16)
        qt = q_ref[0, rows, :].astype(F32).T * (A_HEAD_DIM ** -0.5 * LOG2E)
        qt_maps = (jnp.concatenate([jnp.where(dim < A_HEAD_DIM, qt, 0.0).astype(BF16), waug], axis=0),
                   jnp.concatenate([jnp.where(dim < A_HEAD_DIM, 0.0, qt).astype(BF16), waug], axis=0))
        acc[...] = jnp.zeros(acc.shape, F32)

        def scores(j, buf, qt_maps=qt_maps):
            s_ref, max_ref = buf
            kc = k_ref[0, pl.ds(pl.multiple_of(j * t, t), t), :]
            kc = jnp.concatenate([kc, kaug_ref[...]], axis=1)
            for mp in range(2):
                s = jnp.dot(kc, qt_maps[mp], preferred_element_type=F32)
                s_ref[mp] = s
                max_ref[mp] = jnp.max(s, axis=0, keepdims=True)

        def update(j, buf, stats, qi=qi, acc=acc):
            s_ref, max_ref = buf
            vt = vt_ref[j]
            shift = slope * jnp.asarray((j - qi) * t, F32)
            out = []
            for mp in range(2):
                m_new = jnp.maximum(stats[mp], max_ref[mp] + shift)
                p = jnp.exp2(s_ref[mp] - (m_new - shift)).astype(BF16)
                alpha = jnp.exp2(stats[mp] - m_new)
                acc[mp] = alpha * acc[mp] + jnp.dot(vt, p, preferred_element_type=F32)
                out.append(m_new)
            return tuple(out)

        def scores_diag(buf, qi=qi, qt_maps=qt_maps):
            s_ref, max_ref = buf
            k_lo = jnp.concatenate([k_ref[0, qi * t:qi * t + half, :], kaug_ref[0:half, :]], axis=1)
            k_hi = jnp.concatenate([k_ref[0, qi * t + half:(qi + 1) * t, :], kaug_ref[half:, :]], axis=1)
            for mp in range(2):
                s_lo = jnp.dot(k_lo, qt_maps[mp], preferred_element_type=F32) + mask_ref[0:half, :]
                s_hi = jnp.dot(k_hi, qt_maps[mp][:, half:], preferred_element_type=F32) \
                    + mask_ref[half:, half:]
                s_ref[mp, 0:half, :] = s_lo
                s_ref[mp, half:, half:] = s_hi
                max_lo = jnp.max(s_lo, axis=0, keepdims=True)
                max_ref[mp, :, 0:half] = max_lo[:, :half]
                max_ref[mp, :, half:] = jnp.maximum(max_lo[:, half:], jnp.max(s_hi, axis=0, keepdims=True))

        def update_diag(buf, stats, qi=qi, acc=acc):
            s_ref, max_ref = buf
            for mp in range(2):
                m_new = jnp.maximum(stats[mp], max_ref[mp])
                p_lo = jnp.exp2(s_ref[mp, 0:half, :] - m_new).astype(BF16)
                p_hi = jnp.exp2(s_ref[mp, half:, half:] - m_new[:, half:]).astype(BF16)
                alpha = jnp.exp2(stats[mp] - m_new)
                pv = jnp.dot(vt_ref[qi, :, 0:half], p_lo, preferred_element_type=F32)
                pv_hi = jnp.dot(vt_ref[qi, :, half:], p_hi, preferred_element_type=F32)
                acc[mp, :, 0:half] = alpha[:, :half] * acc[mp, :, 0:half] + pv[:, :half]
                acc[mp, :, half:] = alpha[:, half:] * acc[mp, :, half:] + (pv[:, half:] + pv_hi)

        first, second = s_refs

        def pair(i, stats, scores=scores, update=update, first=first, second=second):
            j = 2 * i
            scores(j + 1, second)
            stats = update(j, first, stats)
            scores(j + 2, first)
            return update(j + 1, second, stats)

        idx = jnp.int32
        if qi == 0:
            scores_diag(first)
            update_diag(first, (neg, neg))
        else:
            scores(idx(0), first)
            stats = lax.fori_loop(0, (qi - 1) // 2, pair, (neg, neg), unroll=(qi - 1) // 2 <= 2)
            if qi % 2 == 1:
                scores_diag(second)
                update_diag(second, update(idx(qi - 1), first, stats))
            else:
                scores(idx(qi - 1), second)
                stats = update(idx(qi - 2), first, stats)
                scores_diag(first)
                update_diag(first, update(idx(qi - 1), second, stats))
        if qi % 2 == 0:
            s_refs = (second, first)

        a1, a2 = acc[0], acc[1]
        ot = a1[:PAIR] / a1[PAIR:PAIR + 1] - lam * (a2[:PAIR] / a2[PAIR:PAIR + 1])
        ot = ot * lax.rsqrt(jnp.mean(ot * ot, axis=0, keepdims=True) + EPS)
        ot = ot * subln_ref[...] * (1.0 - lambda_init)
        o_ref[0, rows, :] = ot.T.astype(o_ref.dtype)


def _diff_attention(p3, slopes, lq1, lk1, lq2, lk2, subln_col, lambda_init, t):
    bsz, seq, _ = p3.shape
    vec = lambda n: pl.BlockSpec((1, n), lambda b, h: (0, 0))
    kernel = functools.partial(_diff_attn_kernel, t=t, lambda_init=lambda_init)
    return pl.pallas_call(
        kernel,
        grid=(bsz, A_HEADS),
        in_specs=[
            pl.BlockSpec(memory_space=pltpu.SMEM),
            pl.BlockSpec((1, seq, PAIR), lambda b, h: (b, 0, h)),
            pl.BlockSpec((1, seq, PAIR), lambda b, h: (b, 0, A_HEADS + h)),
            pl.BlockSpec((1, seq, PAIR), lambda b, h: (b, 0, 2 * A_HEADS + h)),
            vec(A_HEAD_DIM), vec(A_HEAD_DIM), vec(A_HEAD_DIM), vec(A_HEAD_DIM),
            pl.BlockSpec((PAIR, 1), lambda b, h: (0, 0)),
        ],
        out_specs=pl.BlockSpec((1, seq, PAIR), lambda b, h: (b, 0, h)),
        out_shape=jax.ShapeDtypeStruct((bsz, seq, A_COLS), BF16),
        scratch_shapes=[pltpu.VMEM((seq // t, PAIR + BF16_ROWS, t), BF16), pltpu.VMEM((t, t), F32),
                        pltpu.VMEM((t, PAIR), BF16),
                        pltpu.VMEM((2, t, t), F32), pltpu.VMEM((2, t, t), F32),
                        pltpu.VMEM((2, 1, t), F32), pltpu.VMEM((2, 1, t), F32),
                        pltpu.VMEM((2, 2, PAIR + BF16_ROWS, t), F32)],
        compiler_params=pltpu.CompilerParams(
            dimension_semantics=("arbitrary", "arbitrary"), vmem_limit_bytes=VMEM_LIMIT),
        name="diff_attn",
    )(slopes, p3, p3, p3, lq1, lk1, lq2, lk2, subln_col)


def _dilated_kernel(q_ref, k_ref, v_ref, kprev_ref, vprev_ref, o_ref, lse_ref, bias_ref, o_scr, lse_scr,
                    *maybe_order_scr, dil, slopes, tiles_per_seq):
    i = pl.program_id(0)
    blocks = q_ref.shape[2] // STEPS
    tm = o_ref.shape[0]
    two_pass = bool(maybe_order_scr)
    if two_pass:
        assert dil == SPLIT * SPLIT and blocks == 1
        order_scr, = maybe_order_scr

    @pl.when(i == 0)
    def _():
        qi = lax.broadcasted_iota(jnp.int32, (STEPS, 2 * STEPS), 0)
        kj = lax.broadcasted_iota(jnp.int32, (STEPS, 2 * STEPS), 1)
        step = qi + STEPS - kj
        window = (step >= 0) & (step <= STEPS)
        dist = (step * dil).astype(F32)
        for hd in range(B_HEADS):
            alibi = -float(slopes[hd]) * LOG2E * dist
            bias_ref[0, hd] = jnp.where(window & (kj >= STEPS), alibi, -jnp.inf)
            bias_ref[1, hd] = jnp.where(window, alibi, -jnp.inf)

    lane = lax.broadcasted_iota(jnp.int32, (STEPS, PAIR), 1)
    low = lane < B_HEAD_DIM
    low_keys = lax.broadcasted_iota(jnp.int32, (2 * STEPS, PAIR), 1) < B_HEAD_DIM
    ones_even = jnp.where(low_keys, 1.0, 0.0).astype(BF16)
    ones_odd = jnp.where(low_keys, 0.0, 1.0).astype(BF16)

    first_variant = (i % tiles_per_seq != 0).astype(jnp.int32)
    for u in range(dil * blocks):
        r, nl = divmod(u, blocks)
        variant = 1 if nl > 0 else first_variant
        cur = slice(nl * STEPS, (nl + 1) * STEPS)
        before = slice((nl - 1) * STEPS, nl * STEPS)
        q = q_ref[0, r, cur, :].astype(F32) * (B_HEAD_DIM ** -0.5 * LOG2E)
        k_before = k_ref[0, r, before, :] if nl > 0 else kprev_ref[0, r]
        v_before = v_ref[0, r, before, :] if nl > 0 else vprev_ref[0, r]
        kk = jnp.concatenate([k_before, k_ref[0, r, cur, :]], axis=0)
        vv = jnp.concatenate([v_before, v_ref[0, r, cur, :]], axis=0)
        start = nl * (STEPS * dil) + r
        if two_pass:
            rows = pl.ds((r % SPLIT) * (tm // SPLIT) + r // SPLIT, STEPS, stride=SPLIT)
        else:
            rows = pl.ds(start, STEPS, stride=dil) if dil > 1 else pl.ds(start, STEPS)
        for pr in range(B_HEADS // 2):
            cols = slice(pr * PAIR, (pr + 1) * PAIR)
            qp, kp = q[:, cols], kk[:, cols]
            vp = vv[:, cols].astype(F32)
            v_blocks = jnp.concatenate([
                jnp.concatenate([jnp.where(low_keys, vp, 0.0).astype(BF16), ones_even], axis=1),
                jnp.concatenate([jnp.where(low_keys, 0.0, vp).astype(BF16), ones_odd], axis=1)], axis=0)
            probs, maxes = [], []
            for par in range(2):
                qm = jnp.where(low, qp, 0.0) if par == 0 else jnp.where(low, 0.0, qp)
                s = _nt_dot(qm.astype(BF16), kp) + bias_ref[variant, 2 * pr + par]
                m = jnp.max(s, axis=-1, keepdims=True)
                probs.append(jnp.exp2(s - m).astype(BF16))
                maxes.append(m)
            o_den = jnp.dot(jnp.concatenate(probs, axis=1), v_blocks, preferred_element_type=F32)
            den = o_den[:, PAIR:]
            o_scr[pr, rows, :] = o_den[:, :PAIR] / den
            lse_scr[pr, rows, :] = jnp.where(low, maxes[0], maxes[1]) + jnp.log2(den)

    for src, dst in ((o_scr, o_ref), (lse_scr, lse_ref)):
        for pr in range(B_HEADS // 2):
            if two_pass:
                part = tm // SPLIT
                for r in range(SPLIT):
                    order_scr[pr, pl.ds(r, part, stride=SPLIT), :] = src[pr, r * part:(r + 1) * part, :]
            ordered = order_scr if two_pass else src
            dst[:, pr * PAIR:(pr + 1) * PAIR] = ordered[pr].astype(dst.dtype)


def _dilated_attention(pb, bsz, seq, dil):
    tiles, _, per_res, _ = pb.shape
    tm = dil * per_res
    tile_blk = (1, dil, per_res, B_COLS)
    prev_blk = (1, dil, STEPS, B_COLS)
    last = per_res // STEPS - 1
    prev = lambda i: jnp.maximum(i - 1, 0)
    kernel = functools.partial(_dilated_kernel, dil=dil, slopes=_alibi_slopes(B_HEADS),
                               tiles_per_seq=tiles // bsz)
    out_block = pl.BlockSpec((tm, B_COLS), lambda i: (i, 0))
    return pl.pallas_call(
        kernel,
        grid=(tiles,),
        in_specs=[
            pl.BlockSpec(tile_blk, lambda i: (i, 0, 0, 0)),
            pl.BlockSpec(tile_blk, lambda i: (i, 0, 0, 1)),
            pl.BlockSpec(tile_blk, lambda i: (i, 0, 0, 2)),
            pl.BlockSpec(prev_blk, lambda i: (prev(i), 0, last, 1)),
            pl.BlockSpec(prev_blk, lambda i: (prev(i), 0, last, 2)),
        ],
        out_specs=[out_block, out_block],
        out_shape=[jax.ShapeDtypeStruct((bsz * seq, B_COLS), BF16),
                   jax.ShapeDtypeStruct((bsz * seq, B_COLS), F32)],
        scratch_shapes=[pltpu.VMEM((2, B_HEADS, STEPS, 2 * STEPS), F32),
                        pltpu.VMEM((B_HEADS // 2, tm, PAIR), F32), pltpu.VMEM((B_HEADS // 2, tm, PAIR), F32)]
        + [pltpu.VMEM((B_HEADS // 2, tm, PAIR), F32)] * (dil == SPLIT * SPLIT),
        compiler_params=pltpu.CompilerParams(
            dimension_semantics=("arbitrary",), vmem_limit_bytes=VMEM_LIMIT),
        name=f"dilated_attn_d{dil}",
    )(pb, pb, pb, pb, pb)


MERGE_SLAB_ROWS = 256


def _merge_kernel(oa_ref, o0_ref, o1_ref, o2_ref, l0_ref, l1_ref, l2_ref, ga_ref, gb_ref, x_ref,
                  wpa_ref, wpb_ref, wo_ref, g_ref, b_ref, out_ref):
    slab = MERGE_SLAB_ROWS
    for part in range(out_ref.shape[0] // slab):
        rows = slice(part * slab, (part + 1) * slab)
        l0, l1, l2 = l0_ref[rows, :], l1_ref[rows, :], l2_ref[rows, :]
        mx = jnp.maximum(jnp.maximum(l0, l1), l2)
        e0, e1, e2 = jnp.exp2(l0 - mx), jnp.exp2(l1 - mx), jnp.exp2(l2 - mx)
        ob = (e0 * o0_ref[rows, :].astype(F32) + e1 * o1_ref[rows, :].astype(F32)
              + e2 * o2_ref[rows, :].astype(F32)) / (e0 + e1 + e2)
        ya = jnp.dot(oa_ref[rows, :], wpa_ref[...], preferred_element_type=F32)
        yb = jnp.dot(ob.astype(BF16), wpb_ref[...], preferred_element_type=F32)
        y = ga_ref[rows, :].astype(F32) * ya + gb_ref[rows, :].astype(F32) * yb
        z = ALPHA * x_ref[rows, :] + jnp.dot(y.astype(BF16), wo_ref[...], preferred_element_type=F32)
        out_ref[rows, :] = _layer_norm(z, g_ref[...], b_ref[...])


def _merge(oa, obs, lses, gates, x2, wpa, wpb, wo, g, b, tm):
    m = x2.shape[0]
    row = lambda w: pl.BlockSpec((tm, w), lambda i: (i, 0))
    full = lambda a: pl.BlockSpec(a.shape, lambda i: (0, 0))
    once = lambda a: pl.BlockSpec(a.shape, lambda i: (0, 0), pipeline_mode=pl.Buffered(1))
    return pl.pallas_call(
        _merge_kernel,
        grid=(m // tm,),
        in_specs=[row(A_COLS), row(B_COLS), row(B_COLS), row(B_COLS), row(B_COLS), row(B_COLS), row(B_COLS),
                  pl.BlockSpec((tm, D_MODEL), lambda i: (i, 0)), pl.BlockSpec((tm, D_MODEL), lambda i: (i, 1)),
                  row(D_MODEL), once(wpa), once(wpb), once(wo), full(g), full(b)],
        out_specs=row(D_MODEL),
        out_shape=jax.ShapeDtypeStruct((m, D_MODEL), F32),
        compiler_params=pltpu.CompilerParams(dimension_semantics=("parallel",), vmem_limit_bytes=VMEM_LIMIT),
        name="merge_ln1",
    )(oa, *obs, *lses, gates, gates, x2, wpa, wpb, wo, g, b)


HALO = BF16_ROWS


def _ffn_kernel(x_ref, halo_ref, wup_ref, wconv_ref, bconv_ref, wd_ref, g_ref, b_ref,
                out_ref, xcat_ref, acc_ref, *, tiles_per_seq):
    i = pl.program_id(0)
    halo = jnp.where(i % tiles_per_seq == 0, 0.0, halo_ref[...])
    xcat_ref[0:HALO, :] = halo.astype(BF16)
    xcat_ref[HALO:, :] = x_ref[...].astype(BF16)
    xcat = xcat_ref[...]

    def conv(col0, width):
        cols = slice(col0, col0 + width)
        hfull = jnp.dot(xcat, wup_ref[:, cols], preferred_element_type=F32)
        out = bconv_ref[:, cols] + wconv_ref[CONV_WIDTH - 1:CONV_WIDTH, cols] * hfull[HALO:, :]
        for back in range(1, CONV_WIDTH):
            shifted = pltpu.roll(hfull, back, axis=0)[HALO:, :]
            out = out + wconv_ref[CONV_WIDTH - 1 - back:CONV_WIDTH - back, cols] * shifted
        return out

    for col0, width in FF_CHUNKS:
        a = conv(col0, width)
        gv = conv(D_FF + col0, width)
        f = 0.5 * a * (1.0 + lax.erf(a * (2.0 ** -0.5))) * gv
        y = jnp.dot(f.astype(BF16), wd_ref[col0:col0 + width, :], preferred_element_type=F32)
        if col0 == 0:
            acc_ref[...] = y
        else:
            acc_ref[...] += y

    z = ALPHA * x_ref[...] + acc_ref[...]
    out_ref[...] = _layer_norm(z, g_ref[...], b_ref[...])


def _ffn(x1, w_up, w_conv, b_conv, w_down, g, b, seq, tm):
    m = x1.shape[0]
    halo_blocks = tm // HALO
    kernel = functools.partial(_ffn_kernel, tiles_per_seq=seq // tm)
    full = lambda a: pl.BlockSpec(a.shape, lambda i: (0, 0))
    once = lambda a: pl.BlockSpec(a.shape, lambda i: (0, 0), pipeline_mode=pl.Buffered(1))
    return pl.pallas_call(
        kernel,
        grid=(m // tm,),
        in_specs=[
            pl.BlockSpec((tm, D_MODEL), lambda i: (i, 0)),
            pl.BlockSpec((HALO, D_MODEL), lambda i: (jnp.maximum(i * halo_blocks - 1, 0), 0)),
            once(w_up), full(w_conv), full(b_conv), once(w_down), full(g), full(b),
        ],
        out_specs=pl.BlockSpec((tm, D_MODEL), lambda i: (i, 0)),
        out_shape=jax.ShapeDtypeStruct((m, D_MODEL), F32),
        scratch_shapes=[pltpu.VMEM((HALO + tm, D_MODEL), BF16), pltpu.VMEM((tm, D_MODEL), F32)],
        compiler_params=pltpu.CompilerParams(dimension_semantics=("parallel",), vmem_limit_bytes=VMEM_LIMIT),
        name="ffn_ln2",
    )(x1, x1, w_up, w_conv, b_conv, w_down, g, b)


def kernel(x, w_in, b_gate, lambda_q1, lambda_k1, lambda_q2, lambda_k2, subln_w, w_pa, w_pb, w_o, ln1_g, ln1_b,
           w_up, w_conv, b_conv, w_down, ln2_g, ln2_b):
    bsz, seq, d = x.shape
    assert (seq, d) == (4096, D_MODEL) and w_in.shape[0] == DEPTH
    slopes_a = jnp.asarray(_alibi_slopes(A_HEADS))
    for l in range(DEPTH):
        lambda_init = 0.8 - 0.6 * math.exp(-0.3 * l)
        x2 = x.reshape(bsz * seq, d)
        wb = w_in[l].astype(BF16)
        proj_a, gates, xb = _projection_wide(x2, wb, b_gate[l][None, :], tm=PROJ_ROWS)
        proj_groups = _projection_groups(xb, wb, tm=PROJ_ROWS)
        oa = _diff_attention(proj_a.reshape(bsz, seq, 3 * A_COLS), slopes_a, lambda_q1[l][None],
                             lambda_k1[l][None], lambda_q2[l][None], lambda_k2[l][None], subln_w[l][:, None],
                             lambda_init, t=512)
        obs, lses = [], []
        for pb, (window, dil) in zip(proj_groups, B_PATTERNS):
            assert window // dil == STEPS
            o, lse = _dilated_attention(pb, bsz, seq, dil)
            obs.append(o)
            lses.append(lse)

        x1 = _merge(oa.reshape(bsz * seq, A_COLS), obs, lses, gates, x2,
                    w_pa[l].astype(BF16), w_pb[l].astype(BF16), w_o[l].astype(BF16),
                    ln1_g[l][None], ln1_b[l][None], tm=1024)
        x2 = _ffn(x1, w_up[l].astype(BF16), w_conv[l], b_conv[l][None], w_down[l].astype(BF16),
                  ln2_g[l][None], ln2_b[l][None], seq, tm=1024)
        x = x2.reshape(bsz, seq, d)
    return x
```

```python
import functools
import math

import numpy as np
import jax
import jax.numpy as jnp
from jax import lax
from jax.experimental import pallas as pl
from jax.experimental.pallas import tpu as pltpu

BF16 = jnp.bfloat16
F32 = jnp.float32

D_MODEL = 1024
A_HEADS = 8
A_HEAD_DIM = 64
B_PATTERNS = ((128, 1), (512, 4), (2048, 16))
B_HEADS = 8
B_HEAD_DIM = 64
D_FF = 2816
CONV_WIDTH = 3
EPS = 1e-5
DEPTH = 1
ALPHA = (2.0 * DEPTH) ** 0.25

A_COLS = A_HEADS * 2 * A_HEAD_DIM
B_COLS = B_HEADS * B_HEAD_DIM
N_GROUPS = len(B_PATTERNS)
QKV_COLS = 3 * A_COLS + 3 * N_GROUPS * B_COLS
PAIR = 2 * A_HEAD_DIM
STEPS = 128
BF16_ROWS = 16
LOG2E = math.log2(math.e)
BF16_EXACT = 256
SLOPE_TERMS = 3
PROJ_ROWS = 2048
MXU_WIDTH = 256
FF_CHUNKS = ((0, 6 * MXU_WIDTH), (6 * MXU_WIDTH, 5 * MXU_WIDTH))

VMEM_LIMIT = 56 * 1024 * 1024


def _alibi_slopes(n):
    return np.power(np.float32(2.0), -8.0 * (np.arange(n, dtype=np.float32) + 1) / n).astype(np.float32)


def _nt_dot(a, b):
    return lax.dot_general(a, b, (((1,), (1,)), ((), ())), preferred_element_type=F32)


def _layer_norm(z, g, b):
    mu = jnp.mean(z, axis=-1, keepdims=True)
    zc = z - mu
    var = jnp.mean(zc * zc, axis=-1, keepdims=True)
    return zc * lax.rsqrt(var + EPS) * g + b


WIDE_COLS = 1024
A_TILES = 3 * A_COLS // WIDE_COLS
GROUP_TILES = 3
SPLIT = 4


def _proj_wide_kernel(x_ref, wa_ref, wg_lo_ref, wg_hi_ref, b_ref, pa_ref, g_ref, xb_ref):
    j = pl.program_id(1)

    @pl.when(j == 0)
    def _():
        xb_ref[...] = x_ref[...].astype(BF16)

    @pl.when(j < A_TILES)
    def _():
        pa_ref[...] = jnp.dot(xb_ref[...], wa_ref[...], preferred_element_type=F32).astype(pa_ref.dtype)

    @pl.when(j >= A_TILES)
    def _():
        wg = jnp.concatenate([wg_lo_ref[...], wg_hi_ref[...]], axis=1)
        z = jnp.dot(xb_ref[...], wg, preferred_element_type=F32) + b_ref[...]
        g_ref[...] = (0.5 * jnp.tanh(0.5 * z) + 0.5).astype(g_ref.dtype)


def _projection_wide(x2, wb, b_gate, tm):
    m, k = x2.shape
    tn = WIDE_COLS
    gate_tiles = b_gate.shape[1] // tn
    gate_first = QKV_COLS // (tn // 2)
    clamp = lambda j, first, count: jnp.clip(j - first, 0, count - 1)
    return pl.pallas_call(
        _proj_wide_kernel,
        grid=(m // tm, A_TILES + gate_tiles),
        in_specs=[pl.BlockSpec((tm, k), lambda i, j: (i, 0)),
                  pl.BlockSpec((k, tn), lambda i, j: (0, clamp(j, 0, A_TILES))),
                  pl.BlockSpec((k, tn // 2), lambda i, j: (0, gate_first + 2 * clamp(j, A_TILES, gate_tiles))),
                  pl.BlockSpec((k, tn // 2), lambda i, j: (0, gate_first + 2 * clamp(j, A_TILES, gate_tiles) + 1)),
                  pl.BlockSpec((1, tn), lambda i, j: (0, clamp(j, A_TILES, gate_tiles)))],
        out_specs=[pl.BlockSpec((tm, tn), lambda i, j: (i, clamp(j, 0, A_TILES))),
                   pl.BlockSpec((tm, tn), lambda i, j: (i, clamp(j, A_TILES, gate_tiles))),
                   pl.BlockSpec((tm, k), lambda i, j: (i, 0))],
        out_shape=[jax.ShapeDtypeStruct((m, A_TILES * tn), BF16),
                   jax.ShapeDtypeStruct((m, gate_tiles * tn), BF16),
                   jax.ShapeDtypeStruct((m, k), BF16)],
        compiler_params=pltpu.CompilerParams(
            dimension_semantics=("parallel", "arbitrary"), vmem_limit_bytes=VMEM_LIMIT),
        name="proj_wide",
    )(x2, wb, wb, wb, b_gate)


def _proj_groups_kernel(x_ref, w_ref, *rest):
    group_refs, (acc_ref, tmp_ref) = rest[:N_GROUPS], rest[N_GROUPS:]
    j = pl.program_id(1)
    lane_tiles, tm, _ = acc_ref.shape
    for g, (o_ref, (_, dil)) in enumerate(zip(group_refs, B_PATTERNS)):
        @pl.when((j >= g * GROUP_TILES) & (j < (g + 1) * GROUP_TILES))
        def _(o_ref=o_ref, dil=dil):
            acc = jnp.dot(x_ref[...], w_ref[...], preferred_element_type=F32)
            if dil == 1:
                o_ref[0, 0] = acc.astype(o_ref.dtype)
                return
            for c in range(lane_tiles):
                acc_ref[c] = acc[:, c * PAIR:(c + 1) * PAIR]
            for c in range(lane_tiles):
                cols = slice(c * PAIR, (c + 1) * PAIR)
                if dil == SPLIT:
                    for r in range(dil):
                        o_ref[0, r, :, cols] = acc_ref[c, pl.ds(r, tm // dil, stride=dil), :].astype(o_ref.dtype)
                else:
                    assert dil == SPLIT * SPLIT
                    part = tm // SPLIT
                    for r in range(SPLIT):
                        tmp_ref[c, r * part:(r + 1) * part, :] = acc_ref[c, pl.ds(r, part, stride=SPLIT), :]
                    for r in range(dil):
                        start = (r % SPLIT) * part + r // SPLIT
                        o_ref[0, r, :, cols] = \
                            tmp_ref[c, pl.ds(start, tm // dil, stride=SPLIT), :].astype(o_ref.dtype)


def _projection_groups(xb, wb, tm):
    m, k = xb.shape
    tn = B_COLS
    clamp = lambda j, first: jnp.clip(j - first, 0, GROUP_TILES - 1)
    w_block = lambda j: 3 * A_COLS // tn + (j % GROUP_TILES) * N_GROUPS + j // GROUP_TILES
    return pl.pallas_call(
        _proj_groups_kernel,
        grid=(m // tm, N_GROUPS * GROUP_TILES),
        in_specs=[pl.BlockSpec((tm, k), lambda i, j: (i, 0)),
                  pl.BlockSpec((k, tn), lambda i, j: (0, w_block(j)))],
        out_specs=[pl.BlockSpec((1, dil, tm // dil, tn), lambda i, j, g=g: (i, 0, 0, clamp(j, g * GROUP_TILES)))
                   for g, (_, dil) in enumerate(B_PATTERNS)],
        out_shape=[jax.ShapeDtypeStruct((m // tm, dil, tm // dil, GROUP_TILES * tn), BF16)
                   for _, dil in B_PATTERNS],
        scratch_shapes=[pltpu.VMEM((tn // PAIR, tm, PAIR), F32), pltpu.VMEM((tn // PAIR, tm, PAIR), F32)],
        compiler_params=pltpu.CompilerParams(
            dimension_semantics=("parallel", "arbitrary"), vmem_limit_bytes=VMEM_LIMIT),
        name="proj_groups",
    )(xb, wb)


def _diff_attn_kernel(slopes_ref, q_ref, k_ref, v_ref, lq1_ref, lk1_ref, lq2_ref, lk2_ref, subln_ref,
                      o_ref, vt_ref, mask_ref, kaug_ref, sa_ref, sb_ref, ma_ref, mb_ref, acc_ref, *,
                      t, lambda_init):
    slope = slopes_ref[pl.program_id(1)] * LOG2E
    nchunks = vt_ref.shape[0]

    @pl.when((pl.program_id(0) == 0) & (pl.program_id(1) == 0))
    def _():
        krow = lax.broadcasted_iota(jnp.int32, (t, t), 0)
        qcol = lax.broadcasted_iota(jnp.int32, (t, t), 1)
        mask_ref[...] = jnp.where(krow <= qcol, 0.0, -jnp.inf)
        r = lax.broadcasted_iota(jnp.int32, (t, PAIR), 0)
        c = lax.broadcasted_iota(jnp.int32, (t, PAIR), 1)
        r_low = (r % BF16_EXACT).astype(F32)
        r_high = (r - r % BF16_EXACT).astype(F32)
        kaug_ref[...] = jnp.where(c < SLOPE_TERMS, r_low,
                                  jnp.where(c < 2 * SLOPE_TERMS, r_high, 0.0)).astype(BF16)

    rest = jnp.full((PAIR, t), slope, F32)
    wrow = lax.broadcasted_iota(jnp.int32, (PAIR, t), 0)
    waug = jnp.zeros((PAIR, t), F32)
    for term in range(SLOPE_TERMS):
        part = rest.astype(BF16).astype(F32)
        waug = jnp.where((wrow == term) | (wrow == SLOPE_TERMS + term), part, waug)
        rest = rest - part
    waug = waug.astype(BF16)
    dim = lax.broadcasted_iota(jnp.int32, (PAIR, t), 0)

    lam = (jnp.exp(jnp.sum(lq1_ref[...] * lk1_ref[...], axis=-1, keepdims=True))
           - jnp.exp(jnp.sum(lq2_ref[...] * lk2_ref[...], axis=-1, keepdims=True))
           + lambda_init)
    neg = jnp.full((1, t), -jnp.inf, F32)
    half = t // 2
    s_refs = ((sa_ref, ma_ref), (sb_ref, mb_ref))

    for qi in range(nchunks):
        rows = slice(qi * t, (qi + 1) * t)
        acc = acc_ref.at[qi % 2]
        vt_ref[qi, 0:PAIR, :] = v_ref[0, rows, :].astype(F32).T.astype(BF16)
        vt_ref[qi, PAIR:, :] = jnp.ones((BF16_ROWS, t), BF16)
        qt = q_ref[0, rows, :].astype(F32).T * (A_HEAD_DIM ** -0.5 * LOG2E)
        qt_maps = (jnp.concatenate([jnp.where(dim < A_HEAD_DIM, qt, 0.0).astype(BF16), waug], axis=0),
                   jnp.concatenate([jnp.where(dim < A_HEAD_DIM, 0.0, qt).astype(BF16), waug], axis=0))
        acc[...] = jnp.zeros(acc.shape, F32)

        def scores(j, buf, qt_maps=qt_maps):
            s_ref, max_ref = buf
            kc = k_ref[0, pl.ds(pl.multiple_of(j * t, t), t), :]
            kc = jnp.concatenate([kc, kaug_ref[...]], axis=1)
            for mp in range(2):
                s = jnp.dot(kc, qt_maps[mp], preferred_element_type=F32)
                s_ref[mp] = s
                max_ref[mp] = jnp.max(s, axis=0, keepdims=True)

        def update(j, buf, stats, qi=qi, acc=acc):
            s_ref, max_ref = buf
            vt = vt_ref[j]
            shift = slope * jnp.asarray((j - qi) * t, F32)
            out = []
            for mp in range(2):
                m_new = jnp.maximum(stats[mp], max_ref[mp] + shift)
                p = jnp.exp2(s_ref[mp] - (m_new - shift)).astype(BF16)
                alpha = jnp.exp2(stats[mp] - m_new)
                acc[mp] = alpha * acc[mp] + jnp.dot(vt, p, preferred_element_type=F32)
                out.append(m_new)
            return tuple(out)

        def scores_diag(buf, qi=qi, qt_maps=qt_maps):
            s_ref, max_ref = buf
            k_lo = jnp.concatenate([k_ref[0, qi * t:qi * t + half, :], kaug_ref[0:half, :]], axis=1)
            k_hi = jnp.concatenate([k_ref[0, qi * t + half:(qi + 1) * t, :], kaug_ref[half:, :]], axis=1)
            for mp in range(2):
                s_lo = jnp.dot(k_lo, qt_maps[mp], preferred_element_type=F32) + mask_ref[0:half, :]
                s_hi = jnp.dot(k_hi, qt_maps[mp][:, half:], preferred_element_type=F32) \
                    + mask_ref[half:, half:]
                s_ref[mp, 0:half, :] = s_lo
                s_ref[mp, half:, half:] = s_hi
                max_lo = jnp.max(s_lo, axis=0, keepdims=True)
                max_ref[mp, :, 0:half] = max_lo[:, :half]
                max_ref[mp, :, half:] = jnp.maximum(max_lo[:, half:], jnp.max(s_hi, axis=0, keepdims=True))

        def update_diag(buf, stats, qi=qi, acc=acc):
            s_ref, max_ref = buf
            for mp in range(2):
                m_new = jnp.maximum(stats[mp], max_ref[mp])
                p_lo = jnp.exp2(s_ref[mp, 0:half, :] - m_new).astype(BF16)
                p_hi = jnp.exp2(s_ref[mp, half:, half:] - m_new[:, half:]).astype(BF16)
                alpha = jnp.exp2(stats[mp] - m_new)
                pv = jnp.dot(vt_ref[qi, :, 0:half], p_lo, preferred_element_type=F32)
                pv_hi = jnp.dot(vt_ref[qi, :, half:], p_hi, preferred_element_type=F32)
                acc[mp, :, 0:half] = alpha[:, :half] * acc[mp, :, 0:half] + pv[:, :half]
                acc[mp, :, half:] = alpha[:, half:] * acc[mp, :, half:] + (pv[:, half:] + pv_hi)

        first, second = s_refs

        def pair(i, stats, scores=scores, update=update, first=first, second=second):
            j = 2 * i
            scores(j + 1, second)
            stats = update(j, first, stats)
            scores(j + 2, first)
            return update(j + 1, second, stats)

        idx = jnp.int32
        if qi == 0:
            scores_diag(first)
            update_diag(first, (neg, neg))
        else:
            scores(idx(0), first)
            trips = (qi - 1) // 2
            if trips <= 2:
                stats = lax.fori_loop(0, trips, pair, (neg, neg), unroll=True)
            else:
                stats = lax.fori_loop(1, trips, pair, pair(idx(0), (neg, neg)))
            if qi % 2 == 1:
                scores_diag(second)
                update_diag(second, update(idx(qi - 1), first, stats))
            else:
                scores(idx(qi - 1), second)
                stats = update(idx(qi - 2), first, stats)
                scores_diag(first)
                update_diag(first, update(idx(qi - 1), second, stats))
        if qi % 2 == 0:
            s_refs = (second, first)

        a1, a2 = acc[0], acc[1]
        ot = a1[:PAIR] / a1[PAIR:PAIR + 1] - lam * (a2[:PAIR] / a2[PAIR:PAIR + 1])
        ot = ot * lax.rsqrt(jnp.mean(ot * ot, axis=0, keepdims=True) + EPS)
        ot = ot * subln_ref[...] * (1.0 - lambda_init)
        o_ref[0, rows, :] = ot.T.astype(o_ref.dtype)


def _diff_attention(p3, slopes, lq1, lk1, lq2, lk2, subln_col, lambda_init, t):
    bsz, seq, _ = p3.shape
    vec = lambda n: pl.BlockSpec((1, n), lambda b, h: (0, 0))
    kernel = functools.partial(_diff_attn_kernel, t=t, lambda_init=lambda_init)
    return pl.pallas_call(
        kernel,
        grid=(bsz, A_HEADS),
        in_specs=[
            pl.BlockSpec(memory_space=pltpu.SMEM),
            pl.BlockSpec((1, seq, PAIR), lambda b, h: (b, 0, h)),
            pl.BlockSpec((1, seq, PAIR), lambda b, h: (b, 0, A_HEADS + h)),
            pl.BlockSpec((1, seq, PAIR), lambda b, h: (b, 0, 2 * A_HEADS + h)),
            vec(A_HEAD_DIM), vec(A_HEAD_DIM), vec(A_HEAD_DIM), vec(A_HEAD_DIM),
            pl.BlockSpec((PAIR, 1), lambda b, h: (0, 0)),
        ],
        out_specs=pl.BlockSpec((1, seq, PAIR), lambda b, h: (b, 0, h)),
        out_shape=jax.ShapeDtypeStruct((bsz, seq, A_COLS), BF16),
        scratch_shapes=[pltpu.VMEM((seq // t, PAIR + BF16_ROWS, t), BF16), pltpu.VMEM((t, t), F32),
                        pltpu.VMEM((t, PAIR), BF16),
                        pltpu.VMEM((2, t, t), F32), pltpu.VMEM((2, t, t), F32),
                        pltpu.VMEM((2, 1, t), F32), pltpu.VMEM((2, 1, t), F32),
                        pltpu.VMEM((2, 2, PAIR + BF16_ROWS, t), F32)],
        compiler_params=pltpu.CompilerParams(
            dimension_semantics=("arbitrary", "arbitrary"), vmem_limit_bytes=VMEM_LIMIT),
        name="diff_attn",
    )(slopes, p3, p3, p3, lq1, lk1, lq2, lk2, subln_col)


def _dilated_kernel(q_ref, k_ref, v_ref, kprev_ref, vprev_ref, o_ref, lse_ref, bias_ref, o_scr, lse_scr,
                    *maybe_order_scr, dil, slopes, tiles_per_seq):
    i = pl.program_id(0)
    blocks = q_ref.shape[2] // STEPS
    tm = o_ref.shape[0]
    two_pass = bool(maybe_order_scr)
    if two_pass:
        assert dil == SPLIT * SPLIT and blocks == 1
        order_scr, = maybe_order_scr

    @pl.when(i == 0)
    def _():
        qi = lax.broadcasted_iota(jnp.int32, (STEPS, 2 * STEPS), 0)
        kj = lax.broadcasted_iota(jnp.int32, (STEPS, 2 * STEPS), 1)
        step = qi + STEPS - kj
        window = (step >= 0) & (step <= STEPS)
        dist = (step * dil).astype(F32)
        for hd in range(B_HEADS):
            alibi = -float(slopes[hd]) * LOG2E * dist
            bias_ref[0, hd] = jnp.where(window & (kj >= STEPS), alibi, -jnp.inf)
            bias_ref[1, hd] = jnp.where(window, alibi, -jnp.inf)

    lane = lax.broadcasted_iota(jnp.int32, (STEPS, PAIR), 1)
    low = lane < B_HEAD_DIM
    low_keys = lax.broadcasted_iota(jnp.int32, (2 * STEPS, PAIR), 1) < B_HEAD_DIM
    ones_even = jnp.where(low_keys, 1.0, 0.0).astype(BF16)
    ones_odd = jnp.where(low_keys, 0.0, 1.0).astype(BF16)

    first_variant = (i % tiles_per_seq != 0).astype(jnp.int32)
    for u in range(dil * blocks):
        r, nl = divmod(u, blocks)
        variant = 1 if nl > 0 else first_variant
        cur = slice(nl * STEPS, (nl + 1) * STEPS)
        before = slice((nl - 1) * STEPS, nl * STEPS)
        q = q_ref[0, r, cur, :].astype(F32) * (B_HEAD_DIM ** -0.5 * LOG2E)
        k_before = k_ref[0, r, before, :] if nl > 0 else kprev_ref[0, r]
        v_before = v_ref[0, r, before, :] if nl > 0 else vprev_ref[0, r]
        kk = jnp.concatenate([k_before, k_ref[0, r, cur, :]], axis=0)
        vv = jnp.concatenate([v_before, v_ref[0, r, cur, :]], axis=0)
        start = nl * (STEPS * dil) + r
        if two_pass:
            rows = pl.ds((r % SPLIT) * (tm // SPLIT) + r // SPLIT, STEPS, stride=SPLIT)
        else:
            rows = pl.ds(start, STEPS, stride=dil) if dil > 1 else pl.ds(start, STEPS)
        for pr in range(B_HEADS // 2):
            cols = slice(pr * PAIR, (pr + 1) * PAIR)
            qp, kp = q[:, cols], kk[:, cols]
            vp = vv[:, cols].astype(F32)
            v_blocks = jnp.concatenate([
                jnp.concatenate([jnp.where(low_keys, vp, 0.0).astype(BF16), ones_even], axis=1),
                jnp.concatenate([jnp.where(low_keys, 0.0, vp).astype(BF16), ones_odd], axis=1)], axis=0)
            probs, maxes = [], []
            for par in range(2):
                qm = jnp.where(low, qp, 0.0) if par == 0 else jnp.where(low, 0.0, qp)
                s = _nt_dot(qm.astype(BF16), kp) + bias_ref[variant, 2 * pr + par]
                m = jnp.max(s, axis=-1, keepdims=True)
                probs.append(jnp.exp2(s - m).astype(BF16))
                maxes.append(m)
            o_den = jnp.dot(jnp.concatenate(probs, axis=1), v_blocks, preferred_element_type=F32)
            den = o_den[:, PAIR:]
            o_scr[pr, rows, :] = o_den[:, :PAIR] / den
            lse_scr[pr, rows, :] = jnp.where(low, maxes[0], maxes[1]) + jnp.log2(den)

    for src, dst in ((o_scr, o_ref), (lse_scr, lse_ref)):
        for pr in range(B_HEADS // 2):
            if two_pass:
                part = tm // SPLIT
                for r in range(SPLIT):
                    order_scr[pr, pl.ds(r, part, stride=SPLIT), :] = src[pr, r * part:(r + 1) * part, :]
            ordered = order_scr if two_pass else src
            dst[:, pr * PAIR:(pr + 1) * PAIR] = ordered[pr].astype(dst.dtype)


def _dilated_attention(pb, bsz, seq, dil):
    tiles, _, per_res, _ = pb.shape
    tm = dil * per_res
    tile_blk = (1, dil, per_res, B_COLS)
    prev_blk = (1, dil, STEPS, B_COLS)
    last = per_res // STEPS - 1
    prev = lambda i: jnp.maximum(i - 1, 0)
    kernel = functools.partial(_dilated_kernel, dil=dil, slopes=_alibi_slopes(B_HEADS),
                               tiles_per_seq=tiles // bsz)
    out_block = pl.BlockSpec((tm, B_COLS), lambda i: (i, 0))
    return pl.pallas_call(
        kernel,
        grid=(tiles,),
        in_specs=[
            pl.BlockSpec(tile_blk, lambda i: (i, 0, 0, 0)),
            pl.BlockSpec(tile_blk, lambda i: (i, 0, 0, 1)),
            pl.BlockSpec(tile_blk, lambda i: (i, 0, 0, 2)),
            pl.BlockSpec(prev_blk, lambda i: (prev(i), 0, last, 1)),
            pl.BlockSpec(prev_blk, lambda i: (prev(i), 0, last, 2)),
        ],
        out_specs=[out_block, out_block],
        out_shape=[jax.ShapeDtypeStruct((bsz * seq, B_COLS), BF16),
                   jax.ShapeDtypeStruct((bsz * seq, B_COLS), F32)],
        scratch_shapes=[pltpu.VMEM((2, B_HEADS, STEPS, 2 * STEPS), F32),
                        pltpu.VMEM((B_HEADS // 2, tm, PAIR), F32), pltpu.VMEM((B_HEADS // 2, tm, PAIR), F32)]
        + [pltpu.VMEM((B_HEADS // 2, tm, PAIR), F32)] * (dil == SPLIT * SPLIT),
        compiler_params=pltpu.CompilerParams(
            dimension_semantics=("arbitrary",), vmem_limit_bytes=VMEM_LIMIT),
        name=f"dilated_attn_d{dil}",
    )(pb, pb, pb, pb, pb)


MERGE_SLAB_ROWS = 256


def _merge_kernel(oa_ref, o0_ref, o1_ref, o2_ref, l0_ref, l1_ref, l2_ref, ga_ref, gb_ref, x_ref,
                  wpa_ref, wpb_ref, wo_ref, g_ref, b_ref, out_ref):
    slab = MERGE_SLAB_ROWS
    for part in range(out_ref.shape[0] // slab):
        rows = slice(part * slab, (part + 1) * slab)
        l0, l1, l2 = l0_ref[rows, :], l1_ref[rows, :], l2_ref[rows, :]
        mx = jnp.maximum(jnp.maximum(l0, l1), l2)
        e0, e1, e2 = jnp.exp2(l0 - mx), jnp.exp2(l1 - mx), jnp.exp2(l2 - mx)
        ob = (e0 * o0_ref[rows, :].astype(F32) + e1 * o1_ref[rows, :].astype(F32)
              + e2 * o2_ref[rows, :].astype(F32)) / (e0 + e1 + e2)
        ya = jnp.dot(oa_ref[rows, :], wpa_ref[...], preferred_element_type=F32)
        yb = jnp.dot(ob.astype(BF16), wpb_ref[...], preferred_element_type=F32)
        y = ga_ref[rows, :].astype(F32) * ya + gb_ref[rows, :].astype(F32) * yb
        z = ALPHA * x_ref[rows, :] + jnp.dot(y.astype(BF16), wo_ref[...], preferred_element_type=F32)
        out_ref[rows, :] = _layer_norm(z, g_ref[...], b_ref[...])


def _merge(oa, obs, lses, gates, x2, wpa, wpb, wo, g, b, tm):
    m = x2.shape[0]
    row = lambda w: pl.BlockSpec((tm, w), lambda i: (i, 0))
    full = lambda a: pl.BlockSpec(a.shape, lambda i: (0, 0))
    once = lambda a: pl.BlockSpec(a.shape, lambda i: (0, 0), pipeline_mode=pl.Buffered(1))
    return pl.pallas_call(
        _merge_kernel,
        grid=(m // tm,),
        in_specs=[row(A_COLS), row(B_COLS), row(B_COLS), row(B_COLS), row(B_COLS), row(B_COLS), row(B_COLS),
                  pl.BlockSpec((tm, D_MODEL), lambda i: (i, 0)), pl.BlockSpec((tm, D_MODEL), lambda i: (i, 1)),
                  row(D_MODEL), once(wpa), once(wpb), once(wo), full(g), full(b)],
        out_specs=row(D_MODEL),
        out_shape=jax.ShapeDtypeStruct((m, D_MODEL), F32),
        compiler_params=pltpu.CompilerParams(dimension_semantics=("parallel",), vmem_limit_bytes=VMEM_LIMIT),
        name="merge_ln1",
    )(oa, *obs, *lses, gates, gates, x2, wpa, wpb, wo, g, b)


HALO = BF16_ROWS


def _ffn_kernel(x_ref, halo_ref, wup_ref, wconv_ref, bconv_ref, wd_ref, g_ref, b_ref,
                out_ref, xcat_ref, acc_ref, *, tiles_per_seq):
    i = pl.program_id(0)
    halo = jnp.where(i % tiles_per_seq == 0, 0.0, halo_ref[...])
    xcat_ref[0:HALO, :] = halo.astype(BF16)
    xcat_ref[HALO:, :] = x_ref[...].astype(BF16)
    xcat = xcat_ref[...]

    def conv(col0, width):
        cols = slice(col0, col0 + width)
        hfull = jnp.dot(xcat, wup_ref[:, cols], preferred_element_type=F32)
        out = bconv_ref[:, cols] + wconv_ref[CONV_WIDTH - 1:CONV_WIDTH, cols] * hfull[HALO:, :]
        for back in range(1, CONV_WIDTH):
            shifted = pltpu.roll(hfull, back, axis=0)[HALO:, :]
            out = out + wconv_ref[CONV_WIDTH - 1 - back:CONV_WIDTH - back, cols] * shifted
        return out

    for col0, width in FF_CHUNKS:
        a = conv(col0, width)
        gv = conv(D_FF + col0, width)
        f = 0.5 * a * (1.0 + lax.erf(a * (2.0 ** -0.5))) * gv
        y = jnp.dot(f.astype(BF16), wd_ref[col0:col0 + width, :], preferred_element_type=F32)
        if col0 == 0:
            acc_ref[...] = y
        else:
            acc_ref[...] += y

    z = ALPHA * x_ref[...] + acc_ref[...]
    out_ref[...] = _layer_norm(z, g_ref[...], b_ref[...])


def _ffn(x1, w_up, w_conv, b_conv, w_down, g, b, seq, tm):
    m = x1.shape[0]
    halo_blocks = tm // HALO
    kernel = functools.partial(_ffn_kernel, tiles_per_seq=seq // tm)
    full = lambda a: pl.BlockSpec(a.shape, lambda i: (0, 0))
    once = lambda a: pl.BlockSpec(a.shape, lambda i: (0, 0), pipeline_mode=pl.Buffered(1))
    return pl.pallas_call(
        kernel,
        grid=(m // tm,),
        in_specs=[
            pl.BlockSpec((tm, D_MODEL), lambda i: (i, 0)),
            pl.BlockSpec((HALO, D_MODEL), lambda i: (jnp.maximum(i * halo_blocks - 1, 0), 0)),
            once(w_up), full(w_conv), full(b_conv), once(w_down), full(g), full(b),
        ],
        out_specs=pl.BlockSpec((tm, D_MODEL), lambda i: (i, 0)),
        out_shape=jax.ShapeDtypeStruct((m, D_MODEL), F32),
        scratch_shapes=[pltpu.VMEM((HALO + tm, D_MODEL), BF16), pltpu.VMEM((tm, D_MODEL), F32)],
        compiler_params=pltpu.CompilerParams(dimension_semantics=("parallel",), vmem_limit_bytes=VMEM_LIMIT),
        name="ffn_ln2",
    )(x1, x1, w_up, w_conv, b_conv, w_down, g, b)


def kernel(x, w_in, b_gate, lambda_q1, lambda_k1, lambda_q2, lambda_k2, subln_w, w_pa, w_pb, w_o, ln1_g, ln1_b,
           w_up, w_conv, b_conv, w_down, ln2_g, ln2_b):
    bsz, seq, d = x.shape
    assert (seq, d) == (4096, D_MODEL) and w_in.shape[0] == DEPTH
    slopes_a = jnp.asarray(_alibi_slopes(A_HEADS))
    for l in range(DEPTH):
        lambda_init = 0.8 - 0.6 * math.exp(-0.3 * l)
        x2 = x.reshape(bsz * seq, d)
        wb = w_in[l].astype(BF16)
        proj_a, gates, xb = _projection_wide(x2, wb, b_gate[l][None, :], tm=PROJ_ROWS)
        proj_groups = _projection_groups(xb, wb, tm=PROJ_ROWS)
        oa = _diff_attention(proj_a.reshape(bsz, seq, 3 * A_COLS), slopes_a, lambda_q1[l][None],
                             lambda_k1[l][None], lambda_q2[l][None], lambda_k2[l][None], subln_w[l][:, None],
                             lambda_init, t=512)
        obs, lses = [], []
        for pb, (window, dil) in zip(proj_groups, B_PATTERNS):
            assert window // dil == STEPS
            o, lse = _dilated_attention(pb, bsz, seq, dil)
            obs.append(o)
            lses.append(lse)

        x1 = _merge(oa.reshape(bsz * seq, A_COLS), obs, lses, gates, x2,
                    w_pa[l].astype(BF16), w_pb[l].astype(BF16), w_o[l].astype(BF16),
                    ln1_g[l][None], ln1_b[l][None], tm=1024)
        x2 = _ffn(x1, w_up[l].astype(BF16), w_conv[l], b_conv[l][None], w_down[l].astype(BF16),
                  ln2_g[l][None], ln2_b[l][None], seq, tm=1024)
        x = x2.reshape(bsz, seq, d)
    return x
```

```python
import functools
import math

import numpy as np
import jax
import jax.numpy as jnp
from jax import lax
from jax.experimental import pallas as pl
from jax.experimental.pallas import tpu as pltpu

BF16 = jnp.bfloat16
F32 = jnp.float32

D_MODEL = 1024
A_HEADS = 8
A_HEAD_DIM = 64
B_PATTERNS = ((128, 1), (512, 4), (2048, 16))
B_HEADS = 8
B_HEAD_DIM = 64
D_FF = 2816
CONV_WIDTH = 3
EPS = 1e-5
DEPTH = 1
ALPHA = (2.0 * DEPTH) ** 0.25

A_COLS = A_HEADS * 2 * A_HEAD_DIM
B_COLS = B_HEADS * B_HEAD_DIM
N_GROUPS = len(B_PATTERNS)
QKV_COLS = 3 * A_COLS + 3 * N_GROUPS * B_COLS
PAIR = 2 * A_HEAD_DIM
STEPS = 128
BF16_ROWS = 16
LOG2E = math.log2(math.e)
BF16_EXACT = 256
SLOPE_TERMS = 3
PROJ_ROWS = 2048
MXU_WIDTH = 256
FF_CHUNKS = ((0, 6 * MXU_WIDTH), (6 * MXU_WIDTH, 5 * MXU_WIDTH))

VMEM_LIMIT = 56 * 1024 * 1024


def _alibi_slopes(n):
    return np.power(np.float32(2.0), -8.0 * (np.arange(n, dtype=np.float32) + 1) / n).astype(np.float32)


def _nt_dot(a, b):
    return lax.dot_general(a, b, (((1,), (1,)), ((), ())), preferred_element_type=F32)


def _layer_norm(z, g, b):
    mu = jnp.mean(z, axis=-1, keepdims=True)
    zc = z - mu
    var = jnp.mean(zc * zc, axis=-1, keepdims=True)
    return zc * lax.rsqrt(var + EPS) * g + b


WIDE_COLS = 1024
A_TILES = 3 * A_COLS // WIDE_COLS
GROUP_TILES = 3
SPLIT = 4


def _proj_wide_kernel(x_ref, wa_ref, wg_lo_ref, wg_hi_ref, b_ref, pa_ref, g_ref, xb_ref):
    j = pl.program_id(1)

    @pl.when(j == 0)
    def _():
        xb_ref[...] = x_ref[...].astype(BF16)

    @pl.when(j < A_TILES)
    def _():
        pa_ref[...] = jnp.dot(xb_ref[...], wa_ref[...], preferred_element_type=F32).astype(pa_ref.dtype)

    @pl.when(j >= A_TILES)
    def _():
        wg = jnp.concatenate([wg_lo_ref[...], wg_hi_ref[...]], axis=1)
        z = jnp.dot(xb_ref[...], wg, preferred_element_type=F32) + b_ref[...]
        g_ref[...] = (0.5 * jnp.tanh(0.5 * z) + 0.5).astype(g_ref.dtype)


def _projection_wide(x2, wb, b_gate, tm):
    m, k = x2.shape
    tn = WIDE_COLS
    gate_tiles = b_gate.shape[1] // tn
    gate_first = QKV_COLS // (tn // 2)
    clamp = lambda j, first, count: jnp.clip(j - first, 0, count - 1)
    return pl.pallas_call(
        _proj_wide_kernel,
        grid=(m // tm, A_TILES + gate_tiles),
        in_specs=[pl.BlockSpec((tm, k), lambda i, j: (i, 0)),
                  pl.BlockSpec((k, tn), lambda i, j: (0, clamp(j, 0, A_TILES))),
                  pl.BlockSpec((k, tn // 2), lambda i, j: (0, gate_first + 2 * clamp(j, A_TILES, gate_tiles))),
                  pl.BlockSpec((k, tn // 2), lambda i, j: (0, gate_first + 2 * clamp(j, A_TILES, gate_tiles) + 1)),
                  pl.BlockSpec((1, tn), lambda i, j: (0, clamp(j, A_TILES, gate_tiles)))],
        out_specs=[pl.BlockSpec((tm, tn), lambda i, j: (i, clamp(j, 0, A_TILES))),
                   pl.BlockSpec((tm, tn), lambda i, j: (i, clamp(j, A_TILES, gate_tiles))),
                   pl.BlockSpec((tm, k), lambda i, j: (i, 0))],
        out_shape=[jax.ShapeDtypeStruct((m, A_TILES * tn), BF16),
                   jax.ShapeDtypeStruct((m, gate_tiles * tn), BF16),
                   jax.ShapeDtypeStruct((m, k), BF16)],
        compiler_params=pltpu.CompilerParams(
            dimension_semantics=("parallel", "arbitrary"), vmem_limit_bytes=VMEM_LIMIT),
        name="proj_wide",
    )(x2, wb, wb, wb, b_gate)


def _proj_groups_kernel(x_ref, w_ref, *rest):
    group_refs, (acc_ref, tmp_ref) = rest[:N_GROUPS], rest[N_GROUPS:]
    j = pl.program_id(1)
    lane_tiles, tm, _ = acc_ref.shape
    for g, (o_ref, (_, dil)) in enumerate(zip(group_refs, B_PATTERNS)):
        @pl.when((j >= g * GROUP_TILES) & (j < (g + 1) * GROUP_TILES))
        def _(o_ref=o_ref, dil=dil):
            acc = jnp.dot(x_ref[...], w_ref[...], preferred_element_type=F32)
            if dil == 1:
                o_ref[0, 0] = acc.astype(o_ref.dtype)
                return
            for c in range(lane_tiles):
                acc_ref[c] = acc[:, c * PAIR:(c + 1) * PAIR]
            for c in range(lane_tiles):
                cols = slice(c * PAIR, (c + 1) * PAIR)
                if dil == SPLIT:
                    for r in range(dil):
                        o_ref[0, r, :, cols] = acc_ref[c, pl.ds(r, tm // dil, stride=dil), :].astype(o_ref.dtype)
                else:
                    assert dil == SPLIT * SPLIT
                    part = tm // SPLIT
                    for r in range(SPLIT):
                        tmp_ref[c, r * part:(r + 1) * part, :] = acc_ref[c, pl.ds(r, part, stride=SPLIT), :]
                    for r in range(dil):
                        start = (r % SPLIT) * part + r // SPLIT
                        o_ref[0, r, :, cols] = \
                            tmp_ref[c, pl.ds(start, tm // dil, stride=SPLIT), :].astype(o_ref.dtype)


def _projection_groups(xb, wb, tm):
    m, k = xb.shape
    tn = B_COLS
    clamp = lambda j, first: jnp.clip(j - first, 0, GROUP_TILES - 1)
    w_block = lambda j: 3 * A_COLS // tn + (j % GROUP_TILES) * N_GROUPS + j // GROUP_TILES
    return pl.pallas_call(
        _proj_groups_kernel,
        grid=(m // tm, N_GROUPS * GROUP_TILES),
        in_specs=[pl.BlockSpec((tm, k), lambda i, j: (i, 0)),
                  pl.BlockSpec((k, tn), lambda i, j: (0, w_block(j)))],
        out_specs=[pl.BlockSpec((1, dil, tm // dil, tn), lambda i, j, g=g: (i, 0, 0, clamp(j, g * GROUP_TILES)))
                   for g, (_, dil) in enumerate(B_PATTERNS)],
        out_shape=[jax.ShapeDtypeStruct((m // tm, dil, tm // dil, GROUP_TILES * tn), BF16)
                   for _, dil in B_PATTERNS],
        scratch_shapes=[pltpu.VMEM((tn // PAIR, tm, PAIR), F32), pltpu.VMEM((tn // PAIR, tm, PAIR), F32)],
        compiler_params=pltpu.CompilerParams(
            dimension_semantics=("parallel", "arbitrary"), vmem_limit_bytes=VMEM_LIMIT),
        name="proj_groups",
    )(xb, wb)


def _diff_attn_kernel(slopes_ref, q_ref, k_ref, v_ref, lq1_ref, lk1_ref, lq2_ref, lk2_ref, subln_ref,
                      o_ref, vt_ref, mask_ref, kaug_ref, sa_ref, sb_ref, ma_ref, mb_ref, acc_ref, *,
                      t, lambda_init):
    slope = slopes_ref[pl.program_id(1)] * LOG2E
    nchunks = vt_ref.shape[0]

    @pl.when((pl.program_id(0) == 0) & (pl.program_id(1) == 0))
    def _():
        krow = lax.broadcasted_iota(jnp.int32, (t, t), 0)
        qcol = lax.broadcasted_iota(jnp.int32, (t, t), 1)
        mask_ref[...] = jnp.where(krow <= qcol, 0.0, -jnp.inf)
        r = lax.broadcasted_iota(jnp.int32, (t, PAIR), 0)
        c = lax.broadcasted_iota(jnp.int32, (t, PAIR), 1)
        r_low = (r % BF16_EXACT).astype(F32)
        r_high = (r - r % BF16_EXACT).astype(F32)
        kaug_ref[...] = jnp.where(c < SLOPE_TERMS, r_low,
                                  jnp.where(c < 2 * SLOPE_TERMS, r_high, 0.0)).astype(BF16)

    rest = jnp.full((PAIR, t), slope, F32)
    wrow = lax.broadcasted_iota(jnp.int32, (PAIR, t), 0)
    waug = jnp.zeros((PAIR, t), F32)
    for term in range(SLOPE_TERMS):
        part = rest.astype(BF16).astype(F32)
        waug = jnp.where((wrow == term) | (wrow == SLOPE_TERMS + term), part, waug)
        rest = rest - part
    waug = waug.astype(BF16)
    dim = lax.broadcasted_iota(jnp.int32, (PAIR, t), 0)

    lam = (jnp.exp(jnp.sum(lq1_ref[...] * lk1_ref[...], axis=-1, keepdims=True))
           - jnp.exp(jnp.sum(lq2_ref[...] * lk2_ref[...], axis=-1, keepdims=True))
           + lambda_init)
    neg = jnp.full((1, t), -jnp.inf, F32)
    half = t // 2
    s_refs = ((sa_ref, ma_ref), (sb_ref, mb_ref))

    for qi in range(nchunks):
        rows = slice(qi * t, (qi + 1) * t)
        acc = acc_ref.at[qi % 2]
        vt_ref[qi, 0:PAIR, :] = v_ref[0, rows, :].astype(F32).T.astype(BF16)
        vt_ref[qi, PAIR:, :] = jnp.ones((BF16_ROWS, t), BF16)
        qt = q_ref[0, rows, :].astype(F32).T * (A_HEAD_DIM ** -0.5 * LOG2E)
        qt_maps = (jnp.concatenate([jnp.where(dim < A_HEAD_DIM, qt, 0.0).astype(BF16), waug], axis=0),
                   jnp.concatenate([jnp.where(dim < A_HEAD_DIM, 0.0, qt).astype(BF16), waug], axis=0))
        acc[...] = jnp.zeros(acc.shape, F32)

        def scores(j, buf, qt_maps=qt_maps):
            s_ref, max_ref = buf
            kc = k_ref[0, pl.ds(pl.multiple_of(j * t, t), t), :]
            kc = jnp.concatenate([kc, kaug_ref[...]], axis=1)
            for mp in range(2):
                s = jnp.dot(kc, qt_maps[mp], preferred_element_type=F32)
                s_ref[mp] = s
                max_ref[mp] = jnp.max(s, axis=0, keepdims=True)

        def update(j, buf, stats, qi=qi, acc=acc):
            s_ref, max_ref = buf
            vt = vt_ref[j]
            shift = slope * jnp.asarray((j - qi) * t, F32)
            out = []
            for mp in range(2):
                m_new = jnp.maximum(stats[mp], max_ref[mp] + shift)
                p = jnp.exp2(s_ref[mp] - (m_new - shift)).astype(BF16)
                alpha = jnp.exp2(stats[mp] - m_new)
                acc[mp] = alpha * acc[mp] + jnp.dot(vt, p, preferred_element_type=F32)
                out.append(m_new)
            return tuple(out)

        def scores_diag(buf, qi=qi, qt_maps=qt_maps):
            s_ref, max_ref = buf
            k_lo = jnp.concatenate([k_ref[0, qi * t:qi * t + half, :], kaug_ref[0:half, :]], axis=1)
            k_hi = jnp.concatenate([k_ref[0, qi * t + half:(qi + 1) * t, :], kaug_ref[half:, :]], axis=1)
            for mp in range(2):
                s_lo = jnp.dot(k_lo, qt_maps[mp], preferred_element_type=F32) + mask_ref[0:half, :]
                s_hi = jnp.dot(k_hi, qt_maps[mp][:, half:], preferred_element_type=F32) \
                    + mask_ref[half:, half:]
                s_ref[mp, 0:half, :] = s_lo
                s_ref[mp, half:, half:] = s_hi
                max_lo = jnp.max(s_lo, axis=0, keepdims=True)
                max_ref[mp, :, 0:half] = max_lo[:, :half]
                max_ref[mp, :, half:] = jnp.maximum(max_lo[:, half:], jnp.max(s_hi, axis=0, keepdims=True))

        def update_diag(buf, stats, qi=qi, acc=acc):
            s_ref, max_ref = buf
            for mp in range(2):
                m_new = jnp.maximum(stats[mp], max_ref[mp])
                p_lo = jnp.exp2(s_ref[mp, 0:half, :] - m_new).astype(BF16)
                p_hi = jnp.exp2(s_ref[mp, half:, half:] - m_new[:, half:]).astype(BF16)
                alpha = jnp.exp2(stats[mp] - m_new)
                pv = jnp.dot(vt_ref[qi, :, 0:half], p_lo, preferred_element_type=F32)
                pv_hi = jnp.dot(vt_ref[qi, :, half:], p_hi, preferred_element_type=F32)
                acc[mp, :, 0:half] = alpha[:, :half] * acc[mp, :, 0:half] + pv[:, :half]
                acc[mp, :, half:] = alpha[:, half:] * acc[mp, :, half:] + (pv[:, half:] + pv_hi)

        first, second = s_refs

        def pair(i, stats, scores=scores, update=update, first=first, second=second):
            j = 2 * i
            scores(j + 1, second)
            stats = update(j, first, stats)
            scores(j + 2, first)
            return update(j + 1, second, stats)

        idx = jnp.int32
        if qi == 0:
            scores_diag(first)
            update_diag(first, (neg, neg))
        else:
            scores(idx(0), first)
            trips = (qi - 1) // 2
            if trips <= 2:
                stats = lax.fori_loop(0, trips, pair, (neg, neg), unroll=True)
            else:
                stats = lax.fori_loop(1, trips, pair, pair(idx(0), (neg, neg)))
            if qi % 2 == 1:
                scores_diag(second)
                update_diag(second, update(idx(qi - 1), first, stats))
            else:
                scores(idx(qi - 1), second)
                stats = update(idx(qi - 2), first, stats)
                scores_diag(first)
                update_diag(first, update(idx(qi - 1), second, stats))
        if qi % 2 == 0:
            s_refs = (second, first)

        a1, a2 = acc[0], acc[1]
        ot = a1[:PAIR] / a1[PAIR:PAIR + 1] - lam * (a2[:PAIR] / a2[PAIR:PAIR + 1])
        ot = ot * lax.rsqrt(jnp.mean(ot * ot, axis=0, keepdims=True) + EPS)
        ot = ot * subln_ref[...] * (1.0 - lambda_init)
        o_ref[0, rows, :] = ot.T.astype(o_ref.dtype)


def _diff_attention(p3, slopes, lq1, lk1, lq2, lk2, subln_col, lambda_init, t):
    bsz, seq, _ = p3.shape
    vec = lambda n: pl.BlockSpec((1, n), lambda b, h: (0, 0))
    kernel = functools.partial(_diff_attn_kernel, t=t, lambda_init=lambda_init)
    return pl.pallas_call(
        kernel,
        grid=(bsz, A_HEADS),
        in_specs=[
            pl.BlockSpec(memory_space=pltpu.SMEM),
            pl.BlockSpec((1, seq, PAIR), lambda b, h: (b, 0, h)),
            pl.BlockSpec((1, seq, PAIR), lambda b, h: (b, 0, A_HEADS + h)),
            pl.BlockSpec((1, seq, PAIR), lambda b, h: (b, 0, 2 * A_HEADS + h)),
            vec(A_HEAD_DIM), vec(A_HEAD_DIM), vec(A_HEAD_DIM), vec(A_HEAD_DIM),
            pl.BlockSpec((PAIR, 1), lambda b, h: (0, 0)),
        ],
        out_specs=pl.BlockSpec((1, seq, PAIR), lambda b, h: (b, 0, h)),
        out_shape=jax.ShapeDtypeStruct((bsz, seq, A_COLS), BF16),
        scratch_shapes=[pltpu.VMEM((seq // t, PAIR + BF16_ROWS, t), BF16), pltpu.VMEM((t, t), F32),
                        pltpu.VMEM((t, PAIR), BF16),
                        pltpu.VMEM((2, t, t), F32), pltpu.VMEM((2, t, t), F32),
                        pltpu.VMEM((2, 1, t), F32), pltpu.VMEM((2, 1, t), F32),
                        pltpu.VMEM((2, 2, PAIR + BF16_ROWS, t), F32)],
        compiler_params=pltpu.CompilerParams(
            dimension_semantics=("arbitrary", "arbitrary"), vmem_limit_bytes=VMEM_LIMIT),
        name="diff_attn",
    )(slopes, p3, p3, p3, lq1, lk1, lq2, lk2, subln_col)


def _dilated_kernel(q_ref, k_ref, v_ref, kprev_ref, vprev_ref, o_ref, lse_ref, bias_ref, o_scr, lse_scr,
                    *maybe_order_scr, dil, slopes, tiles_per_seq):
    i = pl.program_id(0)
    blocks = q_ref.shape[2] // STEPS
    tm = o_ref.shape[0]
    two_pass = bool(maybe_order_scr)
    if two_pass:
        assert dil == SPLIT * SPLIT and blocks == 1
        order_scr, = maybe_order_scr

    @pl.when(i == 0)
    def _():
        qi = lax.broadcasted_iota(jnp.int32, (STEPS, 2 * STEPS), 0)
        kj = lax.broadcasted_iota(jnp.int32, (STEPS, 2 * STEPS), 1)
        step = qi + STEPS - kj
        window = (step >= 0) & (step <= STEPS)
        dist = (step * dil).astype(F32)
        for hd in range(B_HEADS):
            alibi = -float(slopes[hd]) * LOG2E * dist
            bias_ref[0, hd] = jnp.where(window & (kj >= STEPS), alibi, -jnp.inf)
            bias_ref[1, hd] = jnp.where(window, alibi, -jnp.inf)

    lane = lax.broadcasted_iota(jnp.int32, (STEPS, PAIR), 1)
    low = lane < B_HEAD_DIM
    low_keys = lax.broadcasted_iota(jnp.int32, (2 * STEPS, PAIR), 1) < B_HEAD_DIM
    ones_even = jnp.where(low_keys, 1.0, 0.0).astype(BF16)
    ones_odd = jnp.where(low_keys, 0.0, 1.0).astype(BF16)

    first_variant = (i % tiles_per_seq != 0).astype(jnp.int32)
    for u in range(dil * blocks):
        r, nl = divmod(u, blocks)
        variant = 1 if nl > 0 else first_variant
        cur = slice(nl * STEPS, (nl + 1) * STEPS)
        before = slice((nl - 1) * STEPS, nl * STEPS)
        q = q_ref[0, r, cur, :].astype(F32) * (B_HEAD_DIM ** -0.5 * LOG2E)
        k_before = k_ref[0, r, before, :] if nl > 0 else kprev_ref[0, r]
        v_before = v_ref[0, r, before, :] if nl > 0 else vprev_ref[0, r]
        kk = jnp.concatenate([k_before, k_ref[0, r, cur, :]], axis=0)
        vv = jnp.concatenate([v_before, v_ref[0, r, cur, :]], axis=0)
        start = nl * (STEPS * dil) + r
        if two_pass:
            rows = pl.ds((r % SPLIT) * (tm // SPLIT) + r // SPLIT, STEPS, stride=SPLIT)
        else:
            rows = pl.ds(start, STEPS, stride=dil) if dil > 1 else pl.ds(start, STEPS)
        for pr in range(B_HEADS // 2):
            cols = slice(pr * PAIR, (pr + 1) * PAIR)
            qp, kp = q[:, cols], kk[:, cols]
            vp = vv[:, cols].astype(F32)
            v_blocks = jnp.concatenate([
                jnp.concatenate([jnp.where(low_keys, vp, 0.0).astype(BF16), ones_even], axis=1),
                jnp.concatenate([jnp.where(low_keys, 0.0, vp).astype(BF16), ones_odd], axis=1)], axis=0)
            probs, maxes = [], []
            for par in range(2):
                qm = jnp.where(low, qp, 0.0) if par == 0 else jnp.where(low, 0.0, qp)
                s = _nt_dot(qm.astype(BF16), kp) + bias_ref[variant, 2 * pr + par]
                m = jnp.max(s, axis=-1, keepdims=True)
                probs.append(jnp.exp2(s - m).astype(BF16))
                maxes.append(m)
            o_den = jnp.dot(jnp.concatenate(probs, axis=1), v_blocks, preferred_element_type=F32)
            den = o_den[:, PAIR:]
            o_scr[pr, rows, :] = o_den[:, :PAIR] / den
            lse_scr[pr, rows, :] = jnp.where(low, maxes[0], maxes[1]) + jnp.log2(den)

    for src, dst in ((o_scr, o_ref), (lse_scr, lse_ref)):
        for pr in range(B_HEADS // 2):
            if two_pass:
                part = tm // SPLIT
                for r in range(SPLIT):
                    order_scr[pr, pl.ds(r, part, stride=SPLIT), :] = src[pr, r * part:(r + 1) * part, :]
            ordered = order_scr if two_pass else src
            dst[:, pr * PAIR:(pr + 1) * PAIR] = ordered[pr].astype(dst.dtype)


def _dilated_attention(pb, bsz, seq, dil):
    tiles, _, per_res, _ = pb.shape
    tm = dil * per_res
    tile_blk = (1, dil, per_res, B_COLS)
    prev_blk = (1, dil, STEPS, B_COLS)
    last = per_res // STEPS - 1
    prev = lambda i: jnp.maximum(i - 1, 0)
    kernel = functools.partial(_dilated_kernel, dil=dil, slopes=_alibi_slopes(B_HEADS),
                               tiles_per_seq=tiles // bsz)
    out_block = pl.BlockSpec((tm, B_COLS), lambda i: (i, 0))
    return pl.pallas_call(
        kernel,
        grid=(tiles,),
        in_specs=[
            pl.BlockSpec(tile_blk, lambda i: (i, 0, 0, 0)),
            pl.BlockSpec(tile_blk, lambda i: (i, 0, 0, 1)),
            pl.BlockSpec(tile_blk, lambda i: (i, 0, 0, 2)),
            pl.BlockSpec(prev_blk, lambda i: (prev(i), 0, last, 1)),
            pl.BlockSpec(prev_blk, lambda i: (prev(i), 0, last, 2)),
        ],
        out_specs=[out_block, out_block],
        out_shape=[jax.ShapeDtypeStruct((bsz * seq, B_COLS), BF16),
                   jax.ShapeDtypeStruct((bsz * seq, B_COLS), F32)],
        scratch_shapes=[pltpu.VMEM((2, B_HEADS, STEPS, 2 * STEPS), F32),
                        pltpu.VMEM((B_HEADS // 2, tm, PAIR), F32), pltpu.VMEM((B_HEADS // 2, tm, PAIR), F32)]
        + [pltpu.VMEM((B_HEADS // 2, tm, PAIR), F32)] * (dil == SPLIT * SPLIT),
        compiler_params=pltpu.CompilerParams(
            dimension_semantics=("arbitrary",), vmem_limit_bytes=VMEM_LIMIT),
        name=f"dilated_attn_d{dil}",
    )(pb, pb, pb, pb, pb)


MERGE_SLAB_ROWS = 256


def _merge_kernel(oa_ref, o0_ref, o1_ref, o2_ref, l0_ref, l1_ref, l2_ref, ga_ref, gb_ref, x_ref,
                  wpa_ref, wpb_ref, wo_ref, g_ref, b_ref, out_ref):
    slab = MERGE_SLAB_ROWS
    for part in range(out_ref.shape[0] // slab):
        rows = slice(part * slab, (part + 1) * slab)
        l0, l1, l2 = l0_ref[rows, :], l1_ref[rows, :], l2_ref[rows, :]
        mx = jnp.maximum(jnp.maximum(l0, l1), l2)
        e0, e1, e2 = jnp.exp2(l0 - mx), jnp.exp2(l1 - mx), jnp.exp2(l2 - mx)
        ob = (e0 * o0_ref[rows, :].astype(F32) + e1 * o1_ref[rows, :].astype(F32)
              + e2 * o2_ref[rows, :].astype(F32)) / (e0 + e1 + e2)
        ya = jnp.dot(oa_ref[rows, :], wpa_ref[...], preferred_element_type=F32)
        yb = jnp.dot(ob.astype(BF16), wpb_ref[...], preferred_element_type=F32)
        y = ga_ref[rows, :].astype(F32) * ya + gb_ref[rows, :].astype(F32) * yb
        z = ALPHA * x_ref[rows, :] + jnp.dot(y.astype(BF16), wo_ref[...], preferred_element_type=F32)
        out_ref[rows, :] = _layer_norm(z, g_ref[...], b_ref[...])


def _merge(oa, obs, lses, gates, x2, wpa, wpb, wo, g, b, tm):
    m = x2.shape[0]
    row = lambda w: pl.BlockSpec((tm, w), lambda i: (i, 0))
    full = lambda a: pl.BlockSpec(a.shape, lambda i: (0, 0))
    once = lambda a: pl.BlockSpec(a.shape, lambda i: (0, 0), pipeline_mode=pl.Buffered(1))
    return pl.pallas_call(
        _merge_kernel,
        grid=(m // tm,),
        in_specs=[row(A_COLS), row(B_COLS), row(B_COLS), row(B_COLS), row(B_COLS), row(B_COLS), row(B_COLS),
                  pl.BlockSpec((tm, D_MODEL), lambda i: (i, 0)), pl.BlockSpec((tm, D_MODEL), lambda i: (i, 1)),
                  row(D_MODEL), once(wpa), once(wpb), once(wo), full(g), full(b)],
        out_specs=row(D_MODEL),
        out_shape=jax.ShapeDtypeStruct((m, D_MODEL), F32),
        compiler_params=pltpu.CompilerParams(dimension_semantics=("parallel",), vmem_limit_bytes=VMEM_LIMIT),
        name="merge_ln1",
    )(oa, *obs, *lses, gates, gates, x2, wpa, wpb, wo, g, b)


HALO = BF16_ROWS


def _ffn_kernel(x_ref, halo_ref, wup_ref, wconv_ref, bconv_ref, wd_ref, g_ref, b_ref,
                out_ref, xcat_ref, acc_ref, *, tiles_per_seq):
    i = pl.program_id(0)
    halo = jnp.where(i % tiles_per_seq == 0, 0.0, halo_ref[...])
    xcat_ref[0:HALO, :] = halo.astype(BF16)
    xcat_ref[HALO:, :] = x_ref[...].astype(BF16)
    xcat = xcat_ref[...]

    def conv(col0, width):
        cols = slice(col0, col0 + width)
        hfull = jnp.dot(xcat, wup_ref[:, cols], preferred_element_type=F32)
        out = bconv_ref[:, cols] + wconv_ref[CONV_WIDTH - 1:CONV_WIDTH, cols] * hfull[HALO:, :]
        for back in range(1, CONV_WIDTH):
            shifted = pltpu.roll(hfull, back, axis=0)[HALO:, :]
            out = out + wconv_ref[CONV_WIDTH - 1 - back:CONV_WIDTH - back, cols] * shifted
        return out

    for col0, width in FF_CHUNKS:
        a = conv(col0, width)
        gv = conv(D_FF + col0, width)
        f = 0.5 * a * (1.0 + lax.erf(a * (2.0 ** -0.5))) * gv
        y = jnp.dot(f.astype(BF16), wd_ref[col0:col0 + width, :], preferred_element_type=F32)
        if col0 == 0:
            acc_ref[...] = y
        else:
            acc_ref[...] += y

    z = ALPHA * x_ref[...] + acc_ref[...]
    out_ref[...] = _layer_norm(z, g_ref[...], b_ref[...])


def _ffn(x1, w_up, w_conv, b_conv, w_down, g, b, seq, tm):
    m = x1.shape[0]
    halo_blocks = tm // HALO
    kernel = functools.partial(_ffn_kernel, tiles_per_seq=seq // tm)
    full = lambda a: pl.BlockSpec(a.shape, lambda i: (0, 0))
    once = lambda a: pl.BlockSpec(a.shape, lambda i: (0, 0), pipeline_mode=pl.Buffered(1))
    return pl.pallas_call(
        kernel,
        grid=(m // tm,),
        in_specs=[
            pl.BlockSpec((tm, D_MODEL), lambda i: (i, 0)),
            pl.BlockSpec((HALO, D_MODEL), lambda i: (jnp.maximum(i * halo_blocks - 1, 0), 0)),
            once(w_up), full(w_conv), full(b_conv), once(w_down), full(g), full(b),
        ],
        out_specs=pl.BlockSpec((tm, D_MODEL), lambda i: (i, 0)),
        out_shape=jax.ShapeDtypeStruct((m, D_MODEL), F32),
        scratch_shapes=[pltpu.VMEM((HALO + tm, D_MODEL), BF16), pltpu.VMEM((tm, D_MODEL), F32)],
        compiler_params=pltpu.CompilerParams(
            dimension_semantics=("parallel",), vmem_limit_bytes=VMEM_LIMIT,
            allow_input_fusion=[False, False, True, False, False, True, False, False]),
        name="ffn_ln2",
    )(x1, x1, w_up, w_conv, b_conv, w_down, g, b)


def kernel(x, w_in, b_gate, lambda_q1, lambda_k1, lambda_q2, lambda_k2, subln_w, w_pa, w_pb, w_o, ln1_g, ln1_b,
           w_up, w_conv, b_conv, w_down, ln2_g, ln2_b):
    bsz, seq, d = x.shape
    assert (seq, d) == (4096, D_MODEL) and w_in.shape[0] == DEPTH
    slopes_a = jnp.asarray(_alibi_slopes(A_HEADS))
    for l in range(DEPTH):
        lambda_init = 0.8 - 0.6 * math.exp(-0.3 * l)
        x2 = x.reshape(bsz * seq, d)
        wb = w_in[l].astype(BF16)
        proj_a, gates, xb = _projection_wide(x2, wb, b_gate[l][None, :], tm=PROJ_ROWS)
        proj_groups = _projection_groups(xb, wb, tm=PROJ_ROWS)
        oa = _diff_attention(proj_a.reshape(bsz, seq, 3 * A_COLS), slopes_a, lambda_q1[l][None],
                             lambda_k1[l][None], lambda_q2[l][None], lambda_k2[l][None], subln_w[l][:, None],
                             lambda_init, t=512)
        obs, lses = [], []
        for pb, (window, dil) in zip(proj_groups, B_PATTERNS):
            assert window // dil == STEPS
            o, lse = _dilated_attention(pb, bsz, seq, dil)
            obs.append(o)
            lses.append(lse)

        x1 = _merge(oa.reshape(bsz * seq, A_COLS), obs, lses, gates, x2,
                    w_pa[l].astype(BF16), w_pb[l].astype(BF16), w_o[l].astype(BF16),
                    ln1_g[l][None], ln1_b[l][None], tm=1024)
        x2 = _ffn(x1, w_up[l].astype(BF16), w_conv[l], b_conv[l][None], w_down[l].astype(BF16),
                  ln2_g[l][None], ln2_b[l][None], seq, tm=1024)
        x = x2.reshape(bsz, seq, d)
    return x
```
